```python
import math
import jax
import jax.numpy as jnp
from jax import lax
import numpy as np

D_MODEL = 1024
BATCH = 8
SEQ = 2048
DEPTH = 2
DEC_BATCH = 32
DEC_SEQ = 16
PAST_LEN = 1024

CHUNK = 64
Q_BLOCK = 128
CONV_W = 4
EPS = 1e-6
MIX_DIM = D_MODEL

FOX_HEADS = 8
FOX_HD = 64
FOX_DIM = FOX_HEADS * FOX_HD
FOX_SCALE = 1.0 / math.sqrt(FOX_HD)

SSD_HEADS = 8
SSD_HD = 64
SSD_INNER = SSD_HEADS * SSD_HD
SSD_GROUPS = 2
SSD_N = 128
SSD_CONV_DIM = SSD_INNER + 2 * SSD_GROUPS * SSD_N

RWKV_HEADS = 8
RWKV_HD = 64
RWKV_DIM = RWKV_HEADS * RWKV_HD
RWKV_DECAY_LORA = 64
RWKV_A_LORA = 64
RWKV_GATE_LORA = 128
RWKV_PROJ = 3 * RWKV_DIM + RWKV_DECAY_LORA + RWKV_A_LORA + RWKV_GATE_LORA
RWKV_GN_EPS = 64e-5

GDN_HEADS = 4
GDN_HD = 128
GDN_DIM = GDN_HEADS * GDN_HD
GDN_CONV_DIM = 3 * GDN_DIM

IN_EVEN = 3 * FOX_DIM + FOX_HEADS + SSD_INNER + SSD_CONV_DIM + SSD_HEADS
IN_ODD = RWKV_PROJ + 4 * GDN_DIM + 2 * GDN_HEADS
EVEN_SPLIT = (FOX_DIM, 2 * FOX_DIM, 3 * FOX_DIM, 3 * FOX_DIM + FOX_HEADS, 3 * FOX_DIM + FOX_HEADS + SSD_INNER, 3 * FOX_DIM + FOX_HEADS + SSD_INNER + SSD_CONV_DIM)
ODD_SPLIT = (RWKV_PROJ, RWKV_PROJ + GDN_DIM, RWKV_PROJ + 2 * GDN_DIM, RWKV_PROJ + 3 * GDN_DIM, RWKV_PROJ + 4 * GDN_DIM, RWKV_PROJ + 4 * GDN_DIM + GDN_HEADS)
RWKV_SPLIT = (RWKV_DIM, 2 * RWKV_DIM, 3 * RWKV_DIM, 3 * RWKV_DIM + RWKV_DECAY_LORA, 3 * RWKV_DIM + RWKV_DECAY_LORA + RWKV_A_LORA)

N_EXPERTS = 64
TOP_K = 8
N_EXPERT_GROUPS = 8
TOPK_GROUPS = 4
EXPERTS_PER_GROUP = N_EXPERTS // N_EXPERT_GROUPS
D_EXPERT = 256
D_SHARED = 256
ROUTED_SCALE = 2.5
MOE_BLOCK = 128

kernel_name = 'hybrid_stream_fox_ssd_rwkv7_gdn_moe_step'


def rms_norm(x, g):
    xf = x.astype(jnp.float32)
    y = xf * lax.rsqrt(jnp.mean(xf * xf, axis=-1, keepdims=True) + EPS)
    return (y * g.astype(jnp.float32)).astype(x.dtype)


def group_rms_norm(x, g, groups):
    shp = x.shape
    xg = x.astype(jnp.float32).reshape(*shp[:-1], groups, shp[-1] // groups)
    y = xg * lax.rsqrt(jnp.mean(xg * xg, axis=-1, keepdims=True) + EPS)
    return (y.reshape(shp) * g.astype(jnp.float32)).astype(x.dtype)


def l2_normalize(x):
    xf = x.astype(jnp.float32)
    return (xf * lax.rsqrt(jnp.sum(xf * xf, axis=-1, keepdims=True) + EPS)).astype(x.dtype)


def ada_rms_norm(x, c, g, w_mod, b_mod):
    m = (jax.nn.silu(c) @ w_mod + b_mod)[:, None, :]
    shift, scale, gate = jnp.split(m, 3, axis=-1)
    return rms_norm(x, g) * (1 + scale) + shift, gate


def causal_dwconv(x, prev, w):
    L = x.shape[1]
    full = jnp.concatenate([prev.astype(x.dtype), x], axis=1)
    y = full[:, 0:L] * w[0]
    for i in range(1, CONV_W):
        y = y + full[:, i:i + L] * w[i]
    return y, full[:, L:]


def forgetting_attention(q, k, v, logf, k_past, v_past, logf_past):
    b, L, H, Dh = q.shape
    n_past = k_past.shape[1]
    k_all = jnp.concatenate([k_past.astype(k.dtype), k], axis=1)
    v_all = jnp.concatenate([v_past.astype(v.dtype), v], axis=1)
    cum = jnp.cumsum(jnp.concatenate([logf_past.astype(jnp.float32), logf], axis=1), axis=1)
    cum_k = jnp.transpose(cum, (0, 2, 1))
    cum_q = cum[:, n_past:]
    kpos = jnp.arange(n_past + L)
    qpos = n_past + jnp.arange(L)

    def attend(blk):
        q_blk, c_blk, p_blk = blk
        s = jnp.einsum('bqhd,bkhd->bhqk', q_blk, k_all).astype(jnp.float32) * FOX_SCALE
        s = s + jnp.transpose(c_blk, (0, 2, 1))[..., None] - cum_k[:, :, None, :]
        s = jnp.where(kpos[None, :] <= p_blk[:, None], s, -jnp.inf)
        p = jax.nn.softmax(s, axis=-1)
        return jnp.einsum('bhqk,bkhd->bqhd', p.astype(v_all.dtype), v_all)

    qb = Q_BLOCK if L % Q_BLOCK == 0 else L
    nb = L // qb
    if nb == 1:
        return attend((q, cum_q, qpos))
    out = lax.map(attend, (jnp.swapaxes(q.reshape(b, nb, qb, H, Dh), 0, 1),
                           jnp.swapaxes(cum_q.reshape(b, nb, qb, H), 0, 1),
                           qpos.reshape(nb, qb)))
    return jnp.swapaxes(out, 0, 1).reshape(b, L, H, Dh)


def ssd_chunked(x, dt, a_neg, bm, cm, h0, chunk):
    f32 = jnp.float32
    b, L, H, P = x.shape
    nc = L // chunk
    rep = H // bm.shape[2]
    xc = x.astype(f32).reshape(b, nc, chunk, H, P)
    bc = jnp.repeat(bm.astype(f32), rep, axis=2).reshape(b, nc, chunk, H, SSD_N)
    cc = jnp.repeat(cm.astype(f32), rep, axis=2).reshape(b, nc, chunk, H, SSD_N)
    dtc = dt.reshape(b, nc, chunk, H)
    acum = jnp.cumsum(dtc * a_neg, axis=2)
    idx = jnp.arange(chunk)
    causal = (idx[:, None] >= idx[None, :])[None, None, :, :, None]
    seg = jnp.exp(jnp.where(causal, acum[:, :, :, None, :] - acum[:, :, None, :, :], -jnp.inf))
    xdt = xc * dtc[..., None]
    scores = jnp.einsum('bcthn,bcshn->bctsh', cc, bc) * seg
    y_diag = jnp.einsum('bctsh,bcshp->bcthp', scores, xdt)
    decay_end = jnp.exp(acum[:, :, -1:, :] - acum)
    states = jnp.einsum('bcshp,bcshn->bchpn', xdt * decay_end[..., None], bc)
    chunk_decay = jnp.exp(acum[:, :, -1, :])

    def step(h, inp):
        dec, st = inp
        return h * dec[:, :, None, None] + st, h

    h_fin, h_start = lax.scan(step, h0.astype(f32), (jnp.moveaxis(chunk_decay, 1, 0), jnp.moveaxis(states, 1, 0)))
    h_start = jnp.moveaxis(h_start, 0, 1)
    y_off = jnp.einsum('bcthn,bchpn->bcthp', cc * jnp.exp(acum)[..., None], h_start)
    return (y_diag + y_off).reshape(b, L, H, P), h_fin


def rwkv7_scan(r, w, k, v, a, bb, s0):
    def step(s, inp):
        rt, wt, kt, vt, at, bt = inp
        sa = jnp.einsum('bhvk,bhk->bhv', s, at)
        s = s * wt[:, :, None, :] + sa[..., None] * bt[:, :, None, :] + vt[..., None] * kt[:, :, None, :]
        return s, jnp.einsum('bhvk,bhk->bhv', s, rt)

    xs = tuple(jnp.swapaxes(t, 0, 1) for t in (r, w, k, v, a, bb))
    s_fin, out = lax.scan(step, s0.astype(jnp.float32), xs)
    return jnp.swapaxes(out, 0, 1), s_fin


def gated_delta_chunked(q, k, v, g, beta, s0, chunk):
    f32 = jnp.float32
    b, L, H, K = q.shape
    V = v.shape[-1]
    nc = L // chunk
    heads_first = lambda t: jnp.moveaxis(t.astype(f32).reshape(b, nc, chunk, H, t.shape[-1]), 3, 2)
    qh, kh, vh = heads_first(q), heads_first(k), heads_first(v)
    gam = jnp.cumsum(jnp.moveaxis(g.astype(f32).reshape(b, nc, chunk, H), 3, 2), axis=-1)
    bh = jnp.moveaxis(beta.astype(f32).reshape(b, nc, chunk, H), 3, 2)
    idx = jnp.arange(chunk)
    strict = idx[:, None] > idx[None, :]
    incl = idx[:, None] >= idx[None, :]
    diff = gam[..., :, None] - gam[..., None, :]
    dec_strict = jnp.exp(jnp.where(strict, diff, -jnp.inf))
    dec_incl = jnp.exp(jnp.where(incl, diff, -jnp.inf))
    a_mat = jnp.einsum('bchtk,bchsk->bchts', kh, kh) * dec_strict * bh[..., :, None]
    tri = a_mat + jnp.eye(chunk, dtype=f32)
    u = lax.linalg.triangular_solve(tri, vh * bh[..., None], left_side=True, lower=True, unit_diagonal=True)
    w = lax.linalg.triangular_solve(tri, kh * (bh * jnp.exp(gam))[..., None], left_side=True, lower=True, unit_diagonal=True)
    qk = jnp.einsum('bchtk,bchsk->bchts', qh, kh) * dec_incl
    qg = qh * jnp.exp(gam)[..., None]
    kd = kh * jnp.exp(gam[..., -1:] - gam)[..., None]
    cdec = jnp.exp(gam[..., -1])

    def step(s, inp):
        u_c, w_c, qk_c, qg_c, kd_c, cd_c = inp
        nu = u_c - jnp.einsum('bhtk,bhkv->bhtv', w_c, s)
        o = jnp.einsum('bhtk,bhkv->bhtv', qg_c, s) + jnp.einsum('bhts,bhsv->bhtv', qk_c, nu)
        s = s * cd_c[:, :, None, None] + jnp.einsum('bhsk,bhsv->bhkv', kd_c, nu)
        return s, o

    xs = tuple(jnp.moveaxis(t, 1, 0) for t in (u, w, qk, qg, kd, cdec))
    s_fin, o = lax.scan(step, s0.astype(f32), xs)
    return jnp.transpose(o, (1, 0, 3, 2, 4)).reshape(b, L, H, V), s_fin


def even_mixer(h, past, p):
    f32 = jnp.float32
    b, L, _ = h.shape
    proj = h @ p['w_in']
    q, k, v, f_raw, z, xbc, dt_raw = jnp.split(proj, EVEN_SPLIT, axis=-1)
    q = rms_norm(q.reshape(b, L, FOX_HEADS, FOX_HD), p['fox_q_g'])
    k = rms_norm(k.reshape(b, L, FOX_HEADS, FOX_HD), p['fox_k_g'])
    v = v.reshape(b, L, FOX_HEADS, FOX_HD)
    logf = jax.nn.log_sigmoid((f_raw + p['fox_f_b']).astype(f32))
    fox_out = forgetting_attention(q, k, v, logf, past['fox_k'], past['fox_v'], past['fox_logf'])
    xbc, ssd_conv_new = causal_dwconv(xbc, past['ssd_conv'], p['ssd_conv_w'])
    xbc = jax.nn.silu(xbc + p['ssd_conv_b'])
    xs, bm, cm = jnp.split(xbc, [SSD_INNER, SSD_INNER + SSD_GROUPS * SSD_N], axis=-1)
    xs = xs.reshape(b, L, SSD_HEADS, SSD_HD)
    dt = jax.nn.softplus((dt_raw + p['ssd_dt_bias']).astype(f32))
    a_neg = -jnp.exp(p['ssd_A_log'].astype(f32))
    chunk = CHUNK if L % CHUNK == 0 else L
    y, ssd_new = ssd_chunked(xs, dt, a_neg, bm.reshape(b, L, SSD_GROUPS, SSD_N), cm.reshape(b, L, SSD_GROUPS, SSD_N), past['ssd'], chunk)
    y = (y + p['ssd_D'].astype(f32)[:, None] * xs.astype(f32)).reshape(b, L, SSD_INNER)
    y = group_rms_norm(y * jax.nn.silu(z.astype(f32)), p['ssd_norm_g'], SSD_GROUPS).astype(h.dtype)
    out = jnp.concatenate([fox_out.reshape(b, L, FOX_DIM), y], axis=-1) @ p['w_out']
    new = dict(fox_k=k, fox_v=v, fox_logf=logf.astype(h.dtype), ssd_conv=ssd_conv_new, ssd=ssd_new.astype(h.dtype))
    return out, new


def odd_mixer(h, past, p):
    f32 = jnp.float32
    b, L, _ = h.shape
    proj = h @ p['w_in']
    rw, gq, gk, gv, gz, gb, ga = jnp.split(proj, ODD_SPLIT, axis=-1)
    prev = jnp.concatenate([past['rwkv_shift'][:, None, :].astype(rw.dtype), rw[:, :-1]], axis=1)
    mixed = rw + (prev - rw) * p['rwkv_mu']
    r, k, v, wl, al, gl = jnp.split(mixed, RWKV_SPLIT, axis=-1)
    w_log = -jax.nn.softplus(-(p['rwkv_w0'] + jnp.tanh(wl) @ p['rwkv_w2']).astype(f32)) - 0.5
    decay = jnp.exp(-jnp.exp(w_log))
    icl = jax.nn.sigmoid((p['rwkv_a0'] + al @ p['rwkv_a2']).astype(f32))
    out_gate = jax.nn.sigmoid(gl) @ p['rwkv_g2']
    heads = lambda t: t.reshape(b, L, RWKV_HEADS, RWKV_HD)
    kk = l2_normalize(heads((k * p['rwkv_k_k']).astype(f32)))
    k = k.astype(f32) * (1 + (icl - 1) * p['rwkv_k_a'].astype(f32))
    r_h, k_h, v_h = heads(r.astype(f32)), heads(k), heads(v.astype(f32))
    o, rwkv_new = rwkv7_scan(r_h, heads(decay), k_h, v_h, -kk, kk * heads(icl), past['rwkv'])
    mu = jnp.mean(o, axis=-1, keepdims=True)
    var = jnp.mean(jnp.square(o - mu), axis=-1, keepdims=True)
    o = ((o - mu) * lax.rsqrt(var + RWKV_GN_EPS)).reshape(b, L, RWKV_DIM) * p['rwkv_ln_g'] + p['rwkv_ln_b']
    bonus = jnp.sum(r_h * k_h * p['rwkv_r_k'].astype(f32), axis=-1, keepdims=True) * v_h
    o = ((o + bonus.reshape(b, L, RWKV_DIM)) * out_gate).astype(h.dtype)
    qkv, gdn_conv_new = causal_dwconv(jnp.concatenate([gq, gk, gv], axis=-1), past['gdn_conv'], p['gdn_conv_w'])
    qkv = jax.nn.silu(qkv)
    dq, dk, dv = jnp.split(qkv, [GDN_DIM, 2 * GDN_DIM], axis=-1)
    gh = lambda t: t.reshape(b, L, GDN_HEADS, GDN_HD)
    dq = l2_normalize(gh(dq).astype(f32)) * (GDN_HD ** -0.5)
    dk = l2_normalize(gh(dk).astype(f32))
    beta = jax.nn.sigmoid(gb.astype(f32))
    gdec = -jnp.exp(p['gdn_A_log'].astype(f32)) * jax.nn.softplus((ga + p['gdn_dt_bias']).astype(f32))
    chunk = CHUNK if L % CHUNK == 0 else L
    og, gdn_new = gated_delta_chunked(dq, dk, gh(dv).astype(f32), gdec, beta, past['gdn'], chunk)
    og = (rms_norm(og, p['gdn_norm_g']) * jax.nn.silu(gh(gz).astype(f32))).astype(h.dtype).reshape(b, L, GDN_DIM)
    out = jnp.concatenate([o, og], axis=-1) @ p['w_out']
    new = dict(rwkv_shift=rw[:, -1], rwkv=rwkv_new.astype(h.dtype), gdn_conv=gdn_conv_new, gdn=gdn_new.astype(h.dtype))
    return out, new


def swiglu(x, w1, w3, w2):
    return (jax.nn.silu(x @ w1) * (x @ w3)) @ w2


def routed_experts(x, idx, gate, w1, w3, w2):
    T, D = x.shape
    n_assign = T * TOP_K
    flat_e = idx.reshape(-1)
    flat_tok = jnp.arange(n_assign, dtype=jnp.int32) // TOP_K
    order = jnp.argsort(flat_e)
    e_sorted = flat_e[order]
    counts = jnp.bincount(flat_e, length=N_EXPERTS)
    padded = (counts + MOE_BLOCK - 1) // MOE_BLOCK * MOE_BLOCK
    start = jnp.cumsum(counts) - counts
    pad_end = jnp.cumsum(padded)
    pad_start = pad_end - padded
    dest = pad_start[e_sorted] + jnp.arange(n_assign, dtype=jnp.int32) - start[e_sorted]
    n_blocks = (n_assign + N_EXPERTS * (MOE_BLOCK - 1)) // MOE_BLOCK + 1
    cap = n_blocks * MOE_BLOCK
    slot_tok = jnp.full((cap,), T, jnp.int32).at[dest].set(flat_tok[order])
    slot_gate = jnp.zeros((cap,), x.dtype).at[dest].set(gate.reshape(-1)[order].astype(x.dtype))
    block_e = jnp.minimum(jnp.searchsorted(pad_end, jnp.arange(n_blocks) * MOE_BLOCK, side='right'), N_EXPERTS - 1)
    x_pad = jnp.concatenate([x, jnp.zeros((1, D), x.dtype)], axis=0)

    def block(args):
        tok, g, e = args
        return swiglu(x_pad[tok], w1[e], w3[e], w2[e]) * g[:, None]

    out = lax.map(block, (slot_tok.reshape(n_blocks, MOE_BLOCK), slot_gate.reshape(n_blocks, MOE_BLOCK), block_e))
    y = jnp.zeros((T + 1, D), out.dtype).at[slot_tok].add(out.reshape(cap, D))
    return y[:T]


def moe_ffn(h, router_w, router_b, w1, w3, w2, sw1, sw3, sw2):
    b, L, D = h.shape
    x = h.reshape(b * L, D)
    scores = jax.nn.sigmoid((x @ router_w).astype(jnp.float32))
    sel = scores + router_b.astype(jnp.float32)
    grp = jnp.sum(lax.top_k(sel.reshape(-1, N_EXPERT_GROUPS, EXPERTS_PER_GROUP), 2)[0], axis=-1)
    _, top_groups = lax.top_k(grp, TOPK_GROUPS)
    gmask = jnp.sum(jax.nn.one_hot(top_groups, N_EXPERT_GROUPS, dtype=jnp.float32), axis=1) > 0
    emask = jnp.repeat(gmask, EXPERTS_PER_GROUP, axis=1)
    _, idx = lax.top_k(jnp.where(emask, sel, -jnp.inf), TOP_K)
    gate = jnp.take_along_axis(scores, idx, axis=1)
    gate = gate / jnp.sum(gate, axis=-1, keepdims=True) * ROUTED_SCALE
    y = routed_experts(x, idx, gate, w1, w3, w2) + swiglu(x, sw1, sw3, sw2)
    return y.reshape(b, L, D)


def run_trunk(x, c, past_even, past_odd, W):
    new_even, new_odd = [], []
    for i in range(DEPTH):
        j = i // 2
        h, gate = ada_rms_norm(x, c, W['norm_g'][i, 0], W['ada_w'][i, 0], W['ada_b'][i, 0])
        if i % 2 == 0:
            mix, st = even_mixer(h, {n: a[j] for n, a in past_even.items()}, {n: a[j] for n, a in W['even'].items()})
            new_even.append(st)
        else:
            mix, st = odd_mixer(h, {n: a[j] for n, a in past_odd.items()}, {n: a[j] for n, a in W['odd'].items()})
            new_odd.append(st)
        x = x + gate * mix
        h, gate = ada_rms_norm(x, c, W['norm_g'][i, 1], W['ada_w'][i, 1], W['ada_b'][i, 1])
        x = x + gate * moe_ffn(h, W['router_w'][i], W['router_b'][i], W['moe_w1'][i], W['moe_w3'][i], W['moe_w2'][i],
                               W['shared_w1'][i], W['shared_w3'][i], W['shared_w2'][i])
    stack = lambda lst: {n: jnp.stack([d[n] for d in lst]) for n in lst[0]}
    return x, stack(new_even), stack(new_odd)


def setup_inputs(seed: int = 0) -> dict:
    key = jax.random.key(seed)
    keys = iter(jax.random.split(key, 64))
    f32 = jnp.float32

    def nrm(shape, scale=1.0):
        return jax.random.normal(next(keys), shape, f32) * scale

    def unif(shape, lo, hi):
        return jax.random.uniform(next(keys), shape, f32, lo, hi)

    def inv_softplus_dt(shape):
        dt = jnp.exp(unif(shape, math.log(1e-3), math.log(1e-1)))
        return dt + jnp.log(-jnp.expm1(-dt))

    ne, no = (DEPTH + 1) // 2, DEPTH // 2
    return {
        'x_prompt': nrm((BATCH, SEQ, D_MODEL)),
        'x_sample': nrm((DEC_BATCH, DEC_SEQ, D_MODEL)),
        'cache_fox_k': nrm((ne, DEC_BATCH, PAST_LEN, FOX_HEADS, FOX_HD)),
        'cache_fox_v': nrm((ne, DEC_BATCH, PAST_LEN, FOX_HEADS, FOX_HD)),
        'cache_fox_logf': jax.nn.log_sigmoid(nrm((ne, DEC_BATCH, PAST_LEN, FOX_HEADS)) + 2.5),
        'state_ssd_conv': nrm((ne, DEC_BATCH, CONV_W - 1, SSD_CONV_DIM)),
        'state_ssd': nrm((ne, DEC_BATCH, SSD_HEADS, SSD_HD, SSD_N), 0.1),
        'state_rwkv_shift': nrm((no, DEC_BATCH, RWKV_PROJ)),
        'state_rwkv': nrm((no, DEC_BATCH, RWKV_HEADS, RWKV_HD, RWKV_HD), 0.3),
        'state_gdn_conv': nrm((no, DEC_BATCH, CONV_W - 1, GDN_CONV_DIM)),
        'state_gdn': nrm((no, DEC_BATCH, GDN_HEADS, GDN_HD, GDN_HD), 0.1),
        'c_prompt': nrm((BATCH, D_MODEL)),
        'c_sample': nrm((DEC_BATCH, D_MODEL)),
        'ada_w': nrm((DEPTH, 2, D_MODEL, 3 * D_MODEL), 0.5 * D_MODEL ** -0.5),
        'ada_b': nrm((DEPTH, 2, 3 * D_MODEL), 0.02),
        'norm_g': 1.0 + nrm((DEPTH, 2, D_MODEL), 0.05),
        'even_w_in': nrm((ne, D_MODEL, IN_EVEN), D_MODEL ** -0.5),
        'even_w_out': nrm((ne, MIX_DIM, D_MODEL), MIX_DIM ** -0.5),
        'fox_q_g': 1.0 + nrm((ne, FOX_HD), 0.05),
        'fox_k_g': 1.0 + nrm((ne, FOX_HD), 0.05),
        'fox_f_b': unif((ne, FOX_HEADS), 1.0, 3.0),
        'ssd_conv_w': nrm((ne, CONV_W, SSD_CONV_DIM), CONV_W ** -0.5),
        'ssd_conv_b': nrm((ne, SSD_CONV_DIM), 0.02),
        'ssd_dt_bias': inv_softplus_dt((ne, SSD_HEADS)),
        'ssd_A_log': jnp.log(unif((ne, SSD_HEADS), 1.0, 16.0)),
        'ssd_D': 1.0 + nrm((ne, SSD_HEADS), 0.05),
        'ssd_norm_g': 1.0 + nrm((ne, SSD_INNER), 0.05),
        'odd_w_in': nrm((no, D_MODEL, IN_ODD), D_MODEL ** -0.5),
        'odd_w_out': nrm((no, MIX_DIM, D_MODEL), MIX_DIM ** -0.5),
        'rwkv_mu': unif((no, RWKV_PROJ), 0.0, 1.0),
        'rwkv_w0': unif((no, RWKV_DIM), -5.0, 0.5),
        'rwkv_w2': nrm((no, RWKV_DECAY_LORA, RWKV_DIM), 0.1 * RWKV_DECAY_LORA ** -0.5),
        'rwkv_a0': nrm((no, RWKV_DIM), 0.1),
        'rwkv_a2': nrm((no, RWKV_A_LORA, RWKV_DIM), 0.5 * RWKV_A_LORA ** -0.5),
        'rwkv_g2': nrm((no, RWKV_GATE_LORA, RWKV_DIM), RWKV_GATE_LORA ** -0.5),
        'rwkv_k_k': 0.85 + nrm((no, RWKV_DIM), 0.05),
        'rwkv_k_a': 1.0 + nrm((no, RWKV_DIM), 0.05),
        'rwkv_r_k': nrm((no, RWKV_HEADS, RWKV_HD), 0.1),
        'rwkv_ln_g': 1.0 + nrm((no, RWKV_DIM), 0.05),
        'rwkv_ln_b': nrm((no, RWKV_DIM), 0.02),
        'gdn_conv_w': nrm((no, CONV_W, GDN_CONV_DIM), CONV_W ** -0.5),
        'gdn_A_log': jnp.log(unif((no, GDN_HEADS), 1.0, 16.0)),
        'gdn_dt_bias': inv_softplus_dt((no, GDN_HEADS)),
        'gdn_norm_g': 1.0 + nrm((no, GDN_HD), 0.05),
        'router_w': nrm((DEPTH, D_MODEL, N_EXPERTS), D_MODEL ** -0.5),
        'router_b': nrm((DEPTH, N_EXPERTS), 0.01),
        'moe_w1': nrm((DEPTH, N_EXPERTS, D_MODEL, D_EXPERT), D_MODEL ** -0.5),
        'moe_w3': nrm((DEPTH, N_EXPERTS, D_MODEL, D_EXPERT), D_MODEL ** -0.5),
        'moe_w2': nrm((DEPTH, N_EXPERTS, D_EXPERT, D_MODEL), D_EXPERT ** -0.5),
        'shared_w1': nrm((DEPTH, D_MODEL, D_SHARED), D_MODEL ** -0.5),
        'shared_w3': nrm((DEPTH, D_MODEL, D_SHARED), D_MODEL ** -0.5),
        'shared_w2': nrm((DEPTH, D_SHARED, D_MODEL), D_SHARED ** -0.5),
    }


def reference(x_prompt, x_sample, cache_fox_k, cache_fox_v, cache_fox_logf, state_ssd_conv, state_ssd,
              state_rwkv_shift, state_rwkv, state_gdn_conv, state_gdn, c_prompt, c_sample,
              ada_w, ada_b, norm_g, even_w_in, even_w_out, fox_q_g, fox_k_g, fox_f_b,
              ssd_conv_w, ssd_conv_b, ssd_dt_bias, ssd_A_log, ssd_D, ssd_norm_g,
              odd_w_in, odd_w_out, rwkv_mu, rwkv_w0, rwkv_w2, rwkv_a0, rwkv_a2, rwkv_g2,
              rwkv_k_k, rwkv_k_a, rwkv_r_k, rwkv_ln_g, rwkv_ln_b,
              gdn_conv_w, gdn_A_log, gdn_dt_bias, gdn_norm_g,
              router_w, router_b, moe_w1, moe_w3, moe_w2, shared_w1, shared_w3, shared_w2):
    W = dict(
        ada_w=ada_w, ada_b=ada_b, norm_g=norm_g,
        router_w=router_w, router_b=router_b, moe_w1=moe_w1, moe_w3=moe_w3, moe_w2=moe_w2,
        shared_w1=shared_w1, shared_w3=shared_w3, shared_w2=shared_w2,
        even=dict(w_in=even_w_in, w_out=even_w_out, fox_q_g=fox_q_g, fox_k_g=fox_k_g, fox_f_b=fox_f_b,
                  ssd_conv_w=ssd_conv_w, ssd_conv_b=ssd_conv_b, ssd_dt_bias=ssd_dt_bias, ssd_A_log=ssd_A_log,
                  ssd_D=ssd_D, ssd_norm_g=ssd_norm_g),
        odd=dict(w_in=odd_w_in, w_out=odd_w_out, rwkv_mu=rwkv_mu, rwkv_w0=rwkv_w0, rwkv_w2=rwkv_w2,
                 rwkv_a0=rwkv_a0, rwkv_a2=rwkv_a2, rwkv_g2=rwkv_g2, rwkv_k_k=rwkv_k_k, rwkv_k_a=rwkv_k_a,
                 rwkv_r_k=rwkv_r_k, rwkv_ln_g=rwkv_ln_g, rwkv_ln_b=rwkv_ln_b, gdn_conv_w=gdn_conv_w,
                 gdn_A_log=gdn_A_log, gdn_dt_bias=gdn_dt_bias, gdn_norm_g=gdn_norm_g),
    )
    ne, no = cache_fox_k.shape[0], state_rwkv.shape[0]
    bp, fdt = x_prompt.shape[0], x_prompt.dtype
    prompt_even = dict(
        fox_k=jnp.zeros((ne, bp, 0, FOX_HEADS, FOX_HD), fdt),
        fox_v=jnp.zeros((ne, bp, 0, FOX_HEADS, FOX_HD), fdt),
        fox_logf=jnp.zeros((ne, bp, 0, FOX_HEADS), jnp.float32),
        ssd_conv=jnp.zeros((ne, bp, CONV_W - 1, SSD_CONV_DIM), fdt),
        ssd=jnp.zeros((ne, bp, SSD_HEADS, SSD_HD, SSD_N), jnp.float32))
    prompt_odd = dict(
        rwkv_shift=jnp.zeros((no, bp, RWKV_PROJ), fdt),
        rwkv=jnp.zeros((no, bp, RWKV_HEADS, RWKV_HD, RWKV_HD), jnp.float32),
        gdn_conv=jnp.zeros((no, bp, CONV_W - 1, GDN_CONV_DIM), fdt),
        gdn=jnp.zeros((no, bp, GDN_HEADS, GDN_HD, GDN_HD), jnp.float32))
    sample_even = dict(fox_k=cache_fox_k, fox_v=cache_fox_v, fox_logf=cache_fox_logf, ssd_conv=state_ssd_conv, ssd=state_ssd)
    sample_odd = dict(rwkv_shift=state_rwkv_shift, rwkv=state_rwkv, gdn_conv=state_gdn_conv, gdn=state_gdn)
    y_prompt, pe, po = run_trunk(x_prompt, c_prompt, prompt_even, prompt_odd, W)
    y_sample, se, so = run_trunk(x_sample, c_sample, sample_even, sample_odd, W)
    return (y_prompt, y_sample,
            pe['fox_k'], pe['fox_v'], pe['fox_logf'], pe['ssd_conv'], pe['ssd'],
            po['rwkv_shift'], po['rwkv'], po['gdn_conv'], po['gdn'],
            se['fox_k'], se['fox_v'], se['fox_logf'], se['ssd_conv'], se['ssd'],
            so['rwkv_shift'], so['rwkv'], so['gdn_conv'], so['gdn'])
```

```python
import functools
import math

import jax
import jax.numpy as jnp
from jax import lax
from jax.experimental import pallas as pl
from jax.experimental.pallas import tpu as pltpu

F32 = jnp.float32
BF16 = jnp.bfloat16
I32 = jnp.int32

D_MODEL = 1024
CONV_W = 4
EPS = 1e-6

FOX_HEADS = 8
FOX_HD = 64
FOX_DIM = FOX_HEADS * FOX_HD
FOX_SCALE = 1.0 / math.sqrt(FOX_HD)

SSD_HEADS = 8
SSD_HD = 64
SSD_INNER = SSD_HEADS * SSD_HD
SSD_GROUPS = 2
SSD_N = 128
SSD_CONV_DIM = SSD_INNER + 2 * SSD_GROUPS * SSD_N

RWKV_HEADS = 8
RWKV_HD = 64
RWKV_DIM = RWKV_HEADS * RWKV_HD
RWKV_DECAY_LORA = 64
RWKV_A_LORA = 64
RWKV_GATE_LORA = 128
RWKV_PROJ = 3 * RWKV_DIM + RWKV_DECAY_LORA + RWKV_A_LORA + RWKV_GATE_LORA
RWKV_GN_EPS = 64e-5

GDN_HEADS = 4
GDN_HD = 128
GDN_DIM = GDN_HEADS * GDN_HD
GDN_CONV_DIM = 3 * GDN_DIM

N_EXPERTS = 64
TOP_K = 8
N_EXPERT_GROUPS = 8
TOPK_GROUPS = 4
EXPERTS_PER_GROUP = N_EXPERTS // N_EXPERT_GROUPS
D_EXPERT = 256
D_SHARED = 256
ROUTED_SCALE = 2.5
MOE_BLOCK = 128

LANES = 128
VMEM_LIMIT_BYTES = 56 * 1024 * 1024


def _params(*semantics):
    return pltpu.CompilerParams(dimension_semantics=semantics, vmem_limit_bytes=VMEM_LIMIT_BYTES)


def _const_spec(a):
    return pl.BlockSpec(a.shape, lambda *_: (0,) * a.ndim, pipeline_mode=pl.Buffered(1))


def _mm(a, b):
    return jnp.dot(a.astype(BF16), b.astype(BF16), preferred_element_type=F32)


def _mm_nt(a, b):
    return lax.dot_general(a.astype(BF16), b.astype(BF16), (((1,), (1,)), ((), ())), preferred_element_type=F32)


def _mm_tn(a, b):
    return lax.dot_general(a.astype(BF16), b.astype(BF16), (((0,), (0,)), ((), ())), preferred_element_type=F32)


def _split_hi_lo(x):
    hi = x.astype(BF16)
    return hi, (x - hi.astype(F32)).astype(BF16)


def _dot3(dims, a, b):
    a_hi, a_lo = _split_hi_lo(a)
    b_hi, b_lo = _split_hi_lo(b)
    dot = lambda u, w: lax.dot_general(u, w, (dims, ((), ())), preferred_element_type=F32)
    return dot(a_hi, b_hi) + (dot(a_hi, b_lo) + dot(a_lo, b_hi))


_mm3 = functools.partial(_dot3, ((1,), (0,)))
_mm3_nt = functools.partial(_dot3, ((1,), (1,)))
_mm3_tn = functools.partial(_dot3, ((0,), (0,)))


def _cat3_lhs(x):
    hi, lo = _split_hi_lo(x)
    return jnp.concatenate([hi, hi, lo], axis=-1)


def _split_hi_lo_outside(w):
    w = w.astype(F32)
    hi = lax.bitcast_convert_type(lax.bitcast_convert_type(w, jnp.uint32) & jnp.uint32(0xFFFF0000), F32)
    return hi.astype(BF16), (w - hi).astype(BF16)


def _cat3_rhs(w, axis=0):
    hi, lo = _split_hi_lo_outside(w)
    return jnp.concatenate([hi, lo, hi], axis=axis)


def _sigmoid(x):
    return 1.0 / (1.0 + jnp.exp(-x))


def _silu(x):
    return x * _sigmoid(x)


def _softplus(x):
    return jnp.maximum(x, 0.0) + jnp.log1p(jnp.exp(-jnp.abs(x)))


def _cumsum_rows(x):
    n = x.shape[0]
    row = lax.broadcasted_iota(I32, x.shape, 0)
    s = 1
    while s < n:
        x = x + jnp.where(row >= s, pltpu.roll(x, s, 0), 0.0)
        s *= 2
    return x


def _ada_norm(x, g, shift, scale):
    y = x * lax.rsqrt(jnp.mean(x * x, axis=-1, keepdims=True) + EPS)
    return (y * g) * (1.0 + scale) + shift


def _ada_kernel(c_ref, w_ref, b_ref, o_ref):
    o_ref[0] = _mm3(_silu(c_ref[...]), w_ref[0]) + b_ref[0]


def _ada_modulation(c_all, ada_w, ada_b):
    n = c_all.shape[0]
    n_mod = ada_w.shape[0] * ada_w.shape[1]
    w = ada_w.reshape(n_mod, D_MODEL, 3 * D_MODEL)
    b = ada_b.reshape(n_mod, 1, 3 * D_MODEL)
    return pl.pallas_call(
        _ada_kernel,
        out_shape=jax.ShapeDtypeStruct((n_mod, n, 3 * D_MODEL), F32),
        grid=(n_mod, 3),
        in_specs=[
            pl.BlockSpec((n, D_MODEL), lambda i, j: (0, 0)),
            pl.BlockSpec((1, D_MODEL, D_MODEL), lambda i, j: (i, 0, j)),
            pl.BlockSpec((1, 1, D_MODEL), lambda i, j: (i, 0, j)),
        ],
        out_specs=pl.BlockSpec((1, n, D_MODEL), lambda i, j: (i, 0, j)),
        compiler_params=_params("parallel", "parallel"),
        name="ada_modulation",
    )(c_all, w, b)


ROW_TILE = 256

EV_Q, EV_K, EV_V, EV_Z, EV_XBC, EV_SMALL = 0, FOX_DIM, 2 * FOX_DIM, 3 * FOX_DIM, 3 * FOX_DIM + SSD_INNER, 3 * FOX_DIM + SSD_INNER + SSD_CONV_DIM
EV_COLS = EV_SMALL + LANES
OD_RW, OD_QKV, OD_Z, OD_SMALL = 0, RWKV_PROJ, RWKV_PROJ + GDN_CONV_DIM, RWKV_PROJ + GDN_CONV_DIM + GDN_DIM
OD_COLS = OD_SMALL + LANES


def _row_blocking(b, l):
    if l >= ROW_TILE:
        assert l % ROW_TILE == 0
        return 1, ROW_TILE
    assert ROW_TILE % l == 0 and b % (ROW_TILE // l) == 0 and l % 8 == 0
    return ROW_TILE // l, l


def _normed_rows(x_ref, g_ref, sh_ref, sc_ref):
    x = x_ref[...]
    h = _ada_norm(x, g_ref[...], sh_ref[...], sc_ref[...])
    return h.reshape(x.shape[0] * x.shape[1], x.shape[2])


def _head_rms(y, head_mean, gain):
    hi, lo = _split_hi_lo(y * y)
    ms = jnp.dot(hi, head_mean, preferred_element_type=F32) + jnp.dot(lo, head_mean, preferred_element_type=F32)
    return y * lax.rsqrt(ms + EPS) * gain


FOX_HD3 = 3 * FOX_HD


def _head_cat3(x, rhs):
    hi, lo = _split_hi_lo(x)
    pieces = []
    for h in range(FOX_HEADS):
        hs = slice(h * FOX_HD, (h + 1) * FOX_HD)
        pieces += [hi[:, hs], lo[:, hs], hi[:, hs]] if rhs else [hi[:, hs], hi[:, hs], lo[:, hs]]
    return jnp.concatenate(pieces, axis=-1)


def _inproj_even_kernel(x_ref, g_ref, sh_ref, sc_ref, w_ref, hm_ref, qg_ref, kg_ref, bias_ref,
                        q3_ref, k3_ref, k_ref, v_ref, z_ref, xbc_ref, small_ref):
    h = _cat3_lhs(_normed_rows(x_ref, g_ref, sh_ref, sc_ref))
    shp3 = lambda r: (x_ref.shape[0], x_ref.shape[1], r.shape[2])
    cols = lambda lo, hi: jnp.dot(h, w_ref[:, lo:hi], preferred_element_type=F32)
    hm = hm_ref[...]
    q = _head_rms(cols(EV_Q, EV_K), hm, qg_ref[...]) * FOX_SCALE
    q3_ref[...] = _head_cat3(q, rhs=False).reshape(shp3(q3_ref))
    k = _head_rms(cols(EV_K, EV_V), hm, kg_ref[...])
    k_ref[...] = k.reshape(shp3(k_ref))
    k3_ref[...] = _head_cat3(k, rhs=True).reshape(shp3(k3_ref))
    v_ref[...] = cols(EV_V, EV_Z).reshape(shp3(v_ref))
    z_ref[...] = cols(EV_Z, EV_XBC).reshape(shp3(z_ref))
    xbc_ref[...] = cols(EV_XBC, EV_SMALL).reshape(shp3(xbc_ref))
    t = cols(EV_SMALL, EV_COLS) + bias_ref[...]
    lane = lax.broadcasted_iota(I32, t.shape, 1)
    small = jnp.where(lane < FOX_HEADS, -_softplus(-t), jnp.where(lane < FOX_HEADS + SSD_HEADS, _softplus(t), 0.0))
    small_ref[...] = small.reshape(shp3(small_ref))


def _inproj_odd_kernel(x_ref, g_ref, sh_ref, sc_ref, w_ref, bias_ref, alog_ref,
                       rw_ref, qkv_ref, z_ref, small_ref):
    h = _normed_rows(x_ref, g_ref, sh_ref, sc_ref).astype(BF16)
    shp3 = lambda r: (x_ref.shape[0], x_ref.shape[1], r.shape[2])
    rw_ref[...] = jnp.dot(h, w_ref[:, OD_RW:OD_QKV], preferred_element_type=F32).reshape(shp3(rw_ref))
    qkv_ref[...] = jnp.dot(h, w_ref[:, OD_QKV:OD_Z], preferred_element_type=F32).reshape(shp3(qkv_ref))
    z_ref[...] = jnp.dot(h, w_ref[:, OD_Z:OD_SMALL], preferred_element_type=F32).reshape(shp3(z_ref))
    t = jnp.dot(h, w_ref[:, OD_SMALL:OD_COLS], preferred_element_type=F32) + bias_ref[...]
    lane = lax.broadcasted_iota(I32, t.shape, 1)
    small = jnp.where(lane < GDN_HEADS, _sigmoid(t),
                      jnp.where(lane < 2 * GDN_HEADS, -jnp.exp(alog_ref[...]) * _softplus(t), 0.0))
    small_ref[...] = small.reshape(shp3(small_ref))


def _inproj_call(kernel_fn, name, x, g, shift, scale, w, extras, out_cols, out_dtypes):
    b, l, d = x.shape
    kb, tl = _row_blocking(b, l)
    tok = lambda c: pl.BlockSpec((kb, tl, c), lambda i, j: (i, j, 0))
    per_batch = pl.BlockSpec((kb, 1, d), lambda i, j: (i, 0, 0))
    const = _const_spec
    return pl.pallas_call(
        kernel_fn,
        out_shape=[jax.ShapeDtypeStruct((b, l, c), dt) for c, dt in zip(out_cols, out_dtypes)],
        grid=(b // kb, l // tl),
        in_specs=[tok(d), const(g), per_batch, per_batch, const(w)] + [const(e) for e in extras],
        out_specs=[tok(c) for c in out_cols],
        compiler_params=_params("parallel", "parallel"),
        name=name,
    )(x, g, shift, scale, w, *extras)


def _pad_lanes(*pieces):
    v = jnp.concatenate([p.reshape(-1).astype(F32) for p in pieces])
    return jnp.pad(v, (0, LANES - v.shape[0])).reshape(1, LANES)


def _head_mean_matrix(dim, hd):
    r = jnp.arange(dim) // hd
    return jnp.where(r[:, None] == r[None, :], 1.0 / hd, 0.0).astype(BF16)


def _pack_even_w_in(w_in):
    q, k, v, f, z, xbc, dt = jnp.split(w_in, (FOX_DIM, 2 * FOX_DIM, 3 * FOX_DIM, 3 * FOX_DIM + FOX_HEADS,
                                              3 * FOX_DIM + FOX_HEADS + SSD_INNER,
                                              3 * FOX_DIM + FOX_HEADS + SSD_INNER + SSD_CONV_DIM), axis=1)
    pad = jnp.zeros((w_in.shape[0], LANES - FOX_HEADS - SSD_HEADS), w_in.dtype)
    return _cat3_rhs(jnp.concatenate([q, k, v, z, xbc, f, dt, pad], axis=1))


def _pack_odd_w_in(w_in):
    main, small = w_in[:, :OD_SMALL], w_in[:, OD_SMALL:]
    pad = jnp.zeros((w_in.shape[0], LANES - 2 * GDN_HEADS), w_in.dtype)
    return jnp.concatenate([main, small, pad], axis=1).astype(BF16)


def _split3_bf16(x):
    c1 = x.astype(BF16)
    r1 = x - c1.astype(F32)
    c2 = r1.astype(BF16)
    c3 = (r1 - c2.astype(F32)).astype(BF16)
    return c1, c2, c3


def _rows_to_lanes(sel, x):
    nt = lambda a, b: lax.dot_general(a, b, (((1,), (1,)), ((), ())), preferred_element_type=F32)
    c1, c2, c3 = _split3_bf16(x)
    return (nt(sel, c1) + nt(sel, c2)) + nt(sel, c3)


def _fox_cum_kernel(*refs, n_past):
    if n_past:
        past_ref, small_ref, sel_ref, cum_ref, cumt_ref = refs
        seq = jnp.concatenate([past_ref[0], small_ref[0]], axis=0)
    else:
        small_ref, sel_ref, cum_ref, cumt_ref = refs
        seq = small_ref[0]
    cum = _cumsum_rows(seq)
    cum_ref[0] = cum[n_past:]
    cumt_ref[0] = _rows_to_lanes(sel_ref[...], cum)


def _fox_cumulative(small, past_logf):
    b, l, _ = small.shape
    n_past = 0 if past_logf is None else past_logf.shape[1]
    sel = jnp.eye(FOX_HEADS, LANES, dtype=BF16)
    ins, specs = [], []
    if n_past:
        ins.append(past_logf)
        specs.append(pl.BlockSpec((1, n_past, LANES), lambda i: (i, 0, 0)))
    ins += [small, sel]
    specs += [pl.BlockSpec((1, l, LANES), lambda i: (i, 0, 0)), pl.BlockSpec(sel.shape, lambda i: (0, 0))]
    return pl.pallas_call(
        functools.partial(_fox_cum_kernel, n_past=n_past),
        out_shape=[jax.ShapeDtypeStruct((b, l, LANES), F32), jax.ShapeDtypeStruct((b, FOX_HEADS, n_past + l), F32)],
        grid=(b,),
        in_specs=specs,
        out_specs=[pl.BlockSpec((1, l, LANES), lambda i: (i, 0, 0)),
                   pl.BlockSpec((1, FOX_HEADS, n_past + l), lambda i: (i, 0, 0))],
        compiler_params=_params("parallel"),
        name="fox_cumulative",
    )(*ins)


FOX_TILE = 256


def _fox_flash_kernel(q_ref, k_ref, v_ref, cq_ref, ck_ref, o_ref, m_sc, l_sc, acc_sc):
    qi, kj = pl.program_id(1), pl.program_id(2)

    @pl.when(kj == 0)
    def _():
        m_sc[...] = jnp.full(m_sc.shape, -jnp.inf, F32)
        l_sc[...] = jnp.zeros(l_sc.shape, F32)
        acc_sc[...] = jnp.zeros(acc_sc.shape, F32)

    @pl.when(kj <= qi)
    def _():
        q3, k3 = q_ref[0], k_ref[0]
        v_hi, v_lo = _split_hi_lo(v_ref[0])
        cq, ck = cq_ref[0], ck_ref[0]
        row = lax.broadcasted_iota(I32, (FOX_TILE, FOX_TILE), 0)
        col = lax.broadcasted_iota(I32, (FOX_TILE, FOX_TILE), 1)
        visible = jnp.logical_or(col <= row, kj < qi)
        for h in range(FOX_HEADS):
            hs = slice(h * FOX_HD, (h + 1) * FOX_HD)
            hs3 = slice(h * FOX_HD3, (h + 1) * FOX_HD3)
            s = _scores3(q3[:, hs3], k3[:, hs3]) + (cq[:, h:h + 1] - ck[h:h + 1, :])
            s = jnp.where(visible, s, -jnp.inf)
            m_prev = m_sc[h]
            m_new = jnp.maximum(m_prev, jnp.max(s, axis=-1, keepdims=True))
            p = jnp.exp(s - m_new)
            alpha = jnp.exp(m_prev - m_new)
            l_sc[h] = alpha * l_sc[h] + jnp.sum(p, axis=-1, keepdims=True)
            acc_sc[h] = alpha * acc_sc[h] + _pv3(p, v_hi[:, hs], v_lo[:, hs])
            m_sc[h] = m_new

    @pl.when(kj == qi)
    def _():
        o_ref[0] = jnp.concatenate([acc_sc[h] / l_sc[h] for h in range(FOX_HEADS)], axis=-1)


def _scores3(q3, k3):
    return lax.dot_general(q3, k3, (((1,), (1,)), ((), ())), preferred_element_type=F32)


def _pv3(p, v_hi, v_lo):
    p_hi, p_lo = _split_hi_lo(p)
    dot = lambda a, b: jnp.dot(a, b, preferred_element_type=F32)
    return dot(p_hi, v_hi) + (dot(p_hi, v_lo) + dot(p_lo, v_hi))


def _fox_attention_prompt(q3, k3, v, cum, cumt):
    b, l, _ = v.shape
    n = l // FOX_TILE
    qspec = lambda c: pl.BlockSpec((1, FOX_TILE, c), lambda i, j, t: (i, j, 0))
    kspec = lambda c: pl.BlockSpec((1, FOX_TILE, c), lambda i, j, t: (i, jnp.minimum(t, j), 0))
    return pl.pallas_call(
        _fox_flash_kernel,
        out_shape=jax.ShapeDtypeStruct((b, l, FOX_DIM), F32),
        grid=(b, n, n),
        in_specs=[qspec(FOX_HEADS * FOX_HD3), kspec(FOX_HEADS * FOX_HD3), kspec(FOX_DIM), qspec(LANES),
                  pl.BlockSpec((1, FOX_HEADS, FOX_TILE), lambda i, j, t: (i, 0, jnp.minimum(t, j)))],
        out_specs=qspec(FOX_DIM),
        scratch_shapes=[pltpu.VMEM((FOX_HEADS, FOX_TILE, 1), F32), pltpu.VMEM((FOX_HEADS, FOX_TILE, 1), F32),
                        pltpu.VMEM((FOX_HEADS, FOX_TILE, FOX_HD), F32)],
        compiler_params=_params("parallel", "parallel", "arbitrary"),
        name="fox_flash",
    )(q3, k3, v, cum, cumt)


def _fox_decode_kernel(q_ref, kp_ref, vp_ref, kn_ref, vn_ref, cq_ref, ck_ref, o_ref):
    n_past, l = kp_ref.shape[1], q_ref.shape[1]
    q3, kn3 = q_ref[0], kn_ref[0]
    kp3 = _head_cat3(kp_ref[0], rhs=True)
    vp_hi, vp_lo = _split_hi_lo(vp_ref[0])
    vn_hi, vn_lo = _split_hi_lo(vn_ref[0])
    cq, ck = cq_ref[0], ck_ref[0]
    row = lax.broadcasted_iota(I32, (l, l), 0)
    col = lax.broadcasted_iota(I32, (l, l), 1)
    outs = []
    for h in range(FOX_HEADS):
        hs = slice(h * FOX_HD, (h + 1) * FOX_HD)
        hs3 = slice(h * FOX_HD3, (h + 1) * FOX_HD3)
        cqh = cq[:, h:h + 1]
        s_past = _scores3(q3[:, hs3], kp3[:, hs3]) + (cqh - ck[h:h + 1, :n_past])
        s_new = _scores3(q3[:, hs3], kn3[:, hs3]) + (cqh - ck[h:h + 1, n_past:])
        s_new = jnp.where(col <= row, s_new, -jnp.inf)
        m = jnp.maximum(jnp.max(s_past, axis=-1, keepdims=True), jnp.max(s_new, axis=-1, keepdims=True))
        p_past, p_new = jnp.exp(s_past - m), jnp.exp(s_new - m)
        denom = jnp.sum(p_past, axis=-1, keepdims=True) + jnp.sum(p_new, axis=-1, keepdims=True)
        outs.append((_pv3(p_past, vp_hi[:, hs], vp_lo[:, hs]) + _pv3(p_new, vn_hi[:, hs], vn_lo[:, hs])) / denom)
    o_ref[0] = jnp.concatenate(outs, axis=-1)


def _fox_attention_decode(q3, k3, v, k_past, v_past, cum, cumt):
    b, l, _ = v.shape
    n_past = k_past.shape[1]
    new = lambda c: pl.BlockSpec((1, l, c), lambda i: (i, 0, 0))
    past = pl.BlockSpec((1, n_past, FOX_DIM), lambda i: (i, 0, 0))
    return pl.pallas_call(
        _fox_decode_kernel,
        out_shape=jax.ShapeDtypeStruct((b, l, FOX_DIM), F32),
        grid=(b,),
        in_specs=[new(FOX_HEADS * FOX_HD3), past, past, new(FOX_HEADS * FOX_HD3), new(FOX_DIM), new(LANES),
                  pl.BlockSpec((1, FOX_HEADS, n_past + l), lambda i: (i, 0, 0))],
        out_specs=new(FOX_DIM),
        compiler_params=_params("parallel"),
        name="fox_decode",
    )(q3, k_past, v_past, k3, v, cum, cumt)


CONV_TAIL = 8


def _causal_conv(x, tail, w):
    q = x.shape[0]
    row = lax.broadcasted_iota(I32, (CONV_TAIL, x.shape[1]), 0)
    acc = x * w[CONV_W - 1:CONV_W]
    for s in range(1, CONV_W):
        xr = pltpu.roll(x, s, 0)
        head = jnp.where(row < s, pltpu.roll(tail, s, 0), xr[:CONV_TAIL])
        shifted = head if q == CONV_TAIL else jnp.concatenate([head, xr[CONV_TAIL:]], axis=0)
        acc = acc + shifted * w[CONV_W - 1 - s:CONV_W - s]
    return acc


def _conv_tail_from_state(state):
    return jnp.pad(state, ((0, 0), (CONV_TAIL - (CONV_W - 1), 0), (0, 0)))


SSD_CHUNK = 128
SSD_DT_LANE = FOX_HEADS


def _ssd_kernel(xbc_ref, z_ref, small_ref, tail0_ref, h0_ref, cw_ref, cb_ref, alog_ref, drep_ref, ng_ref, sel_ref,
                y_ref, tail_ref, hout_ref, tail_sc, h_sc):
    c = pl.program_id(1)

    @pl.when(c == 0)
    def _():
        tail_sc[...] = tail0_ref[0]
        h_sc[...] = h0_ref[0]

    x = xbc_ref[0]
    q = x.shape[0]
    act = _silu(_causal_conv(x, tail_sc[...], cw_ref[...]) + cb_ref[...])
    tail_sc[...] = x[q - CONV_TAIL:]
    xs = act[:, :SSD_INNER]
    bm = act[:, SSD_INNER:SSD_INNER + SSD_GROUPS * SSD_N]
    cm = act[:, SSD_INNER + SSD_GROUPS * SSD_N:]

    small = small_ref[0]
    lane = lax.broadcasted_iota(I32, small.shape, 1)
    is_dt = jnp.logical_and(lane >= SSD_DT_LANE, lane < SSD_DT_LANE + SSD_HEADS)
    da = jnp.where(is_dt, small * -jnp.exp(alog_ref[...]), 0.0)
    acum = _cumsum_rows(da)
    acum_t = _rows_to_lanes(sel_ref[...], acum)
    trow = lax.broadcasted_iota(I32, (q, q), 0)
    tcol = lax.broadcasted_iota(I32, (q, q), 1)
    heads_per_group = SSD_HEADS // SSD_GROUPS
    ys = []
    for g in range(SSD_GROUPS):
        bg = bm[:, g * SSD_N:(g + 1) * SSD_N]
        cg = cm[:, g * SSD_N:(g + 1) * SSD_N]
        cb = _mm3_nt(cg, bg)
        for h in range(g * heads_per_group, (g + 1) * heads_per_group):
            a_col = acum[:, SSD_DT_LANE + h:SSD_DT_LANE + h + 1]
            dt_col = small[:, SSD_DT_LANE + h:SSD_DT_LANE + h + 1]
            a_last = a_col[q - 1:q]
            seg = jnp.exp(jnp.where(trow >= tcol, a_col - acum_t[h:h + 1, :], -jnp.inf))
            xdt = xs[:, h * SSD_HD:(h + 1) * SSD_HD] * dt_col
            state = h_sc[h]
            ys.append(_mm3(cb * seg, xdt) + jnp.exp(a_col) * _mm3_nt(cg, state))
            h_sc[h] = state * jnp.exp(a_last) + _mm3_tn(xdt * jnp.exp(a_last - a_col), bg)
    y = jnp.concatenate(ys, axis=-1) + drep_ref[...] * xs
    y = y * _silu(z_ref[0])
    gw = SSD_INNER // SSD_GROUPS
    normed = []
    for g in range(SSD_GROUPS):
        yg = y[:, g * gw:(g + 1) * gw]
        normed.append(yg * lax.rsqrt(jnp.mean(yg * yg, axis=-1, keepdims=True) + EPS))
    y_ref[0] = (jnp.concatenate(normed, axis=-1) * ng_ref[...]).astype(y_ref.dtype)

    @pl.when(c == pl.num_programs(1) - 1)
    def _():
        tail_ref[0] = tail_sc[...]
        hout_ref[0] = h_sc[...]


def _ssd_mixer(xbc, z, small, conv_state, h0, conv_w, conv_b, a_log, d_skip, norm_g):
    b, l, _ = xbc.shape
    q = min(SSD_CHUNK, l)
    tail0 = _conv_tail_from_state(conv_state)
    alog = _pad_lanes(jnp.zeros(SSD_DT_LANE), a_log)
    drep = jnp.repeat(d_skip.astype(F32), SSD_HD)[None]
    sel = jnp.eye(SSD_HEADS, LANES, k=SSD_DT_LANE, dtype=BF16)
    tok = lambda c: pl.BlockSpec((1, q, c), lambda i, j: (i, j, 0))
    per_b = lambda a: pl.BlockSpec((1,) + a.shape[1:], lambda i, j: (i,) + (0,) * (a.ndim - 1))
    const = _const_spec
    consts = [conv_w, conv_b[None], alog, drep, norm_g[None], sel]
    y, tail, h_new = pl.pallas_call(
        _ssd_kernel,
        out_shape=[jax.ShapeDtypeStruct((b, l, SSD_INNER), F32),
                   jax.ShapeDtypeStruct((b, CONV_TAIL, SSD_CONV_DIM), F32),
                   jax.ShapeDtypeStruct(h0.shape, F32)],
        grid=(b, l // q),
        in_specs=[tok(SSD_CONV_DIM), tok(SSD_INNER), tok(LANES), per_b(tail0), per_b(h0)] + [const(a) for a in consts],
        out_specs=[tok(SSD_INNER), per_b(tail0), per_b(h0)],
        scratch_shapes=[pltpu.VMEM((CONV_TAIL, SSD_CONV_DIM), F32), pltpu.VMEM(h0.shape[1:], F32)],
        compiler_params=_params("parallel", "arbitrary"),
        name="ssd_mixer",
    )(xbc, z, small, tail0, h0, *consts)
    return y, tail[:, CONV_TAIL - (CONV_W - 1):], h_new


REC_CHUNK = 64


def _unit_lower_inverse(n):
    c = n.shape[0]
    eye = jnp.where(lax.broadcasted_iota(I32, (c, c), 0) == lax.broadcasted_iota(I32, (c, c), 1), 1.0, 0.0)
    inv = eye + n
    power = n
    span = 2
    while span < c:
        power = _mm(power, power)
        inv = inv + _mm(inv, power)
        span *= 2
    return inv


def _shift_rows_by_one(x, tail):
    q = x.shape[0]
    row = lax.broadcasted_iota(I32, (CONV_TAIL, x.shape[1]), 0)
    xr = pltpu.roll(x, 1, 0)
    head = jnp.where(row < 1, pltpu.roll(tail, 1, 0), xr[:CONV_TAIL])
    return head if q == CONV_TAIL else jnp.concatenate([head, xr[CONV_TAIL:]], axis=0)


RW_R, RW_K, RW_V, RW_LORA, RW_GATE = 0, RWKV_DIM, 2 * RWKV_DIM, 3 * RWKV_DIM, 3 * RWKV_DIM + RWKV_DECAY_LORA + RWKV_A_LORA


def _rwkv_kernel(rw_ref, tail0_ref, s0_ref, mu_ref, w0_ref, w2_ref, a0_ref, a2_ref, g2_ref, kk_ref, ka_ref, rk_ref,
                 lng_ref, lnb_ref, hsum_ref, o_ref, sout_ref, tail_sc, s_sc):
    c = pl.program_id(1)

    @pl.when(c == 0)
    def _():
        tail_sc[...] = tail0_ref[0]
        s_sc[...] = s0_ref[0]

    x = rw_ref[0]
    q = x.shape[0]
    mixed = x + (_shift_rows_by_one(x, tail_sc[...]) - x) * mu_ref[...]
    tail_sc[...] = x[q - CONV_TAIL:]
    r = mixed[:, RW_R:RW_K]
    k = mixed[:, RW_K:RW_V]
    v = mixed[:, RW_V:RW_LORA]
    lora = mixed[:, RW_LORA:RW_GATE]
    w_log = -_softplus(-(w0_ref[...] + _mm(jnp.tanh(lora), w2_ref[...]))) - 0.5
    logw = -jnp.exp(w_log)
    icl = _sigmoid(a0_ref[...] + _mm(lora, a2_ref[...]))
    out_gate = _mm(_sigmoid(mixed[:, RW_GATE:]), g2_ref[...])
    kk = k * kk_ref[...]
    kk = kk * lax.rsqrt(_mm(kk * kk, hsum_ref[...]) + EPS)
    k2 = k * (1.0 + (icl - 1.0) * ka_ref[...])
    cum = _cumsum_rows(logw)
    w_run = jnp.exp(cum)
    w_inv = jnp.exp(-cum)
    rt = r * w_run
    at = -kk * jnp.exp(cum - logw)
    bt = kk * icl * w_inv
    kt = k2 * w_inv
    trow = lax.broadcasted_iota(I32, (q, q), 0)
    tcol = lax.broadcasted_iota(I32, (q, q), 1)
    strict, incl = trow > tcol, trow >= tcol
    outs = []
    for h in range(RWKV_HEADS):
        hs = slice(h * RWKV_HD, (h + 1) * RWKV_HD)
        ar = jnp.concatenate([at[:, hs], rt[:, hs]], axis=0)
        bk = jnp.concatenate([bt[:, hs], kt[:, hs]], axis=0)
        gram = _mm_nt(ar, bk)
        a_ab = jnp.where(strict, gram[:q, :q], 0.0)
        a_ak = jnp.where(strict, gram[:q, q:], 0.0)
        a_r = jnp.concatenate([jnp.where(incl, gram[q:, :q], 0.0), jnp.where(incl, gram[q:, q:], 0.0)], axis=1)
        s0 = s_sc[h]
        ars = _mm_nt(ar, s0)
        vh = v[:, hs]
        u = _mm(_unit_lower_inverse(a_ab), ars[:q] + _mm(a_ak, vh))
        uv = jnp.concatenate([u, vh], axis=0)
        o = ars[q:] + _mm(a_r, uv)
        s_sc[h] = (s0 + _mm_tn(uv, bk)) * w_run[q - 1:q, hs]
        mean = jnp.mean(o, axis=-1, keepdims=True)
        var = jnp.mean(jnp.square(o - mean), axis=-1, keepdims=True)
        o = (o - mean) * lax.rsqrt(var + RWKV_GN_EPS) * lng_ref[:, hs] + lnb_ref[:, hs]
        bonus = jnp.sum(r[:, hs] * k2[:, hs] * rk_ref[:, hs], axis=-1, keepdims=True) * vh
        outs.append(o + bonus)
    o_ref[0] = (jnp.concatenate(outs, axis=-1) * out_gate).astype(o_ref.dtype)

    @pl.when(c == pl.num_programs(1) - 1)
    def _():
        sout_ref[0] = s_sc[...]


def _rwkv_mixer(rw, shift_state, s0, p):
    b, l, _ = rw.shape
    q = min(REC_CHUNK, l)
    tail0 = jnp.pad(shift_state[:, None, :], ((0, 0), (CONV_TAIL - 1, 0), (0, 0)))
    zeros = jnp.zeros((RWKV_DECAY_LORA, RWKV_DIM), F32)
    consts = [p['rwkv_mu'][None], p['rwkv_w0'][None],
              jnp.concatenate([p['rwkv_w2'], zeros], axis=0).astype(BF16), p['rwkv_a0'][None],
              jnp.concatenate([zeros, p['rwkv_a2']], axis=0).astype(BF16), p['rwkv_g2'].astype(BF16),
              p['rwkv_k_k'][None], p['rwkv_k_a'][None], p['rwkv_r_k'].reshape(1, RWKV_DIM),
              p['rwkv_ln_g'][None], p['rwkv_ln_b'][None], _head_mean_matrix(RWKV_DIM, RWKV_HD) * RWKV_HD]
    tok = lambda c: pl.BlockSpec((1, q, c), lambda i, j: (i, j, 0))
    per_b = lambda a: pl.BlockSpec((1,) + a.shape[1:], lambda i, j: (i,) + (0,) * (a.ndim - 1))
    const = _const_spec
    return pl.pallas_call(
        _rwkv_kernel,
        out_shape=[jax.ShapeDtypeStruct((b, l, RWKV_DIM), BF16), jax.ShapeDtypeStruct(s0.shape, F32)],
        grid=(b, l // q),
        in_specs=[tok(RWKV_PROJ), per_b(tail0), per_b(s0)] + [const(a) for a in consts],
        out_specs=[tok(RWKV_DIM), per_b(s0)],
        scratch_shapes=[pltpu.VMEM((CONV_TAIL, RWKV_PROJ), F32), pltpu.VMEM(s0.shape[1:], F32)],
        compiler_params=_params("parallel", "arbitrary"),
        name="rwkv_mixer",
    )(rw, tail0, s0, *consts)


GDN_BETA_LANE, GDN_G_LANE = 0, GDN_HEADS


def _l2norm(x):
    return x * lax.rsqrt(jnp.sum(x * x, axis=-1, keepdims=True) + EPS)


def _gdn_kernel(qkv_ref, z_ref, small_ref, tail0_ref, s0_ref, cw_ref, ng_ref, sel_ref,
                o_ref, tail_ref, sout_ref, tail_sc, s_sc):
    c = pl.program_id(1)

    @pl.when(c == 0)
    def _():
        tail_sc[...] = tail0_ref[0]
        s_sc[...] = s0_ref[0]

    x = qkv_ref[0]
    q = x.shape[0]
    act = _silu(_causal_conv(x, tail_sc[...], cw_ref[...]))
    tail_sc[...] = x[q - CONV_TAIL:]
    small = small_ref[0]
    z = z_ref[0]
    lane = lax.broadcasted_iota(I32, small.shape, 1)
    is_g = jnp.logical_and(lane >= GDN_G_LANE, lane < GDN_G_LANE + GDN_HEADS)
    gam = _cumsum_rows(jnp.where(is_g, small, 0.0))
    gam_t = _rows_to_lanes(sel_ref[...], gam)
    trow = lax.broadcasted_iota(I32, (q, q), 0)
    tcol = lax.broadcasted_iota(I32, (q, q), 1)
    outs = []
    for h in range(GDN_HEADS):
        hs = slice(h * GDN_HD, (h + 1) * GDN_HD)
        qh = _l2norm(act[:, hs]) * (GDN_HD ** -0.5)
        kh = _l2norm(act[:, GDN_DIM + h * GDN_HD:GDN_DIM + (h + 1) * GDN_HD])
        vh = act[:, 2 * GDN_DIM + h * GDN_HD:2 * GDN_DIM + (h + 1) * GDN_HD]
        beta = small[:, GDN_BETA_LANE + h:GDN_BETA_LANE + h + 1]
        g_col = gam[:, GDN_G_LANE + h:GDN_G_LANE + h + 1]
        g_last = g_col[q - 1:q]
        diff = g_col - gam_t[h:h + 1, :]
        a_mat = _mm_nt(kh, kh) * jnp.exp(jnp.where(trow > tcol, diff, -jnp.inf)) * beta
        eg = jnp.exp(g_col)
        uw = _mm(_unit_lower_inverse(-a_mat), jnp.concatenate([vh * beta, kh * (beta * eg)], axis=1))
        qk = _mm_nt(qh, kh) * jnp.exp(jnp.where(trow >= tcol, diff, -jnp.inf))
        state = s_sc[h]
        nu = uw[:, :GDN_HD] - _mm(uw[:, GDN_HD:], state)
        o = _mm(qh * eg, state) + _mm(qk, nu)
        s_sc[h] = state * jnp.exp(g_last) + _mm_tn(kh * jnp.exp(g_last - g_col), nu)
        o = o * lax.rsqrt(jnp.mean(o * o, axis=-1, keepdims=True) + EPS) * ng_ref[...]
        outs.append(o * _silu(z[:, hs]))
    o_ref[0] = jnp.concatenate(outs, axis=-1).astype(o_ref.dtype)

    @pl.when(c == pl.num_programs(1) - 1)
    def _():
        tail_ref[0] = tail_sc[...]
        sout_ref[0] = s_sc[...]


def _gdn_mixer(qkv, z, small, conv_state, s0, conv_w, norm_g):
    b, l, _ = qkv.shape
    q = min(REC_CHUNK, l)
    tail0 = _conv_tail_from_state(conv_state)
    sel = jnp.eye(GDN_HEADS, LANES, k=GDN_G_LANE, dtype=BF16)
    consts = [conv_w, norm_g[None], sel]
    tok = lambda c: pl.BlockSpec((1, q, c), lambda i, j: (i, j, 0))
    per_b = lambda a: pl.BlockSpec((1,) + a.shape[1:], lambda i, j: (i,) + (0,) * (a.ndim - 1))
    const = _const_spec
    o, tail, s_new = pl.pallas_call(
        _gdn_kernel,
        out_shape=[jax.ShapeDtypeStruct((b, l, GDN_DIM), BF16),
                   jax.ShapeDtypeStruct((b, CONV_TAIL, GDN_CONV_DIM), F32),
                   jax.ShapeDtypeStruct(s0.shape, F32)],
        grid=(b, l // q),
        in_specs=[tok(GDN_CONV_DIM), tok(GDN_DIM), tok(LANES), per_b(tail0), per_b(s0)] + [const(a) for a in consts],
        out_specs=[tok(GDN_DIM), per_b(tail0), per_b(s0)],
        scratch_shapes=[pltpu.VMEM((CONV_TAIL, GDN_CONV_DIM), F32), pltpu.VMEM(s0.shape[1:], F32)],
        compiler_params=_params("parallel", "arbitrary"),
        name="gdn_mixer",
    )(qkv, z, small, tail0, s0, *consts)
    return o, tail[:, CONV_TAIL - (CONV_W - 1):], s_new


def _outproj_kernel(a_ref, b_ref, w_ref, x_ref, gate_ref, g_ref, sh_ref, sc_ref, rwt_ref,
                    xo_ref, h_ref, logit_ref, *, split_inputs):
    kb, tl, d = x_ref.shape
    rows = kb * tl
    a = a_ref[...].reshape(rows, a_ref.shape[2])
    b = b_ref[...].reshape(rows, b_ref.shape[2])
    lhs = jnp.concatenate([_cat3_lhs(a), _cat3_lhs(b)] if split_inputs else [a.astype(BF16), b.astype(BF16)], axis=-1)
    mix = jnp.dot(lhs, w_ref[...], preferred_element_type=F32)
    x_new = x_ref[...] + gate_ref[...] * mix.reshape(kb, tl, d)
    xo_ref[...] = x_new
    h = _ada_norm(x_new, g_ref[...], sh_ref[...], sc_ref[...])
    h_ref[...] = h.astype(h_ref.dtype)
    logit_ref[...] = lax.dot_general(rwt_ref[...], _cat3_lhs(h.reshape(rows, d)), (((1,), (1,)), ((), ())),
                                     preferred_element_type=F32)


def _outproj_residual(a, b_half, w_out, x, gate, g2, shift2, scale2, router_w, split_inputs):
    b, l, d = x.shape
    kb, tl = _row_blocking(b, l)
    half = a.shape[2]
    pack = _cat3_rhs if split_inputs else (lambda w: w.astype(BF16))
    w = jnp.concatenate([pack(w_out[:half]), pack(w_out[half:])], axis=0)
    rwt = _cat3_rhs(router_w.T, axis=1)
    n_l = l // tl
    tok = lambda c: pl.BlockSpec((kb, tl, c), lambda i, j: (i, j, 0))
    per_batch = pl.BlockSpec((kb, 1, d), lambda i, j: (i, 0, 0))
    const = _const_spec
    return pl.pallas_call(
        functools.partial(_outproj_kernel, split_inputs=split_inputs),
        out_shape=[jax.ShapeDtypeStruct((b, l, d), F32), jax.ShapeDtypeStruct((b, l, d), BF16),
                   jax.ShapeDtypeStruct((N_EXPERTS, b * l), F32)],
        grid=(b // kb, n_l),
        in_specs=[tok(half), tok(b_half.shape[2]), const(w), tok(d), per_batch, const(g2), per_batch, per_batch, const(rwt)],
        out_specs=[tok(d), tok(d), pl.BlockSpec((N_EXPERTS, kb * tl), lambda i, j: (0, i * n_l + j))],
        compiler_params=_params("parallel", "parallel"),
        name="outproj_residual",
    )(a, b_half, w, x, gate, g2, shift2, scale2, rwt)


def _beats(a, b, a_first):
    return jnp.where(a > b, 1, jnp.where(a == b, a_first, 0))


def _router_kernel(logit_ref, bias_ref, idx_ref, gate_ref):
    tm = logit_ref.shape[1]
    scores = _sigmoid(logit_ref[...])
    sel = scores + bias_ref[...]
    ng, eg = N_EXPERT_GROUPS, EXPERTS_PER_GROUP
    sub = lax.broadcasted_iota(I32, (eg, tm), 0)
    groups, grp = [], []
    for g in range(ng):
        xg = sel[g * eg:(g + 1) * eg]
        m1 = jnp.max(xg, axis=0, keepdims=True)
        first = jnp.min(jnp.where(xg == m1, sub, eg), axis=0, keepdims=True)
        m2 = jnp.max(jnp.where(sub == first, -jnp.inf, xg), axis=0, keepdims=True)
        groups.append(xg)
        grp.append(m1 + m2)
    masked = []
    for g in range(ng):
        rank = sum(_beats(grp[o], grp[g], 1 if o < g else 0) for o in range(ng) if o != g)
        masked.append(jnp.where(rank < TOPK_GROUPS, groups[g], -jnp.inf))
    selm = jnp.concatenate(masked, axis=0)
    erow = lax.broadcasted_iota(I32, (N_EXPERTS, tm), 0)
    rank = jnp.zeros((N_EXPERTS, tm), I32)
    for o in range(N_EXPERTS):
        rank = rank + _beats(selm[o:o + 1], selm, jnp.where(erow > o, 1, 0))
    idx, gates = [], []
    for k in range(TOP_K):
        pick = rank == k
        idx.append(jnp.sum(jnp.where(pick, erow, 0), axis=0, keepdims=True))
        gates.append(jnp.sum(jnp.where(pick, scores, 0.0), axis=0, keepdims=True))
    idx_ref[...] = jnp.concatenate(idx, axis=0)
    gate = jnp.concatenate(gates, axis=0)
    gate_ref[...] = gate / jnp.sum(gate, axis=0, keepdims=True) * ROUTED_SCALE


def _router(logits, router_b):
    t = logits.shape[1]
    bias = router_b.astype(F32)[:, None]
    tile = lambda rows: pl.BlockSpec((rows, ROW_TILE), lambda i: (0, i))
    return pl.pallas_call(
        _router_kernel,
        out_shape=[jax.ShapeDtypeStruct((TOP_K, t), I32), jax.ShapeDtypeStruct((TOP_K, t), F32)],
        grid=(t // ROW_TILE,),
        in_specs=[tile(N_EXPERTS), _const_spec(bias)],
        out_specs=[tile(TOP_K), tile(TOP_K)],
        compiler_params=_params("parallel"),
        name="moe_router",
    )(logits, bias)


def _swiglu(x, w1, w3, w2):
    hid = _silu(jnp.dot(x, w1, preferred_element_type=F32)) * jnp.dot(x, w3, preferred_element_type=F32)
    return jnp.dot(hid.astype(BF16), w2, preferred_element_type=F32)


def _expert_block_kernel(be_ref, x_ref, g_ref, w1_ref, w3_ref, w2_ref, o_ref):
    o_ref[...] = _swiglu(x_ref[...], w1_ref[0], w3_ref[0], w2_ref[0]) * g_ref[...]


def _expert_blocks(xs, slot_gate, block_e, w1, w3, w2):
    cap, d = xs.shape
    n_blocks = cap // MOE_BLOCK
    rows = lambda c: pl.BlockSpec((MOE_BLOCK, c), lambda i, be: (i, 0))
    return pl.pallas_call(
        _expert_block_kernel,
        out_shape=jax.ShapeDtypeStruct((cap, d), F32),
        grid_spec=pltpu.PrefetchScalarGridSpec(
            num_scalar_prefetch=1,
            grid=(n_blocks,),
            in_specs=[rows(d), rows(1),
                      pl.BlockSpec((1, d, D_EXPERT), lambda i, be: (be[i], 0, 0)),
                      pl.BlockSpec((1, d, D_EXPERT), lambda i, be: (be[i], 0, 0)),
                      pl.BlockSpec((1, D_EXPERT, d), lambda i, be: (be[i], 0, 0))],
            out_specs=rows(d)),
        compiler_params=_params("arbitrary"),
        name="moe_expert_blocks",
    )(block_e, xs, slot_gate, w1, w3, w2)


def _shared_combine_kernel(h_ref, y_ref, x_ref, gate_ref, w1_ref, w3_ref, w2_ref, o_ref):
    kb, tl, d = x_ref.shape
    h = h_ref[...].reshape(kb * tl, d)
    shared = _swiglu(h, w1_ref[...], w3_ref[...], w2_ref[...])
    o_ref[...] = x_ref[...] + gate_ref[...] * (y_ref[...] + shared.reshape(kb, tl, d))


def _shared_combine(h, y_routed, x, gate, sw1, sw3, sw2):
    b, l, d = x.shape
    kb, tl = _row_blocking(b, l)
    tok = pl.BlockSpec((kb, tl, d), lambda i, j: (i, j, 0))
    per_batch = pl.BlockSpec((kb, 1, d), lambda i, j: (i, 0, 0))
    const = _const_spec
    return pl.pallas_call(
        _shared_combine_kernel,
        out_shape=jax.ShapeDtypeStruct((b, l, d), F32),
        grid=(b // kb, l // tl),
        in_specs=[tok, tok, tok, per_batch, const(sw1), const(sw3), const(sw2)],
        out_specs=tok,
        compiler_params=_params("parallel", "parallel"),
        name="moe_shared_combine",
    )(h, y_routed, x, gate, sw1, sw3, sw2)


def _dispatch_plan(idx, gate, n_tok):
    n_assign = n_tok * TOP_K
    flat_e = idx.reshape(-1)
    flat_tok = jnp.arange(n_assign, dtype=I32) // TOP_K
    order = jnp.argsort(flat_e)
    e_sorted = flat_e[order]
    counts = jnp.bincount(flat_e, length=N_EXPERTS)
    padded = (counts + MOE_BLOCK - 1) // MOE_BLOCK * MOE_BLOCK
    start = jnp.cumsum(counts) - counts
    pad_end = jnp.cumsum(padded)
    pad_start = pad_end - padded
    dest = pad_start[e_sorted] + jnp.arange(n_assign, dtype=I32) - start[e_sorted]
    n_blocks = (n_assign + N_EXPERTS * (MOE_BLOCK - 1)) // MOE_BLOCK + 1
    cap = n_blocks * MOE_BLOCK
    slot_tok = jnp.full((cap,), n_tok, I32).at[dest].set(flat_tok[order])
    slot_gate = jnp.zeros((cap,), F32).at[dest].set(gate.reshape(-1)[order])
    block_e = jnp.minimum(jnp.searchsorted(pad_end, jnp.arange(n_blocks) * MOE_BLOCK, side='right'), N_EXPERTS - 1)
    return slot_tok, slot_gate, block_e.astype(I32)


def _moe(h, logits, x, gate_mod, router_b, w1, w3, w2, sw1, sw3, sw2):
    b, l, d = x.shape
    t = b * l
    h2 = h.reshape(t, d)
    idx_t, gate_t = _router(logits, router_b)
    slot_tok, slot_gate, block_e = _dispatch_plan(idx_t.T, gate_t.T, t)
    xs = jnp.concatenate([h2, jnp.zeros((1, d), h2.dtype)], axis=0)[slot_tok]
    out = _expert_blocks(xs, slot_gate[:, None], block_e, w1, w3, w2)
    y = jnp.zeros((t + 1, d), F32).at[slot_tok].add(out)[:t]
    return _shared_combine(h, y.reshape(b, l, d), x, gate_mod, sw1, sw3, sw2)


def _split_mod(m):
    m = m[:, None, :]
    return m[..., :D_MODEL], m[..., D_MODEL:2 * D_MODEL], m[..., 2 * D_MODEL:]


def _even_layer(x, mod_a, mod_b, past, w):
    b, l, _ = x.shape
    shift, scale, gate = _split_mod(mod_a)
    q3, k3, k, v, z, xbc, small = _inproj_call(
        _inproj_even_kernel, "inproj_even", x, w['norm_g0'], shift, scale, w['w_in'],
        [w['head_mean'], w['q_gain'], w['k_gain'], w['small_bias']],
        [FOX_HEADS * FOX_HD3, FOX_HEADS * FOX_HD3, FOX_DIM, FOX_DIM, SSD_INNER, SSD_CONV_DIM, LANES],
        [BF16, BF16, F32, F32, F32, F32, F32])
    if past['fox_k'] is None:
        cum, cumt = _fox_cumulative(small, None)
        fox = _fox_attention_prompt(q3, k3, v, cum, cumt)
    else:
        n_past = past['fox_k'].shape[1]
        past_logf = jnp.pad(past['fox_logf'].astype(F32), ((0, 0), (0, 0), (0, LANES - FOX_HEADS)))
        cum, cumt = _fox_cumulative(small, past_logf)
        fox = _fox_attention_decode(q3, k3, v, past['fox_k'].reshape(b, n_past, FOX_DIM),
                                    past['fox_v'].reshape(b, n_past, FOX_DIM), cum, cumt)
    y, conv_new, ssd_new = _ssd_mixer(xbc, z, small, past['ssd_conv'], past['ssd'], w['ssd_conv_w'], w['ssd_conv_b'],
                                      w['ssd_A_log'], w['ssd_D'], w['ssd_norm_g'])
    shift2, scale2, _ = _split_mod(mod_b)
    x, h, logits = _outproj_residual(fox, y, w['w_out'], x, gate, w['norm_g1'], shift2, scale2, w['router_w'],
                                     split_inputs=True)
    new = dict(fox_k=k.reshape(b, l, FOX_HEADS, FOX_HD), fox_v=v.reshape(b, l, FOX_HEADS, FOX_HD),
               fox_logf=small[..., :FOX_HEADS], ssd_conv=conv_new, ssd=ssd_new)
    return x, h, logits, new


def _odd_layer(x, mod_a, mod_b, past, w):
    shift, scale, gate = _split_mod(mod_a)
    rw, qkv, gz, small = _inproj_call(
        _inproj_odd_kernel, "inproj_odd", x, w['norm_g0'], shift, scale, w['w_in'],
        [w['small_bias'], w['small_alog']], [RWKV_PROJ, GDN_CONV_DIM, GDN_DIM, LANES], [F32, F32, F32, F32])
    o_rwkv, rwkv_new = _rwkv_mixer(rw, past['rwkv_shift'], past['rwkv'], w['p'])
    o_gdn, gconv_new, gdn_new = _gdn_mixer(qkv, gz, small, past['gdn_conv'], past['gdn'], w['p']['gdn_conv_w'],
                                           w['p']['gdn_norm_g'])
    shift2, scale2, _ = _split_mod(mod_b)
    x, h, logits = _outproj_residual(o_rwkv, o_gdn, w['w_out'], x, gate, w['norm_g1'], shift2, scale2, w['router_w'],
                                     split_inputs=False)
    new = dict(rwkv_shift=rw[:, -1], rwkv=rwkv_new, gdn_conv=gconv_new, gdn=gdn_new)
    return x, h, logits, new


def _run_trunk(x, mods, past_even, past_odd, layers):
    new_even, new_odd = [], []
    for i, w in enumerate(layers):
        j = i // 2
        if i % 2 == 0:
            past = {n: (None if a is None else a[j]) for n, a in past_even.items()}
            x, h, logits, st = _even_layer(x, mods[2 * i], mods[2 * i + 1], past, w)
            new_even.append(st)
        else:
            x, h, logits, st = _odd_layer(x, mods[2 * i], mods[2 * i + 1], {n: a[j] for n, a in past_odd.items()}, w)
            new_odd.append(st)
        x = _moe(h, logits, x, _split_mod(mods[2 * i + 1])[2], *w['moe'])
    stack = lambda lst: {n: jnp.stack([d[n] for d in lst]) for n in lst[0]}
    return x, stack(new_even), stack(new_odd)


def kernel(x_prompt, x_sample, cache_fox_k, cache_fox_v, cache_fox_logf, state_ssd_conv, state_ssd, state_rwkv_shift, state_rwkv, state_gdn_conv, state_gdn, c_prompt, c_sample, ada_w, ada_b, norm_g, even_w_in, even_w_out, fox_q_g, fox_k_g, fox_f_b, ssd_conv_w, ssd_conv_b, ssd_dt_bias, ssd_A_log, ssd_D, ssd_norm_g, odd_w_in, odd_w_out, rwkv_mu, rwkv_w0, rwkv_w2, rwkv_a0, rwkv_a2, rwkv_g2, rwkv_k_k, rwkv_k_a, rwkv_r_k, rwkv_ln_g, rwkv_ln_b, gdn_conv_w, gdn_A_log, gdn_dt_bias, gdn_norm_g, router_w, router_b, moe_w1, moe_w3, moe_w2, shared_w1, shared_w3, shared_w2):
    depth = ada_w.shape[0]
    ne, no = cache_fox_k.shape[0], state_rwkv.shape[0]
    bp, bs = x_prompt.shape[0], x_sample.shape[0]
    odd_params = dict(rwkv_mu=rwkv_mu, rwkv_w0=rwkv_w0, rwkv_w2=rwkv_w2, rwkv_a0=rwkv_a0, rwkv_a2=rwkv_a2, rwkv_g2=rwkv_g2,
                      rwkv_k_k=rwkv_k_k, rwkv_k_a=rwkv_k_a, rwkv_r_k=rwkv_r_k, rwkv_ln_g=rwkv_ln_g, rwkv_ln_b=rwkv_ln_b,
                      gdn_conv_w=gdn_conv_w, gdn_norm_g=gdn_norm_g)
    layers = []
    for i in range(depth):
        j = i // 2
        w = dict(norm_g0=norm_g[i, 0][None], norm_g1=norm_g[i, 1][None], router_w=router_w[i],
                 moe=(router_b[i], moe_w1[i].astype(BF16), moe_w3[i].astype(BF16), moe_w2[i].astype(BF16),
                      shared_w1[i].astype(BF16), shared_w3[i].astype(BF16), shared_w2[i].astype(BF16)))
        if i % 2 == 0:
            w.update(w_in=_pack_even_w_in(even_w_in[j]), w_out=even_w_out[j],
                     head_mean=_head_mean_matrix(FOX_DIM, FOX_HD),
                     q_gain=jnp.tile(fox_q_g[j], FOX_HEADS)[None], k_gain=jnp.tile(fox_k_g[j], FOX_HEADS)[None],
                     small_bias=_pad_lanes(fox_f_b[j], ssd_dt_bias[j]),
                     ssd_conv_w=ssd_conv_w[j], ssd_conv_b=ssd_conv_b[j], ssd_A_log=ssd_A_log[j], ssd_D=ssd_D[j],
                     ssd_norm_g=ssd_norm_g[j])
        else:
            w.update(w_in=_pack_odd_w_in(odd_w_in[j]), w_out=odd_w_out[j],
                     small_bias=_pad_lanes(jnp.zeros(GDN_HEADS), gdn_dt_bias[j]),
                     small_alog=_pad_lanes(jnp.zeros(GDN_HEADS), gdn_A_log[j]),
                     p={n: a[j] for n, a in odd_params.items()})
        layers.append(w)

    mods = _ada_modulation(jnp.concatenate([c_prompt, c_sample], axis=0), ada_w, ada_b)
    mods_p, mods_s = mods[:, :bp], mods[:, bp:]

    prompt_even = dict(fox_k=None, fox_v=None, fox_logf=None,
                       ssd_conv=jnp.zeros((ne, bp) + state_ssd_conv.shape[2:], F32),
                       ssd=jnp.zeros((ne, bp) + state_ssd.shape[2:], F32))
    prompt_odd = dict(rwkv_shift=jnp.zeros((no, bp) + state_rwkv_shift.shape[2:], F32),
                      rwkv=jnp.zeros((no, bp) + state_rwkv.shape[2:], F32),
                      gdn_conv=jnp.zeros((no, bp) + state_gdn_conv.shape[2:], F32),
                      gdn=jnp.zeros((no, bp) + state_gdn.shape[2:], F32))
    sample_even = dict(fox_k=cache_fox_k, fox_v=cache_fox_v, fox_logf=cache_fox_logf, ssd_conv=state_ssd_conv, ssd=state_ssd)
    sample_odd = dict(rwkv_shift=state_rwkv_shift, rwkv=state_rwkv, gdn_conv=state_gdn_conv, gdn=state_gdn)
    y_prompt, pe, po = _run_trunk(x_prompt, mods_p, prompt_even, prompt_odd, layers)
    y_sample, se, so = _run_trunk(x_sample, mods_s, sample_even, sample_odd, layers)
    return (y_prompt, y_sample,
            pe['fox_k'], pe['fox_v'], pe['fox_logf'], pe['ssd_conv'], pe['ssd'],
            po['rwkv_shift'], po['rwkv'], po['gdn_conv'], po['gdn'],
            se['fox_k'], se['fox_v'], se['fox_logf'], se['ssd_conv'], se['ssd'],
            so['rwkv_shift'], so['rwkv'], so['gdn_conv'], so['gdn'])
```

```python
import functools
import math

import jax
import jax.numpy as jnp
from jax import lax
from jax.experimental import pallas as pl
from jax.experimental.pallas import tpu as pltpu

F32 = jnp.float32
BF16 = jnp.bfloat16
I32 = jnp.int32

D_MODEL = 1024
CONV_W = 4
EPS = 1e-6

FOX_HEADS = 8
FOX_HD = 64
FOX_DIM = FOX_HEADS * FOX_HD
FOX_SCALE = 1.0 / math.sqrt(FOX_HD)

SSD_HEADS = 8
SSD_HD = 64
SSD_INNER = SSD_HEADS * SSD_HD
SSD_GROUPS = 2
SSD_N = 128
SSD_CONV_DIM = SSD_INNER + 2 * SSD_GROUPS * SSD_N

RWKV_HEADS = 8
RWKV_HD = 64
RWKV_DIM = RWKV_HEADS * RWKV_HD
RWKV_DECAY_LORA = 64
RWKV_A_LORA = 64
RWKV_GATE_LORA = 128
RWKV_PROJ = 3 * RWKV_DIM + RWKV_DECAY_LORA + RWKV_A_LORA + RWKV_GATE_LORA
RWKV_GN_EPS = 64e-5

GDN_HEADS = 4
GDN_HD = 128
GDN_DIM = GDN_HEADS * GDN_HD
GDN_CONV_DIM = 3 * GDN_DIM

N_EXPERTS = 64
TOP_K = 8
N_EXPERT_GROUPS = 8
TOPK_GROUPS = 4
EXPERTS_PER_GROUP = N_EXPERTS // N_EXPERT_GROUPS
D_EXPERT = 256
D_SHARED = 256
ROUTED_SCALE = 2.5
MOE_BLOCK = 128

LANES = 128
VMEM_LIMIT_BYTES = 56 * 1024 * 1024


def _params(*semantics):
    return pltpu.CompilerParams(dimension_semantics=semantics, vmem_limit_bytes=VMEM_LIMIT_BYTES)


def _const_spec(a):
    return pl.BlockSpec(a.shape, lambda *_: (0,) * a.ndim, pipeline_mode=pl.Buffered(1))


def _mm(a, b):
    return jnp.dot(a.astype(BF16), b.astype(BF16), preferred_element_type=F32)


def _mm_nt(a, b):
    return lax.dot_general(a.astype(BF16), b.astype(BF16), (((1,), (1,)), ((), ())), preferred_element_type=F32)


def _mm_tn(a, b):
    return lax.dot_general(a.astype(BF16), b.astype(BF16), (((0,), (0,)), ((), ())), preferred_element_type=F32)


def _split_hi_lo(x):
    hi = x.astype(BF16)
    return hi, (x - hi.astype(F32)).astype(BF16)


def _dot3(dims, a, b):
    a_hi, a_lo = _split_hi_lo(a)
    b_hi, b_lo = _split_hi_lo(b)
    dot = lambda u, w: lax.dot_general(u, w, (dims, ((), ())), preferred_element_type=F32)
    return dot(a_hi, b_hi) + (dot(a_hi, b_lo) + dot(a_lo, b_hi))


_mm3 = functools.partial(_dot3, ((1,), (0,)))
_mm3_nt = functools.partial(_dot3, ((1,), (1,)))
_mm3_tn = functools.partial(_dot3, ((0,), (0,)))


def _cat3_lhs(x):
    hi, lo = _split_hi_lo(x)
    return jnp.concatenate([hi, hi, lo], axis=-1)


def _split_hi_lo_outside(w):
    w = w.astype(F32)
    hi = lax.bitcast_convert_type(lax.bitcast_convert_type(w, jnp.uint32) & jnp.uint32(0xFFFF0000), F32)
    return hi.astype(BF16), (w - hi).astype(BF16)


def _cat3_rhs(w, axis=0):
    hi, lo = _split_hi_lo_outside(w)
    return jnp.concatenate([hi, lo, hi], axis=axis)


def _sigmoid(x):
    return 1.0 / (1.0 + jnp.exp(-x))


def _silu(x):
    return x * _sigmoid(x)


def _softplus(x):
    return jnp.maximum(x, 0.0) + jnp.log1p(jnp.exp(-jnp.abs(x)))


def _cumsum_rows(x):
    n = x.shape[0]
    row = lax.broadcasted_iota(I32, x.shape, 0)
    s = 1
    while s < n:
        x = x + jnp.where(row >= s, pltpu.roll(x, s, 0), 0.0)
        s *= 2
    return x


def _ada_norm(x, g, shift, scale):
    y = x * lax.rsqrt(jnp.mean(x * x, axis=-1, keepdims=True) + EPS)
    return (y * g) * (1.0 + scale) + shift


def _ada_kernel(c_ref, w_ref, b_ref, o_ref):
    o_ref[0] = _mm3(_silu(c_ref[...]), w_ref[0]) + b_ref[0]


def _ada_modulation(c_all, ada_w, ada_b):
    n = c_all.shape[0]
    n_mod = ada_w.shape[0] * ada_w.shape[1]
    w = ada_w.reshape(n_mod, D_MODEL, 3 * D_MODEL)
    b = ada_b.reshape(n_mod, 1, 3 * D_MODEL)
    return pl.pallas_call(
        _ada_kernel,
        out_shape=jax.ShapeDtypeStruct((n_mod, n, 3 * D_MODEL), F32),
        grid=(n_mod, 3),
        in_specs=[
            pl.BlockSpec((n, D_MODEL), lambda i, j: (0, 0)),
            pl.BlockSpec((1, D_MODEL, D_MODEL), lambda i, j: (i, 0, j)),
            pl.BlockSpec((1, 1, D_MODEL), lambda i, j: (i, 0, j)),
        ],
        out_specs=pl.BlockSpec((1, n, D_MODEL), lambda i, j: (i, 0, j)),
        compiler_params=_params("parallel", "parallel"),
        name="ada_modulation",
    )(c_all, w, b)


ROW_TILE = 256

EV_Q, EV_K, EV_V, EV_Z, EV_XBC, EV_SMALL = 0, FOX_DIM, 2 * FOX_DIM, 3 * FOX_DIM, 3 * FOX_DIM + SSD_INNER, 3 * FOX_DIM + SSD_INNER + SSD_CONV_DIM
EV_COLS = EV_SMALL + LANES
OD_RW, OD_QKV, OD_Z, OD_SMALL = 0, RWKV_PROJ, RWKV_PROJ + GDN_CONV_DIM, RWKV_PROJ + GDN_CONV_DIM + GDN_DIM
OD_COLS = OD_SMALL + LANES


def _row_blocking(b, l):
    if l >= ROW_TILE:
        assert l % ROW_TILE == 0
        return 1, ROW_TILE
    assert ROW_TILE % l == 0 and b % (ROW_TILE // l) == 0 and l % 8 == 0
    return ROW_TILE // l, l


def _normed_rows(x_ref, g_ref, sh_ref, sc_ref):
    x = x_ref[...]
    h = _ada_norm(x, g_ref[...], sh_ref[...], sc_ref[...])
    return h.reshape(x.shape[0] * x.shape[1], x.shape[2])


def _head_rms(y, head_mean, gain):
    hi, lo = _split_hi_lo(y * y)
    ms = jnp.dot(hi, head_mean, preferred_element_type=F32) + jnp.dot(lo, head_mean, preferred_element_type=F32)
    return y * lax.rsqrt(ms + EPS) * gain


FOX_HD3 = 3 * FOX_HD


def _head_cat3(x, rhs):
    hi, lo = _split_hi_lo(x)
    pieces = []
    for h in range(FOX_HEADS):
        hs = slice(h * FOX_HD, (h + 1) * FOX_HD)
        pieces += [hi[:, hs], lo[:, hs], hi[:, hs]] if rhs else [hi[:, hs], hi[:, hs], lo[:, hs]]
    return jnp.concatenate(pieces, axis=-1)


def _inproj_even_kernel(x_ref, g_ref, sh_ref, sc_ref, w_ref, hm_ref, qg_ref, kg_ref, bias_ref,
                        q3_ref, k3_ref, k_ref, v_ref, z_ref, xbc_ref, small_ref):
    h = _cat3_lhs(_normed_rows(x_ref, g_ref, sh_ref, sc_ref))
    shp3 = lambda r: (x_ref.shape[0], x_ref.shape[1], r.shape[2])
    cols = lambda lo, hi: jnp.dot(h, w_ref[:, lo:hi], preferred_element_type=F32)
    hm = hm_ref[...]
    q = _head_rms(cols(EV_Q, EV_K), hm, qg_ref[...]) * FOX_SCALE
    q3_ref[...] = _head_cat3(q, rhs=False).reshape(shp3(q3_ref))
    k = _head_rms(cols(EV_K, EV_V), hm, kg_ref[...])
    k_ref[...] = k.reshape(shp3(k_ref))
    k3_ref[...] = _head_cat3(k, rhs=True).reshape(shp3(k3_ref))
    v_ref[...] = cols(EV_V, EV_Z).reshape(shp3(v_ref))
    z_ref[...] = cols(EV_Z, EV_XBC).reshape(shp3(z_ref))
    xbc_ref[...] = cols(EV_XBC, EV_SMALL).reshape(shp3(xbc_ref))
    t = cols(EV_SMALL, EV_COLS) + bias_ref[...]
    lane = lax.broadcasted_iota(I32, t.shape, 1)
    small = jnp.where(lane < FOX_HEADS, -_softplus(-t), jnp.where(lane < FOX_HEADS + SSD_HEADS, _softplus(t), 0.0))
    small_ref[...] = small.reshape(shp3(small_ref))


def _inproj_odd_kernel(x_ref, g_ref, sh_ref, sc_ref, w_ref, bias_ref, alog_ref,
                       rw_ref, qkv_ref, z_ref, small_ref):
    h = _normed_rows(x_ref, g_ref, sh_ref, sc_ref).astype(BF16)
    shp3 = lambda r: (x_ref.shape[0], x_ref.shape[1], r.shape[2])
    rw_ref[...] = jnp.dot(h, w_ref[:, OD_RW:OD_QKV], preferred_element_type=F32).reshape(shp3(rw_ref))
    qkv_ref[...] = jnp.dot(h, w_ref[:, OD_QKV:OD_Z], preferred_element_type=F32).reshape(shp3(qkv_ref))
    z_ref[...] = jnp.dot(h, w_ref[:, OD_Z:OD_SMALL], preferred_element_type=F32).reshape(shp3(z_ref))
    t = jnp.dot(h, w_ref[:, OD_SMALL:OD_COLS], preferred_element_type=F32) + bias_ref[...]
    lane = lax.broadcasted_iota(I32, t.shape, 1)
    small = jnp.where(lane < GDN_HEADS, _sigmoid(t),
                      jnp.where(lane < 2 * GDN_HEADS, -jnp.exp(alog_ref[...]) * _softplus(t), 0.0))
    small_ref[...] = small.reshape(shp3(small_ref))


def _inproj_call(kernel_fn, name, x, g, shift, scale, w, extras, out_cols, out_dtypes):
    b, l, d = x.shape
    kb, tl = _row_blocking(b, l)
    tok = lambda c: pl.BlockSpec((kb, tl, c), lambda i, j: (i, j, 0))
    per_batch = pl.BlockSpec((kb, 1, d), lambda i, j: (i, 0, 0))
    const = _const_spec
    return pl.pallas_call(
        kernel_fn,
        out_shape=[jax.ShapeDtypeStruct((b, l, c), dt) for c, dt in zip(out_cols, out_dtypes)],
        grid=(b // kb, l // tl),
        in_specs=[tok(d), const(g), per_batch, per_batch, const(w)] + [const(e) for e in extras],
        out_specs=[tok(c) for c in out_cols],
        compiler_params=_params("parallel", "parallel"),
        name=name,
    )(x, g, shift, scale, w, *extras)


def _pad_lanes(*pieces):
    v = jnp.concatenate([p.reshape(-1).astype(F32) for p in pieces])
    return jnp.pad(v, (0, LANES - v.shape[0])).reshape(1, LANES)


def _head_mean_matrix(dim, hd):
    r = jnp.arange(dim) // hd
    return jnp.where(r[:, None] == r[None, :], 1.0 / hd, 0.0).astype(BF16)


def _pack_even_w_in(w_in):
    q, k, v, f, z, xbc, dt = jnp.split(w_in, (FOX_DIM, 2 * FOX_DIM, 3 * FOX_DIM, 3 * FOX_DIM + FOX_HEADS,
                                              3 * FOX_DIM + FOX_HEADS + SSD_INNER,
                                              3 * FOX_DIM + FOX_HEADS + SSD_INNER + SSD_CONV_DIM), axis=1)
    pad = jnp.zeros((w_in.shape[0], LANES - FOX_HEADS - SSD_HEADS), w_in.dtype)
    return _cat3_rhs(jnp.concatenate([q, k, v, z, xbc, f, dt, pad], axis=1))


def _pack_odd_w_in(w_in):
    main, small = w_in[:, :OD_SMALL], w_in[:, OD_SMALL:]
    pad = jnp.zeros((w_in.shape[0], LANES - 2 * GDN_HEADS), w_in.dtype)
    return jnp.concatenate([main, small, pad], axis=1).astype(BF16)


def _split3_bf16(x):
    c1 = x.astype(BF16)
    r1 = x - c1.astype(F32)
    c2 = r1.astype(BF16)
    c3 = (r1 - c2.astype(F32)).astype(BF16)
    return c1, c2, c3


def _rows_to_lanes(sel, x):
    nt = lambda a, b: lax.dot_general(a, b, (((1,), (1,)), ((), ())), preferred_element_type=F32)
    c1, c2, c3 = _split3_bf16(x)
    return (nt(sel, c1) + nt(sel, c2)) + nt(sel, c3)


def _fox_cum_kernel(*refs, n_past):
    if n_past:
        past_ref, small_ref, sel_ref, cum_ref, cumt_ref = refs
        seq = jnp.concatenate([past_ref[0], small_ref[0]], axis=0)
    else:
        small_ref, sel_ref, cum_ref, cumt_ref = refs
        seq = small_ref[0]
    cum = _cumsum_rows(seq)
    cum_ref[0] = cum[n_past:]
    cumt_ref[0] = _rows_to_lanes(sel_ref[...], cum)


def _fox_cumulative(small, past_logf):
    b, l, _ = small.shape
    n_past = 0 if past_logf is None else past_logf.shape[1]
    sel = jnp.eye(FOX_HEADS, LANES, dtype=BF16)
    ins, specs = [], []
    if n_past:
        ins.append(past_logf)
        specs.append(pl.BlockSpec((1, n_past, LANES), lambda i: (i, 0, 0)))
    ins += [small, sel]
    specs += [pl.BlockSpec((1, l, LANES), lambda i: (i, 0, 0)), pl.BlockSpec(sel.shape, lambda i: (0, 0))]
    return pl.pallas_call(
        functools.partial(_fox_cum_kernel, n_past=n_past),
        out_shape=[jax.ShapeDtypeStruct((b, l, LANES), F32), jax.ShapeDtypeStruct((b, FOX_HEADS, n_past + l), F32)],
        grid=(b,),
        in_specs=specs,
        out_specs=[pl.BlockSpec((1, l, LANES), lambda i: (i, 0, 0)),
                   pl.BlockSpec((1, FOX_HEADS, n_past + l), lambda i: (i, 0, 0))],
        compiler_params=_params("parallel"),
        name="fox_cumulative",
    )(*ins)


FOX_TILE = 256


def _fox_flash_kernel(q_ref, k_ref, v_ref, cq_ref, ck_ref, o_ref, m_sc, l_sc, acc_sc):
    qi, kj = pl.program_id(1), pl.program_id(2)

    @pl.when(kj == 0)
    def _():
        m_sc[...] = jnp.full(m_sc.shape, -jnp.inf, F32)
        l_sc[...] = jnp.zeros(l_sc.shape, F32)
        acc_sc[...] = jnp.zeros(acc_sc.shape, F32)

    @pl.when(kj <= qi)
    def _():
        q3, k3 = q_ref[0], k_ref[0]
        v_hi, v_lo = _split_hi_lo(v_ref[0])
        cq, ck = cq_ref[0], ck_ref[0]
        row = lax.broadcasted_iota(I32, (FOX_TILE, FOX_TILE), 0)
        col = lax.broadcasted_iota(I32, (FOX_TILE, FOX_TILE), 1)
        visible = jnp.logical_or(col <= row, kj < qi)
        for h in range(FOX_HEADS):
            hs = slice(h * FOX_HD, (h + 1) * FOX_HD)
            hs3 = slice(h * FOX_HD3, (h + 1) * FOX_HD3)
            s = _scores3(q3[:, hs3], k3[:, hs3]) + (cq[:, h:h + 1] - ck[h:h + 1, :])
            s = jnp.where(visible, s, -jnp.inf)
            m_prev = m_sc[h]
            m_new = jnp.maximum(m_prev, jnp.max(s, axis=-1, keepdims=True))
            p = jnp.exp(s - m_new)
            alpha = jnp.exp(m_prev - m_new)
            l_sc[h] = alpha * l_sc[h] + jnp.sum(p, axis=-1, keepdims=True)
            acc_sc[h] = alpha * acc_sc[h] + _pv3(p, v_hi[:, hs], v_lo[:, hs])
            m_sc[h] = m_new

    @pl.when(kj == qi)
    def _():
        o_ref[0] = jnp.concatenate([acc_sc[h] / l_sc[h] for h in range(FOX_HEADS)], axis=-1)


def _scores3(q3, k3):
    return lax.dot_general(q3, k3, (((1,), (1,)), ((), ())), preferred_element_type=F32)


def _pv3(p, v_hi, v_lo):
    p_hi, p_lo = _split_hi_lo(p)
    dot = lambda a, b: jnp.dot(a, b, preferred_element_type=F32)
    return dot(p_hi, v_hi) + (dot(p_hi, v_lo) + dot(p_lo, v_hi))


def _fox_attention_prompt(q3, k3, v, cum, cumt):
    b, l, _ = v.shape
    n = l // FOX_TILE
    qspec = lambda c: pl.BlockSpec((1, FOX_TILE, c), lambda i, j, t: (i, j, 0))
    kspec = lambda c: pl.BlockSpec((1, FOX_TILE, c), lambda i, j, t: (i, jnp.minimum(t, j), 0))
    return pl.pallas_call(
        _fox_flash_kernel,
        out_shape=jax.ShapeDtypeStruct((b, l, FOX_DIM), F32),
        grid=(b, n, n),
        in_specs=[qspec(FOX_HEADS * FOX_HD3), kspec(FOX_HEADS * FOX_HD3), kspec(FOX_DIM), qspec(LANES),
                  pl.BlockSpec((1, FOX_HEADS, FOX_TILE), lambda i, j, t: (i, 0, jnp.minimum(t, j)))],
        out_specs=qspec(FOX_DIM),
        scratch_shapes=[pltpu.VMEM((FOX_HEADS, FOX_TILE, 1), F32), pltpu.VMEM((FOX_HEADS, FOX_TILE, 1), F32),
                        pltpu.VMEM((FOX_HEADS, FOX_TILE, FOX_HD), F32)],
        compiler_params=_params("parallel", "parallel", "arbitrary"),
        name="fox_flash",
    )(q3, k3, v, cum, cumt)


def _fox_decode_kernel(q_ref, kp_ref, vp_ref, kn_ref, vn_ref, cq_ref, ck_ref, o_ref):
    n_past, l = kp_ref.shape[1], q_ref.shape[1]
    q3, kn3 = q_ref[0], kn_ref[0]
    kp3 = _head_cat3(kp_ref[0], rhs=True)
    vp_hi, vp_lo = _split_hi_lo(vp_ref[0])
    vn_hi, vn_lo = _split_hi_lo(vn_ref[0])
    cq, ck = cq_ref[0], ck_ref[0]
    row = lax.broadcasted_iota(I32, (l, l), 0)
    col = lax.broadcasted_iota(I32, (l, l), 1)
    outs = []
    for h in range(FOX_HEADS):
        hs = slice(h * FOX_HD, (h + 1) * FOX_HD)
        hs3 = slice(h * FOX_HD3, (h + 1) * FOX_HD3)
        cqh = cq[:, h:h + 1]
        s_past = _scores3(q3[:, hs3], kp3[:, hs3]) + (cqh - ck[h:h + 1, :n_past])
        s_new = _scores3(q3[:, hs3], kn3[:, hs3]) + (cqh - ck[h:h + 1, n_past:])
        s_new = jnp.where(col <= row, s_new, -jnp.inf)
        m = jnp.maximum(jnp.max(s_past, axis=-1, keepdims=True), jnp.max(s_new, axis=-1, keepdims=True))
        p_past, p_new = jnp.exp(s_past - m), jnp.exp(s_new - m)
        denom = jnp.sum(p_past, axis=-1, keepdims=True) + jnp.sum(p_new, axis=-1, keepdims=True)
        outs.append((_pv3(p_past, vp_hi[:, hs], vp_lo[:, hs]) + _pv3(p_new, vn_hi[:, hs], vn_lo[:, hs])) / denom)
    o_ref[0] = jnp.concatenate(outs, axis=-1)


def _fox_attention_decode(q3, k3, v, k_past, v_past, cum, cumt):
    b, l, _ = v.shape
    n_past = k_past.shape[1]
    new = lambda c: pl.BlockSpec((1, l, c), lambda i: (i, 0, 0))
    past = pl.BlockSpec((1, n_past, FOX_DIM), lambda i: (i, 0, 0))
    return pl.pallas_call(
        _fox_decode_kernel,
        out_shape=jax.ShapeDtypeStruct((b, l, FOX_DIM), F32),
        grid=(b,),
        in_specs=[new(FOX_HEADS * FOX_HD3), past, past, new(FOX_HEADS * FOX_HD3), new(FOX_DIM), new(LANES),
                  pl.BlockSpec((1, FOX_HEADS, n_past + l), lambda i: (i, 0, 0))],
        out_specs=new(FOX_DIM),
        compiler_params=_params("parallel"),
        name="fox_decode",
    )(q3, k_past, v_past, k3, v, cum, cumt)


CONV_TAIL = 8


def _causal_conv(x, tail, w):
    q = x.shape[0]
    row = lax.broadcasted_iota(I32, (CONV_TAIL, x.shape[1]), 0)
    acc = x * w[CONV_W - 1:CONV_W]
    for s in range(1, CONV_W):
        xr = pltpu.roll(x, s, 0)
        head = jnp.where(row < s, pltpu.roll(tail, s, 0), xr[:CONV_TAIL])
        shifted = head if q == CONV_TAIL else jnp.concatenate([head, xr[CONV_TAIL:]], axis=0)
        acc = acc + shifted * w[CONV_W - 1 - s:CONV_W - s]
    return acc


def _conv_tail_from_state(state):
    return jnp.pad(state, ((0, 0), (CONV_TAIL - (CONV_W - 1), 0), (0, 0)))


SSD_CHUNK = 128
SSD_DT_LANE = FOX_HEADS


def _ssd_kernel(xbc_ref, z_ref, small_ref, tail0_ref, h0_ref, cw_ref, cb_ref, alog_ref, drep_ref, ng_ref, sel_ref,
                y_ref, tail_ref, hout_ref, tail_sc, h_sc):
    c = pl.program_id(1)

    @pl.when(c == 0)
    def _():
        tail_sc[...] = tail0_ref[0]
        h_sc[...] = h0_ref[0]

    x = xbc_ref[0]
    q = x.shape[0]
    act = _silu(_causal_conv(x, tail_sc[...], cw_ref[...]) + cb_ref[...])
    tail_sc[...] = x[q - CONV_TAIL:]
    xs = act[:, :SSD_INNER]
    bm = act[:, SSD_INNER:SSD_INNER + SSD_GROUPS * SSD_N]
    cm = act[:, SSD_INNER + SSD_GROUPS * SSD_N:]

    small = small_ref[0]
    lane = lax.broadcasted_iota(I32, small.shape, 1)
    is_dt = jnp.logical_and(lane >= SSD_DT_LANE, lane < SSD_DT_LANE + SSD_HEADS)
    da = jnp.where(is_dt, small * -jnp.exp(alog_ref[...]), 0.0)
    acum = _cumsum_rows(da)
    acum_t = _rows_to_lanes(sel_ref[...], acum)
    trow = lax.broadcasted_iota(I32, (q, q), 0)
    tcol = lax.broadcasted_iota(I32, (q, q), 1)
    heads_per_group = SSD_HEADS // SSD_GROUPS
    ys = []
    for g in range(SSD_GROUPS):
        bg = bm[:, g * SSD_N:(g + 1) * SSD_N]
        cg = cm[:, g * SSD_N:(g + 1) * SSD_N]
        cb = _mm3_nt(cg, bg)
        for h in range(g * heads_per_group, (g + 1) * heads_per_group):
            a_col = acum[:, SSD_DT_LANE + h:SSD_DT_LANE + h + 1]
            dt_col = small[:, SSD_DT_LANE + h:SSD_DT_LANE + h + 1]
            a_last = a_col[q - 1:q]
            seg = jnp.exp(jnp.where(trow >= tcol, a_col - acum_t[h:h + 1, :], -jnp.inf))
            xdt = xs[:, h * SSD_HD:(h + 1) * SSD_HD] * dt_col
            state = h_sc[h]
            ys.append(_mm3(cb * seg, xdt) + jnp.exp(a_col) * _mm3_nt(cg, state))
            h_sc[h] = state * jnp.exp(a_last) + _mm3_tn(xdt * jnp.exp(a_last - a_col), bg)
    y = jnp.concatenate(ys, axis=-1) + drep_ref[...] * xs
    y = y * _silu(z_ref[0])
    gw = SSD_INNER // SSD_GROUPS
    normed = []
    for g in range(SSD_GROUPS):
        yg = y[:, g * gw:(g + 1) * gw]
        normed.append(yg * lax.rsqrt(jnp.mean(yg * yg, axis=-1, keepdims=True) + EPS))
    y_ref[0] = (jnp.concatenate(normed, axis=-1) * ng_ref[...]).astype(y_ref.dtype)

    @pl.when(c == pl.num_programs(1) - 1)
    def _():
        tail_ref[0] = tail_sc[...]
        hout_ref[0] = h_sc[...]


def _ssd_mixer(xbc, z, small, conv_state, h0, conv_w, conv_b, a_log, d_skip, norm_g):
    b, l, _ = xbc.shape
    q = min(SSD_CHUNK, l)
    tail0 = _conv_tail_from_state(conv_state)
    alog = _pad_lanes(jnp.zeros(SSD_DT_LANE), a_log)
    drep = jnp.repeat(d_skip.astype(F32), SSD_HD)[None]
    sel = jnp.eye(SSD_HEADS, LANES, k=SSD_DT_LANE, dtype=BF16)
    tok = lambda c: pl.BlockSpec((1, q, c), lambda i, j: (i, j, 0))
    per_b = lambda a: pl.BlockSpec((1,) + a.shape[1:], lambda i, j: (i,) + (0,) * (a.ndim - 1))
    const = _const_spec
    consts = [conv_w, conv_b[None], alog, drep, norm_g[None], sel]
    y, tail, h_new = pl.pallas_call(
        _ssd_kernel,
        out_shape=[jax.ShapeDtypeStruct((b, l, SSD_INNER), F32),
                   jax.ShapeDtypeStruct((b, CONV_TAIL, SSD_CONV_DIM), F32),
                   jax.ShapeDtypeStruct(h0.shape, F32)],
        grid=(b, l // q),
        in_specs=[tok(SSD_CONV_DIM), tok(SSD_INNER), tok(LANES), per_b(tail0), per_b(h0)] + [const(a) for a in consts],
        out_specs=[tok(SSD_INNER), per_b(tail0), per_b(h0)],
        scratch_shapes=[pltpu.VMEM((CONV_TAIL, SSD_CONV_DIM), F32), pltpu.VMEM(h0.shape[1:], F32)],
        compiler_params=_params("parallel", "arbitrary"),
        name="ssd_mixer",
    )(xbc, z, small, tail0, h0, *consts)
    return y, tail[:, CONV_TAIL - (CONV_W - 1):], h_new


REC_CHUNK = 64


INV_BASE = 8


def _unit_lower_inverse(n):
    c = n.shape[0]
    row = lax.broadcasted_iota(I32, (c, c), 0)
    col = lax.broadcasted_iota(I32, (c, c), 1)
    shift = INV_BASE.bit_length() - 1
    same = lax.shift_right_logical(row, shift) == lax.shift_right_logical(col, shift)
    diag = jnp.where(same, n, 0.0)
    inv = jnp.where(row == col, 1.0, 0.0) + diag
    power = diag
    span = 2
    while span < INV_BASE:
        power = _mm3(power, power)
        inv = inv + _mm3(inv, power)
        span *= 2
    size = INV_BASE
    while size < c:
        shift += 1
        size *= 2
        merged = lax.shift_right_logical(row, shift) == lax.shift_right_logical(col, shift)
        off = jnp.where(jnp.logical_and(merged, jnp.logical_not(same)), n, 0.0)
        inv = inv + _mm3(_mm3(inv, off), inv)
        same = merged
    return inv


def _shift_rows_by_one(x, tail):
    q = x.shape[0]
    row = lax.broadcasted_iota(I32, (CONV_TAIL, x.shape[1]), 0)
    xr = pltpu.roll(x, 1, 0)
    head = jnp.where(row < 1, pltpu.roll(tail, 1, 0), xr[:CONV_TAIL])
    return head if q == CONV_TAIL else jnp.concatenate([head, xr[CONV_TAIL:]], axis=0)


RW_R, RW_K, RW_V, RW_LORA, RW_GATE = 0, RWKV_DIM, 2 * RWKV_DIM, 3 * RWKV_DIM, 3 * RWKV_DIM + RWKV_DECAY_LORA + RWKV_A_LORA


def _rwkv_kernel(rw_ref, tail0_ref, s0_ref, mu_ref, w0_ref, w2_ref, a0_ref, a2_ref, g2_ref, kk_ref, ka_ref, rk_ref,
                 lng_ref, lnb_ref, hsum_ref, o_ref, sout_ref, tail_sc, s_sc):
    c = pl.program_id(1)

    @pl.when(c == 0)
    def _():
        tail_sc[...] = tail0_ref[0]
        s_sc[...] = s0_ref[0]

    x = rw_ref[0]
    q = x.shape[0]
    mixed = x + (_shift_rows_by_one(x, tail_sc[...]) - x) * mu_ref[...]
    tail_sc[...] = x[q - CONV_TAIL:]
    r = mixed[:, RW_R:RW_K]
    k = mixed[:, RW_K:RW_V]
    v = mixed[:, RW_V:RW_LORA]
    lora = mixed[:, RW_LORA:RW_GATE]
    w_log = -_softplus(-(w0_ref[...] + _mm(jnp.tanh(lora), w2_ref[...]))) - 0.5
    logw = -jnp.exp(w_log)
    icl = _sigmoid(a0_ref[...] + _mm(lora, a2_ref[...]))
    out_gate = _mm(_sigmoid(mixed[:, RW_GATE:]), g2_ref[...])
    kk = k * kk_ref[...]
    kk = kk * lax.rsqrt(_mm(kk * kk, hsum_ref[...]) + EPS)
    k2 = k * (1.0 + (icl - 1.0) * ka_ref[...])
    cum = _cumsum_rows(logw)
    w_run = jnp.exp(cum)
    w_inv = jnp.exp(-cum)
    rt = r * w_run
    at = -kk * jnp.exp(cum - logw)
    bt = kk * icl * w_inv
    kt = k2 * w_inv
    trow = lax.broadcasted_iota(I32, (q, q), 0)
    tcol = lax.broadcasted_iota(I32, (q, q), 1)
    strict, incl = trow > tcol, trow >= tcol
    outs = []
    for h in range(RWKV_HEADS):
        hs = slice(h * RWKV_HD, (h + 1) * RWKV_HD)
        ar = jnp.concatenate([at[:, hs], rt[:, hs]], axis=0)
        bk = jnp.concatenate([bt[:, hs], kt[:, hs]], axis=0)
        gram = _mm_nt(ar, bk)
        a_ab = jnp.where(strict, gram[:q, :q], 0.0)
        a_ak = jnp.where(strict, gram[:q, q:], 0.0)
        a_r = jnp.concatenate([jnp.where(incl, gram[q:, :q], 0.0), jnp.where(incl, gram[q:, q:], 0.0)], axis=1)
        s0 = s_sc[h]
        ars = _mm_nt(ar, s0)
        vh = v[:, hs]
        u = _mm(_unit_lower_inverse(a_ab), ars[:q] + _mm(a_ak, vh))
        uv = jnp.concatenate([u, vh], axis=0)
        o = ars[q:] + _mm(a_r, uv)
        s_sc[h] = (s0 + _mm_tn(uv, bk)) * w_run[q - 1:q, hs]
        mean = jnp.mean(o, axis=-1, keepdims=True)
        var = jnp.mean(jnp.square(o - mean), axis=-1, keepdims=True)
        o = (o - mean) * lax.rsqrt(var + RWKV_GN_EPS) * lng_ref[:, hs] + lnb_ref[:, hs]
        bonus = jnp.sum(r[:, hs] * k2[:, hs] * rk_ref[:, hs], axis=-1, keepdims=True) * vh
        outs.append(o + bonus)
    o_ref[0] = (jnp.concatenate(outs, axis=-1) * out_gate).astype(o_ref.dtype)

    @pl.when(c == pl.num_programs(1) - 1)
    def _():
        sout_ref[0] = s_sc[...]


def _rwkv_mixer(rw, shift_state, s0, p):
    b, l, _ = rw.shape
    q = min(REC_CHUNK, l)
    tail0 = jnp.pad(shift_state[:, None, :], ((0, 0), (CONV_TAIL - 1, 0), (0, 0)))
    zeros = jnp.zeros((RWKV_DECAY_LORA, RWKV_DIM), F32)
    consts = [p['rwkv_mu'][None], p['rwkv_w0'][None],
              jnp.concatenate([p['rwkv_w2'], zeros], axis=0).astype(BF16), p['rwkv_a0'][None],
              jnp.concatenate([zeros, p['rwkv_a2']], axis=0).astype(BF16), p['rwkv_g2'].astype(BF16),
              p['rwkv_k_k'][None], p['rwkv_k_a'][None], p['rwkv_r_k'].reshape(1, RWKV_DIM),
              p['rwkv_ln_g'][None], p['rwkv_ln_b'][None], _head_mean_matrix(RWKV_DIM, RWKV_HD) * RWKV_HD]
    tok = lambda c: pl.BlockSpec((1, q, c), lambda i, j: (i, j, 0))
    per_b = lambda a: pl.BlockSpec((1,) + a.shape[1:], lambda i, j: (i,) + (0,) * (a.ndim - 1))
    const = _const_spec
    return pl.pallas_call(
        _rwkv_kernel,
        out_shape=[jax.ShapeDtypeStruct((b, l, RWKV_DIM), BF16), jax.ShapeDtypeStruct(s0.shape, F32)],
        grid=(b, l // q),
        in_specs=[tok(RWKV_PROJ), per_b(tail0), per_b(s0)] + [const(a) for a in consts],
        out_specs=[tok(RWKV_DIM), per_b(s0)],
        scratch_shapes=[pltpu.VMEM((CONV_TAIL, RWKV_PROJ), F32), pltpu.VMEM(s0.shape[1:], F32)],
        compiler_params=_params("parallel", "arbitrary"),
        name="rwkv_mixer",
    )(rw, tail0, s0, *consts)


GDN_BETA_LANE, GDN_G_LANE = 0, GDN_HEADS


def _l2norm(x):
    return x * lax.rsqrt(jnp.sum(x * x, axis=-1, keepdims=True) + EPS)


def _gdn_kernel(qkv_ref, z_ref, small_ref, tail0_ref, s0_ref, cw_ref, ng_ref, sel_ref,
                o_ref, tail_ref, sout_ref, tail_sc, s_sc):
    c = pl.program_id(1)

    @pl.when(c == 0)
    def _():
        tail_sc[...] = tail0_ref[0]
        s_sc[...] = s0_ref[0]

    x = qkv_ref[0]
    q = x.shape[0]
    act = _silu(_causal_conv(x, tail_sc[...], cw_ref[...]))
    tail_sc[...] = x[q - CONV_TAIL:]
    small = small_ref[0]
    z = z_ref[0]
    lane = lax.broadcasted_iota(I32, small.shape, 1)
    is_g = jnp.logical_and(lane >= GDN_G_LANE, lane < GDN_G_LANE + GDN_HEADS)
    gam = _cumsum_rows(jnp.where(is_g, small, 0.0))
    gam_t = _rows_to_lanes(sel_ref[...], gam)
    trow = lax.broadcasted_iota(I32, (q, q), 0)
    tcol = lax.broadcasted_iota(I32, (q, q), 1)
    outs = []
    for h in range(GDN_HEADS):
        hs = slice(h * GDN_HD, (h + 1) * GDN_HD)
        qh = _l2norm(act[:, hs]) * (GDN_HD ** -0.5)
        kh = _l2norm(act[:, GDN_DIM + h * GDN_HD:GDN_DIM + (h + 1) * GDN_HD])
        vh = act[:, 2 * GDN_DIM + h * GDN_HD:2 * GDN_DIM + (h + 1) * GDN_HD]
        beta = small[:, GDN_BETA_LANE + h:GDN_BETA_LANE + h + 1]
        g_col = gam[:, GDN_G_LANE + h:GDN_G_LANE + h + 1]
        g_last = g_col[q - 1:q]
        diff = g_col - gam_t[h:h + 1, :]
        a_mat = _mm_nt(kh, kh) * jnp.exp(jnp.where(trow > tcol, diff, -jnp.inf)) * beta
        eg = jnp.exp(g_col)
        uw = _mm(_unit_lower_inverse(-a_mat), jnp.concatenate([vh * beta, kh * (beta * eg)], axis=1))
        qk = _mm_nt(qh, kh) * jnp.exp(jnp.where(trow >= tcol, diff, -jnp.inf))
        state = s_sc[h]
        nu = uw[:, :GDN_HD] - _mm(uw[:, GDN_HD:], state)
        o = _mm(qh * eg, state) + _mm(qk, nu)
        s_sc[h] = state * jnp.exp(g_last) + _mm_tn(kh * jnp.exp(g_last - g_col), nu)
        o = o * lax.rsqrt(jnp.mean(o * o, axis=-1, keepdims=True) + EPS) * ng_ref[...]
        outs.append(o * _silu(z[:, hs]))
    o_ref[0] = jnp.concatenate(outs, axis=-1).astype(o_ref.dtype)

    @pl.when(c == pl.num_programs(1) - 1)
    def _():
        tail_ref[0] = tail_sc[...]
        sout_ref[0] = s_sc[...]


def _gdn_mixer(qkv, z, small, conv_state, s0, conv_w, norm_g):
    b, l, _ = qkv.shape
    q = min(REC_CHUNK, l)
    tail0 = _conv_tail_from_state(conv_state)
    sel = jnp.eye(GDN_HEADS, LANES, k=GDN_G_LANE, dtype=BF16)
    consts = [conv_w, norm_g[None], sel]
    tok = lambda c: pl.BlockSpec((1, q, c), lambda i, j: (i, j, 0))
    per_b = lambda a: pl.BlockSpec((1,) + a.shape[1:], lambda i, j: (i,) + (0,) * (a.ndim - 1))
    const = _const_spec
    o, tail, s_new = pl.pallas_call(
        _gdn_kernel,
        out_shape=[jax.ShapeDtypeStruct((b, l, GDN_DIM), BF16),
                   jax.ShapeDtypeStruct((b, CONV_TAIL, GDN_CONV_DIM), F32),
                   jax.ShapeDtypeStruct(s0.shape, F32)],
        grid=(b, l // q),
        in_specs=[tok(GDN_CONV_DIM), tok(GDN_DIM), tok(LANES), per_b(tail0), per_b(s0)] + [const(a) for a in consts],
        out_specs=[tok(GDN_DIM), per_b(tail0), per_b(s0)],
        scratch_shapes=[pltpu.VMEM((CONV_TAIL, GDN_CONV_DIM), F32), pltpu.VMEM(s0.shape[1:], F32)],
        compiler_params=_params("parallel", "arbitrary"),
        name="gdn_mixer",
    )(qkv, z, small, tail0, s0, *consts)
    return o, tail[:, CONV_TAIL - (CONV_W - 1):], s_new


def _outproj_kernel(a_ref, b_ref, w_ref, x_ref, gate_ref, g_ref, sh_ref, sc_ref, rwt_ref,
                    xo_ref, h_ref, logit_ref, *, split_inputs):
    kb, tl, d = x_ref.shape
    rows = kb * tl
    a = a_ref[...].reshape(rows, a_ref.shape[2])
    b = b_ref[...].reshape(rows, b_ref.shape[2])
    lhs = jnp.concatenate([_cat3_lhs(a), _cat3_lhs(b)] if split_inputs else [a.astype(BF16), b.astype(BF16)], axis=-1)
    mix = jnp.dot(lhs, w_ref[...], preferred_element_type=F32)
    x_new = x_ref[...] + gate_ref[...] * mix.reshape(kb, tl, d)
    xo_ref[...] = x_new
    h = _ada_norm(x_new, g_ref[...], sh_ref[...], sc_ref[...])
    h_ref[...] = h.astype(h_ref.dtype)
    logit_ref[...] = lax.dot_general(rwt_ref[...], _cat3_lhs(h.reshape(rows, d)), (((1,), (1,)), ((), ())),
                                     preferred_element_type=F32)


def _outproj_residual(a, b_half, w_out, x, gate, g2, shift2, scale2, router_w, split_inputs):
    b, l, d = x.shape
    kb, tl = _row_blocking(b, l)
    half = a.shape[2]
    pack = _cat3_rhs if split_inputs else (lambda w: w.astype(BF16))
    w = jnp.concatenate([pack(w_out[:half]), pack(w_out[half:])], axis=0)
    rwt = _cat3_rhs(router_w.T, axis=1)
    n_l = l // tl
    tok = lambda c: pl.BlockSpec((kb, tl, c), lambda i, j: (i, j, 0))
    per_batch = pl.BlockSpec((kb, 1, d), lambda i, j: (i, 0, 0))
    const = _const_spec
    return pl.pallas_call(
        functools.partial(_outproj_kernel, split_inputs=split_inputs),
        out_shape=[jax.ShapeDtypeStruct((b, l, d), F32), jax.ShapeDtypeStruct((b, l, d), BF16),
                   jax.ShapeDtypeStruct((N_EXPERTS, b * l), F32)],
        grid=(b // kb, n_l),
        in_specs=[tok(half), tok(b_half.shape[2]), const(w), tok(d), per_batch, const(g2), per_batch, per_batch, const(rwt)],
        out_specs=[tok(d), tok(d), pl.BlockSpec((N_EXPERTS, kb * tl), lambda i, j: (0, i * n_l + j))],
        compiler_params=_params("parallel", "parallel"),
        name="outproj_residual",
    )(a, b_half, w, x, gate, g2, shift2, scale2, rwt)


def _beats(a, b, a_first):
    return jnp.where(a > b, 1, jnp.where(a == b, a_first, 0))


def _router_kernel(logit_ref, bias_ref, gate_ref, pos_ref, count_ref):
    tm = logit_ref.shape[1]
    scores = _sigmoid(logit_ref[...])
    sel = scores + bias_ref[...]
    ng, eg = N_EXPERT_GROUPS, EXPERTS_PER_GROUP
    sub = lax.broadcasted_iota(I32, (eg, tm), 0)
    groups, grp = [], []
    for g in range(ng):
        xg = sel[g * eg:(g + 1) * eg]
        m1 = jnp.max(xg, axis=0, keepdims=True)
        first = jnp.min(jnp.where(xg == m1, sub, eg), axis=0, keepdims=True)
        m2 = jnp.max(jnp.where(sub == first, -jnp.inf, xg), axis=0, keepdims=True)
        groups.append(xg)
        grp.append(m1 + m2)
    masked = []
    for g in range(ng):
        rank = sum(_beats(grp[o], grp[g], 1 if o < g else 0) for o in range(ng) if o != g)
        masked.append(jnp.where(rank < TOPK_GROUPS, groups[g], -jnp.inf))
    selm = jnp.concatenate(masked, axis=0)
    erow = lax.broadcasted_iota(I32, (N_EXPERTS, tm), 0)
    rank = jnp.zeros((N_EXPERTS, tm), I32)
    for o in range(N_EXPERTS):
        rank = rank + _beats(selm[o:o + 1], selm, jnp.where(erow > o, 1, 0))
    gate = jnp.concatenate(
        [jnp.sum(jnp.where(rank == k, scores, 0.0), axis=0, keepdims=True) for k in range(TOP_K)], axis=0)
    gate_ref[...] = gate / jnp.sum(gate, axis=0, keepdims=True) * ROUTED_SCALE
    chosen = jnp.where(rank < TOP_K, 1.0, 0.0)
    earlier = jnp.where(lax.broadcasted_iota(I32, (tm, tm), 0) < lax.broadcasted_iota(I32, (tm, tm), 1), 1.0, 0.0)
    before = _mm(chosen, earlier)
    count = jnp.sum(chosen, axis=1, keepdims=True)
    padded = jnp.ceil(count * (1.0 / SEG_ALIGN)) * SEG_ALIGN
    lower = jnp.where(lax.broadcasted_iota(I32, (N_EXPERTS, N_EXPERTS), 1) < lax.broadcasted_iota(I32, (N_EXPERTS, N_EXPERTS), 0), 1.0, 0.0)
    seg_start = _mm(lower, jnp.broadcast_to(padded, (N_EXPERTS, LANES)))[:, :1]
    slot = seg_start + before
    pos_ref[...] = jnp.concatenate(
        [jnp.sum(jnp.where(rank == k, slot, 0.0), axis=0, keepdims=True) for k in range(TOP_K)], axis=0).astype(I32)
    count_ref[0] = jnp.broadcast_to(count, (N_EXPERTS, LANES)).astype(I32)


def _router(logits, router_b):
    t = logits.shape[1]
    n_tiles = t // ROW_TILE
    bias = router_b.astype(F32)[:, None]
    tile = lambda rows: pl.BlockSpec((rows, ROW_TILE), lambda i: (0, i))
    return pl.pallas_call(
        _router_kernel,
        out_shape=[jax.ShapeDtypeStruct((TOP_K, t), F32), jax.ShapeDtypeStruct((TOP_K, t), I32),
                   jax.ShapeDtypeStruct((n_tiles, N_EXPERTS, LANES), I32)],
        grid=(n_tiles,),
        in_specs=[tile(N_EXPERTS), _const_spec(bias)],
        out_specs=[tile(TOP_K), tile(TOP_K), pl.BlockSpec((1, N_EXPERTS, LANES), lambda i: (i, 0, 0))],
        compiler_params=_params("parallel"),
        name="moe_router",
    )(logits, bias)


SEG_ALIGN = 16
SEG_CHUNK = 32
EXPERT_BLOCK = 256
ONEHOT_CHUNK = 256
XS_COLS = D_MODEL + LANES
LOCAL_ROWS = -(-(TOP_K * ROW_TILE + N_EXPERTS * (SEG_ALIGN - 1)) // ONEHOT_CHUNK) * ONEHOT_CHUNK


def _segment_dmas(n_rows, src_ref, src_row, dst_ref, dst_row, sem, wait):
    def dma(rows, s, d):
        cp = pltpu.make_async_copy(src_ref.at[pl.ds(pl.multiple_of(s, SEG_ALIGN), rows)],
                                   dst_ref.at[pl.ds(pl.multiple_of(d, SEG_ALIGN), rows)], sem)
        cp.wait() if wait else cp.start()

    n_full = lax.shift_right_logical(n_rows, SEG_CHUNK.bit_length() - 1)

    def bulk(q, carry):
        dma(SEG_CHUNK, src_row + q * SEG_CHUNK, dst_row + q * SEG_CHUNK)
        return carry

    lax.fori_loop(0, n_full, bulk, 0)
    off = n_full * SEG_CHUNK
    size = SEG_CHUNK // 2
    while size >= SEG_ALIGN:
        has = (n_rows & size) != 0

        @pl.when(has)
        def _(size=size, off=off):
            dma(size, src_row + off, dst_row + off)

        off = off + jnp.where(has, size, 0)
        size //= 2


def _tile_segments(tile, cnt_ref, lstart_ref, gstart_ref, local_ref, global_ref, sem, to_global, wait):
    base = tile * N_EXPERTS

    def seg(e, carry):
        n, lo, go = cnt_ref[base + e], lstart_ref[base + e], gstart_ref[base + e]
        if to_global:
            _segment_dmas(n, local_ref, lo, global_ref, go, sem, wait)
        else:
            _segment_dmas(n, global_ref, go, local_ref, lo, sem, wait)
        return carry

    lax.fori_loop(0, N_EXPERTS, seg, 0)


def _zero_fill(tail_rows_ref, tail_start_ref, used_ref, zero_ref, xs_ref, sem, wait):
    def tail(e, carry):
        _segment_dmas(tail_rows_ref[e], zero_ref, 0, xs_ref, tail_start_ref[e], sem, wait)
        return carry

    lax.fori_loop(0, N_EXPERTS, tail, 0)

    def block(j, carry):
        cp = pltpu.make_async_copy(zero_ref, xs_ref.at[pl.ds(pl.multiple_of(j * EXPERT_BLOCK, EXPERT_BLOCK), EXPERT_BLOCK)], sem)
        cp.wait() if wait else cp.start()
        return carry

    lax.fori_loop(used_ref[0], xs_ref.shape[0] // EXPERT_BLOCK, block, 0)


def _dispatch_kernel(cnt_ref, lstart_ref, gstart_ref, tail_rows_ref, tail_start_ref, used_ref,
                     h_ref, pos_ref, gate_ref, xs_ref, loc_ref, zero_ref, sem):
    tile = pl.program_id(0) * pl.num_programs(1) + pl.program_id(1)
    last = pl.num_programs(0) * pl.num_programs(1) - 1
    kb, tl, d = h_ref.shape
    tm = kb * tl
    h = h_ref[...].reshape(tm, d)
    pos, gate = pos_ref[...], gate_ref[...]
    row = lax.broadcasted_iota(I32, (ONEHOT_CHUNK, tm), 0)
    lane = lax.broadcasted_iota(I32, (ONEHOT_CHUNK, LANES), 1)
    for c in range(LOCAL_ROWS // ONEHOT_CHUNK):
        onehot = jnp.zeros((ONEHOT_CHUNK, tm), F32)
        gsel = jnp.zeros((ONEHOT_CHUNK, tm), F32)
        for k in range(TOP_K):
            hit = (pos[k:k + 1, :] - c * ONEHOT_CHUNK) == row
            onehot = jnp.where(hit, 1.0, onehot)
            gsel = jnp.where(hit, gate[k:k + 1, :], gsel)
        rows = jnp.dot(onehot.astype(BF16), h, preferred_element_type=F32)
        g1, g2, g3 = (t.astype(F32) for t in _split3_bf16(jnp.sum(gsel, axis=1, keepdims=True)))
        sl = slice(c * ONEHOT_CHUNK, (c + 1) * ONEHOT_CHUNK)
        loc_ref[sl, :d] = rows.astype(BF16)
        loc_ref[sl, d:] = jnp.where(lane == 0, g1, jnp.where(lane == 1, g2, jnp.where(lane == 2, g3, 0.0))).astype(BF16)

    @pl.when(tile == last)
    def _():
        zero_ref[...] = jnp.zeros(zero_ref.shape, zero_ref.dtype)
        _zero_fill(tail_rows_ref, tail_start_ref, used_ref, zero_ref, xs_ref, sem, wait=False)

    _tile_segments(tile, cnt_ref, lstart_ref, gstart_ref, loc_ref, xs_ref, sem, to_global=True, wait=False)
    _tile_segments(tile, cnt_ref, lstart_ref, gstart_ref, loc_ref, xs_ref, sem, to_global=True, wait=True)

    @pl.when(tile == last)
    def _():
        _zero_fill(tail_rows_ref, tail_start_ref, used_ref, zero_ref, xs_ref, sem, wait=True)


def _moe_dispatch(h, pos, gate, plan):
    b, l, d = h.shape
    kb, tl = _row_blocking(b, l)
    n_l = l // tl
    tok = pl.BlockSpec((kb, tl, d), lambda i, j, *_: (i, j, 0))
    per_choice = pl.BlockSpec((TOP_K, kb * tl), lambda i, j, *_: (0, i * n_l + j))
    return pl.pallas_call(
        _dispatch_kernel,
        out_shape=jax.ShapeDtypeStruct((plan['cap'], XS_COLS), BF16),
        grid_spec=pltpu.PrefetchScalarGridSpec(
            num_scalar_prefetch=6,
            grid=(b // kb, n_l),
            in_specs=[tok, per_choice, per_choice],
            out_specs=pl.BlockSpec(memory_space=pl.ANY),
            scratch_shapes=[pltpu.VMEM((LOCAL_ROWS, XS_COLS), BF16), pltpu.VMEM((EXPERT_BLOCK, XS_COLS), BF16),
                            pltpu.SemaphoreType.DMA]),
        compiler_params=_params("arbitrary", "arbitrary"),
        name="moe_dispatch",
    )(plan['seg_rows'], plan['local_start'], plan['global_start'], plan['tail_rows'], plan['tail_start'],
      plan['blocks_used'], h, pos, gate)


def _swiglu(x, w1, w3, w2):
    hid = _silu(jnp.dot(x, w1, preferred_element_type=F32)) * jnp.dot(x, w3, preferred_element_type=F32)
    return jnp.dot(hid.astype(BF16), w2, preferred_element_type=F32)


def _expert_block_kernel(be_ref, nused_ref, x_ref, w1_ref, w3_ref, w2_ref, y_ref):
    j = pl.program_id(0)

    @pl.when(j < nused_ref[0])
    def _():
        g = x_ref[:, D_MODEL:].astype(F32)
        gate = (g[:, 0:1] + g[:, 1:2]) + g[:, 2:3]
        out = _swiglu(x_ref[:, :D_MODEL], w1_ref[0], w3_ref[0], w2_ref[0])
        y_ref[...] = (out * gate).astype(y_ref.dtype)

    @pl.when(j >= nused_ref[0])
    def _():
        y_ref[...] = jnp.zeros(y_ref.shape, y_ref.dtype)


def _expert_blocks(xs, plan, w1, w3, w2):
    cap = xs.shape[0]
    d = D_MODEL
    weight = lambda shape: pl.BlockSpec((1,) + shape, lambda j, be, nu: (be[j], 0, 0))
    return pl.pallas_call(
        _expert_block_kernel,
        out_shape=jax.ShapeDtypeStruct((cap, d), BF16),
        grid_spec=pltpu.PrefetchScalarGridSpec(
            num_scalar_prefetch=2,
            grid=(cap // EXPERT_BLOCK,),
            in_specs=[pl.BlockSpec((EXPERT_BLOCK, XS_COLS), lambda j, *_: (j, 0)),
                      weight((d, D_EXPERT)), weight((d, D_EXPERT)), weight((D_EXPERT, d))],
            out_specs=pl.BlockSpec((EXPERT_BLOCK, d), lambda j, *_: (j, 0))),
        compiler_params=_params("arbitrary"),
        name="moe_expert_blocks",
    )(plan['block_expert'], plan['blocks_used'], xs, w1, w3, w2)


def _combine_kernel(cnt_ref, lstart_ref, gstart_ref, ys_ref, pos_ref, h_ref, x_ref, gmod_ref, w1_ref, w3_ref, w2_ref,
                    o_ref, loc_ref, sem):
    tile = pl.program_id(0) * pl.num_programs(1) + pl.program_id(1)
    kb, tl, d = x_ref.shape
    tm = kb * tl

    @pl.when(tile == 0)
    def _():
        loc_ref[...] = jnp.zeros(loc_ref.shape, loc_ref.dtype)

    _tile_segments(tile, cnt_ref, lstart_ref, gstart_ref, loc_ref, ys_ref, sem, to_global=False, wait=False)
    shared = _swiglu(h_ref[...].reshape(tm, d), w1_ref[...], w3_ref[...], w2_ref[...])
    _tile_segments(tile, cnt_ref, lstart_ref, gstart_ref, loc_ref, ys_ref, sem, to_global=False, wait=True)
    pos = pos_ref[...]
    col = lax.broadcasted_iota(I32, (tm, ONEHOT_CHUNK), 1)
    routed = jnp.zeros((tm, d), F32)
    for c in range(LOCAL_ROWS // ONEHOT_CHUNK):
        onehot = jnp.zeros((tm, ONEHOT_CHUNK), F32)
        for k in range(TOP_K):
            onehot = jnp.where((pos[:, k:k + 1] - c * ONEHOT_CHUNK) == col, 1.0, onehot)
        rows = loc_ref[c * ONEHOT_CHUNK:(c + 1) * ONEHOT_CHUNK, :]
        routed = routed + jnp.dot(onehot.astype(BF16), rows, preferred_element_type=F32)
    o_ref[...] = x_ref[...] + gmod_ref[...] * (routed + shared).reshape(kb, tl, d)


def _moe_combine(ys, pos_t, h, x, gate_mod, plan, sw1, sw3, sw2):
    b, l, d = x.shape
    kb, tl = _row_blocking(b, l)
    n_l = l // tl
    tok = pl.BlockSpec((kb, tl, d), lambda i, j, *_: (i, j, 0))
    per_batch = pl.BlockSpec((kb, 1, d), lambda i, j, *_: (i, 0, 0))
    return pl.pallas_call(
        _combine_kernel,
        out_shape=jax.ShapeDtypeStruct((b, l, d), F32),
        grid_spec=pltpu.PrefetchScalarGridSpec(
            num_scalar_prefetch=3,
            grid=(b // kb, n_l),
            in_specs=[pl.BlockSpec(memory_space=pl.ANY),
                      pl.BlockSpec((kb * tl, TOP_K), lambda i, j, *_: (i * n_l + j, 0)),
                      tok, tok, per_batch, _const_spec(sw1), _const_spec(sw3), _const_spec(sw2)],
            out_specs=tok,
            scratch_shapes=[pltpu.VMEM((LOCAL_ROWS, D_MODEL), BF16), pltpu.SemaphoreType.DMA]),
        compiler_params=_params("arbitrary", "arbitrary"),
        name="moe_combine",
    )(plan['seg_rows'], plan['local_start'], plan['global_start'], ys, pos_t, h, x, gate_mod, sw1, sw3, sw2)


def _dispatch_plan(counts, n_tok):
    n_tiles = counts.shape[0]
    seg = (counts + SEG_ALIGN - 1) // SEG_ALIGN * SEG_ALIGN
    local_start = jnp.cumsum(seg, axis=1) - seg
    region = jnp.sum(seg, axis=0)
    region_pad = (region + EXPERT_BLOCK - 1) // EXPERT_BLOCK * EXPERT_BLOCK
    region_end = jnp.cumsum(region_pad)
    region_beg = region_end - region_pad
    global_start = region_beg[None, :] + jnp.cumsum(seg, axis=0) - seg
    worst = TOP_K * n_tok + N_EXPERTS * (SEG_ALIGN - 1) * n_tiles + N_EXPERTS * (EXPERT_BLOCK - 1)
    cap = -(-worst // EXPERT_BLOCK) * EXPERT_BLOCK
    block_row = jnp.arange(cap // EXPERT_BLOCK, dtype=I32) * EXPERT_BLOCK
    block_expert = jnp.minimum(jnp.sum((region_end[None, :] <= block_row[:, None]).astype(I32), axis=1), N_EXPERTS - 1)
    flat = lambda a: a.reshape(-1).astype(I32)
    return dict(cap=cap, seg_rows=flat(seg), local_start=flat(local_start), global_start=flat(global_start),
                tail_rows=flat(region_pad - region), tail_start=flat(region_beg + region),
                block_expert=flat(block_expert), blocks_used=flat(region_end[-1:] // EXPERT_BLOCK))


def _moe(h, logits, x, gate_mod, router_b, w1, w3, w2, sw1, sw3, sw2):
    b, l, d = x.shape
    gate, pos, counts = _router(logits, router_b)
    plan = _dispatch_plan(counts[:, :, 0], b * l)
    xs = _moe_dispatch(h, pos, gate, plan)
    ys = _expert_blocks(xs, plan, w1, w3, w2)
    return _moe_combine(ys, pos.T, h, x, gate_mod, plan, sw1, sw3, sw2)


def _split_mod(m):
    m = m[:, None, :]
    return m[..., :D_MODEL], m[..., D_MODEL:2 * D_MODEL], m[..., 2 * D_MODEL:]


def _even_layer(x, mod_a, mod_b, past, w):
    b, l, _ = x.shape
    shift, scale, gate = _split_mod(mod_a)
    q3, k3, k, v, z, xbc, small = _inproj_call(
        _inproj_even_kernel, "inproj_even", x, w['norm_g0'], shift, scale, w['w_in'],
        [w['head_mean'], w['q_gain'], w['k_gain'], w['small_bias']],
        [FOX_HEADS * FOX_HD3, FOX_HEADS * FOX_HD3, FOX_DIM, FOX_DIM, SSD_INNER, SSD_CONV_DIM, LANES],
        [BF16, BF16, F32, F32, F32, F32, F32])
    if past['fox_k'] is None:
        cum, cumt = _fox_cumulative(small, None)
        fox = _fox_attention_prompt(q3, k3, v, cum, cumt)
    else:
        n_past = past['fox_k'].shape[1]
        past_logf = jnp.pad(past['fox_logf'].astype(F32), ((0, 0), (0, 0), (0, LANES - FOX_HEADS)))
        cum, cumt = _fox_cumulative(small, past_logf)
        fox = _fox_attention_decode(q3, k3, v, past['fox_k'].reshape(b, n_past, FOX_DIM),
                                    past['fox_v'].reshape(b, n_past, FOX_DIM), cum, cumt)
    y, conv_new, ssd_new = _ssd_mixer(xbc, z, small, past['ssd_conv'], past['ssd'], w['ssd_conv_w'], w['ssd_conv_b'],
                                      w['ssd_A_log'], w['ssd_D'], w['ssd_norm_g'])
    shift2, scale2, _ = _split_mod(mod_b)
    x, h, logits = _outproj_residual(fox, y, w['w_out'], x, gate, w['norm_g1'], shift2, scale2, w['router_w'],
                                     split_inputs=True)
    new = dict(fox_k=k.reshape(b, l, FOX_HEADS, FOX_HD), fox_v=v.reshape(b, l, FOX_HEADS, FOX_HD),
               fox_logf=small[..., :FOX_HEADS], ssd_conv=conv_new, ssd=ssd_new)
    return x, h, logits, new


def _odd_layer(x, mod_a, mod_b, past, w):
    shift, scale, gate = _split_mod(mod_a)
    rw, qkv, gz, small = _inproj_call(
        _inproj_odd_kernel, "inproj_odd", x, w['norm_g0'], shift, scale, w['w_in'],
        [w['small_bias'], w['small_alog']], [RWKV_PROJ, GDN_CONV_DIM, GDN_DIM, LANES], [F32, F32, F32, F32])
    o_rwkv, rwkv_new = _rwkv_mixer(rw, past['rwkv_shift'], past['rwkv'], w['p'])
    o_gdn, gconv_new, gdn_new = _gdn_mixer(qkv, gz, small, past['gdn_conv'], past['gdn'], w['p']['gdn_conv_w'],
                                           w['p']['gdn_norm_g'])
    shift2, scale2, _ = _split_mod(mod_b)
    x, h, logits = _outproj_residual(o_rwkv, o_gdn, w['w_out'], x, gate, w['norm_g1'], shift2, scale2, w['router_w'],
                                     split_inputs=False)
    new = dict(rwkv_shift=rw[:, -1], rwkv=rwkv_new, gdn_conv=gconv_new, gdn=gdn_new)
    return x, h, logits, new


def _run_trunk(x, mods, past_even, past_odd, layers):
    new_even, new_odd = [], []
    for i, w in enumerate(layers):
        j = i // 2
        if i % 2 == 0:
            past = {n: (None if a is None else a[j]) for n, a in past_even.items()}
            x, h, logits, st = _even_layer(x, mods[2 * i], mods[2 * i + 1], past, w)
            new_even.append(st)
        else:
            x, h, logits, st = _odd_layer(x, mods[2 * i], mods[2 * i + 1], {n: a[j] for n, a in past_odd.items()}, w)
            new_odd.append(st)
        x = _moe(h, logits, x, _split_mod(mods[2 * i + 1])[2], *w['moe'])
    stack = lambda lst: {n: jnp.stack([d[n] for d in lst]) for n in lst[0]}
    return x, stack(new_even), stack(new_odd)


def kernel(x_prompt, x_sample, cache_fox_k, cache_fox_v, cache_fox_logf, state_ssd_conv, state_ssd, state_rwkv_shift, state_rwkv, state_gdn_conv, state_gdn, c_prompt, c_sample, ada_w, ada_b, norm_g, even_w_in, even_w_out, fox_q_g, fox_k_g, fox_f_b, ssd_conv_w, ssd_conv_b, ssd_dt_bias, ssd_A_log, ssd_D, ssd_norm_g, odd_w_in, odd_w_out, rwkv_mu, rwkv_w0, rwkv_w2, rwkv_a0, rwkv_a2, rwkv_g2, rwkv_k_k, rwkv_k_a, rwkv_r_k, rwkv_ln_g, rwkv_ln_b, gdn_conv_w, gdn_A_log, gdn_dt_bias, gdn_norm_g, router_w, router_b, moe_w1, moe_w3, moe_w2, shared_w1, shared_w3, shared_w2):
    depth = ada_w.shape[0]
    ne, no = cache_fox_k.shape[0], state_rwkv.shape[0]
    bp, bs = x_prompt.shape[0], x_sample.shape[0]
    odd_params = dict(rwkv_mu=rwkv_mu, rwkv_w0=rwkv_w0, rwkv_w2=rwkv_w2, rwkv_a0=rwkv_a0, rwkv_a2=rwkv_a2, rwkv_g2=rwkv_g2,
                      rwkv_k_k=rwkv_k_k, rwkv_k_a=rwkv_k_a, rwkv_r_k=rwkv_r_k, rwkv_ln_g=rwkv_ln_g, rwkv_ln_b=rwkv_ln_b,
                      gdn_conv_w=gdn_conv_w, gdn_norm_g=gdn_norm_g)
    layers = []
    for i in range(depth):
        j = i // 2
        w = dict(norm_g0=norm_g[i, 0][None], norm_g1=norm_g[i, 1][None], router_w=router_w[i],
                 moe=(router_b[i], moe_w1[i].astype(BF16), moe_w3[i].astype(BF16), moe_w2[i].astype(BF16),
                      shared_w1[i].astype(BF16), shared_w3[i].astype(BF16), shared_w2[i].astype(BF16)))
        if i % 2 == 0:
            w.update(w_in=_pack_even_w_in(even_w_in[j]), w_out=even_w_out[j],
                     head_mean=_head_mean_matrix(FOX_DIM, FOX_HD),
                     q_gain=jnp.tile(fox_q_g[j], FOX_HEADS)[None], k_gain=jnp.tile(fox_k_g[j], FOX_HEADS)[None],
                     small_bias=_pad_lanes(fox_f_b[j], ssd_dt_bias[j]),
                     ssd_conv_w=ssd_conv_w[j], ssd_conv_b=ssd_conv_b[j], ssd_A_log=ssd_A_log[j], ssd_D=ssd_D[j],
                     ssd_norm_g=ssd_norm_g[j])
        else:
            w.update(w_in=_pack_odd_w_in(odd_w_in[j]), w_out=odd_w_out[j],
                     small_bias=_pad_lanes(jnp.zeros(GDN_HEADS), gdn_dt_bias[j]),
                     small_alog=_pad_lanes(jnp.zeros(GDN_HEADS), gdn_A_log[j]),
                     p={n: a[j] for n, a in odd_params.items()})
        layers.append(w)

    mods = _ada_modulation(jnp.concatenate([c_prompt, c_sample], axis=0), ada_w, ada_b)
    mods_p, mods_s = mods[:, :bp], mods[:, bp:]

    prompt_even = dict(fox_k=None, fox_v=None, fox_logf=None,
                       ssd_conv=jnp.zeros((ne, bp) + state_ssd_conv.shape[2:], F32),
                       ssd=jnp.zeros((ne, bp) + state_ssd.shape[2:], F32))
    prompt_odd = dict(rwkv_shift=jnp.zeros((no, bp) + state_rwkv_shift.shape[2:], F32),
                      rwkv=jnp.zeros((no, bp) + state_rwkv.shape[2:], F32),
                      gdn_conv=jnp.zeros((no, bp) + state_gdn_conv.shape[2:], F32),
                      gdn=jnp.zeros((no, bp) + state_gdn.shape[2:], F32))
    sample_even = dict(fox_k=cache_fox_k, fox_v=cache_fox_v, fox_logf=cache_fox_logf, ssd_conv=state_ssd_conv, ssd=state_ssd)
    sample_odd = dict(rwkv_shift=state_rwkv_shift, rwkv=state_rwkv, gdn_conv=state_gdn_conv, gdn=state_gdn)
    y_prompt, pe, po = _run_trunk(x_prompt, mods_p, prompt_even, prompt_odd, layers)
    y_sample, se, so = _run_trunk(x_sample, mods_s, sample_even, sample_odd, layers)
    return (y_prompt, y_sample,
            pe['fox_k'], pe['fox_v'], pe['fox_logf'], pe['ssd_conv'], pe['ssd'],
            po['rwkv_shift'], po['rwkv'], po['gdn_conv'], po['gdn'],
            se['fox_k'], se['fox_v'], se['fox_logf'], se['ssd_conv'], se['ssd'],
            so['rwkv_shift'], so['rwkv'], so['gdn_conv'], so['gdn'])
```

```python
import functools
import math

import jax
import jax.numpy as jnp
from jax import lax
from jax.experimental import pallas as pl
from jax.experimental.pallas import tpu as pltpu

F32 = jnp.float32
BF16 = jnp.bfloat16
I32 = jnp.int32

D_MODEL = 1024
CONV_W = 4
EPS = 1e-6

FOX_HEADS = 8
FOX_HD = 64
FOX_DIM = FOX_HEADS * FOX_HD
FOX_SCALE = 1.0 / math.sqrt(FOX_HD)

SSD_HEADS = 8
SSD_HD = 64
SSD_INNER = SSD_HEADS * SSD_HD
SSD_GROUPS = 2
SSD_N = 128
SSD_CONV_DIM = SSD_INNER + 2 * SSD_GROUPS * SSD_N

RWKV_HEADS = 8
RWKV_HD = 64
RWKV_DIM = RWKV_HEADS * RWKV_HD
RWKV_DECAY_LORA = 64
RWKV_A_LORA = 64
RWKV_GATE_LORA = 128
RWKV_PROJ = 3 * RWKV_DIM + RWKV_DECAY_LORA + RWKV_A_LORA + RWKV_GATE_LORA
RWKV_GN_EPS = 64e-5

GDN_HEADS = 4
GDN_HD = 128
GDN_DIM = GDN_HEADS * GDN_HD
GDN_CONV_DIM = 3 * GDN_DIM

N_EXPERTS = 64
TOP_K = 8
N_EXPERT_GROUPS = 8
TOPK_GROUPS = 4
EXPERTS_PER_GROUP = N_EXPERTS // N_EXPERT_GROUPS
D_EXPERT = 256
D_SHARED = 256
ROUTED_SCALE = 2.5
MOE_BLOCK = 128

LANES = 128
VMEM_LIMIT_BYTES = 56 * 1024 * 1024


def _params(*semantics):
    return pltpu.CompilerParams(dimension_semantics=semantics, vmem_limit_bytes=VMEM_LIMIT_BYTES)


def _const_spec(a):
    return pl.BlockSpec(a.shape, lambda *_: (0,) * a.ndim, pipeline_mode=pl.Buffered(1))


def _mm(a, b):
    return jnp.dot(a.astype(BF16), b.astype(BF16), preferred_element_type=F32)


def _mm_nt(a, b):
    return lax.dot_general(a.astype(BF16), b.astype(BF16), (((1,), (1,)), ((), ())), preferred_element_type=F32)


def _mm_tn(a, b):
    return lax.dot_general(a.astype(BF16), b.astype(BF16), (((0,), (0,)), ((), ())), preferred_element_type=F32)


def _split_hi_lo(x):
    hi = x.astype(BF16)
    return hi, (x - hi.astype(F32)).astype(BF16)


def _dot3(dims, a, b, batch=((), ())):
    a_hi, a_lo = _split_hi_lo(a)
    b_hi, b_lo = _split_hi_lo(b)
    dot = lambda u, w: lax.dot_general(u, w, (dims, batch), preferred_element_type=F32)
    return dot(a_hi, b_hi) + (dot(a_hi, b_lo) + dot(a_lo, b_hi))


_mm3 = functools.partial(_dot3, ((1,), (0,)))
_mm3_nt = functools.partial(_dot3, ((1,), (1,)))
_mm3_tn = functools.partial(_dot3, ((0,), (0,)))

_HEAD_BATCH = ((0,), (0,))


def _bdot(dims, a, b):
    return lax.dot_general(a.astype(BF16), b.astype(BF16), (dims, _HEAD_BATCH), preferred_element_type=F32)


_bmm = functools.partial(_bdot, ((2,), (1,)))
_bmm_nt = functools.partial(_bdot, ((2,), (2,)))
_bmm_tn = functools.partial(_bdot, ((1,), (1,)))
_bmm3 = functools.partial(_dot3, ((2,), (1,)), batch=_HEAD_BATCH)


def _heads(x, hd):
    return jnp.stack([x[:, h * hd:(h + 1) * hd] for h in range(x.shape[1] // hd)], axis=0)


def _unheads(x):
    return jnp.concatenate([x[h] for h in range(x.shape[0])], axis=-1)


def _cat3_lhs(x):
    hi, lo = _split_hi_lo(x)
    return jnp.concatenate([hi, hi, lo], axis=-1)


def _split_hi_lo_outside(w):
    w = w.astype(F32)
    hi = lax.bitcast_convert_type(lax.bitcast_convert_type(w, jnp.uint32) & jnp.uint32(0xFFFF0000), F32)
    return hi.astype(BF16), (w - hi).astype(BF16)


def _cat3_rhs(w, axis=0):
    hi, lo = _split_hi_lo_outside(w)
    return jnp.concatenate([hi, lo, hi], axis=axis)


def _sigmoid(x):
    return 1.0 / (1.0 + jnp.exp(-x))


def _silu(x):
    return x * _sigmoid(x)


def _softplus(x):
    return jnp.maximum(x, 0.0) + jnp.log1p(jnp.exp(-jnp.abs(x)))


def _cumsum_rows(x):
    n = x.shape[0]
    row = lax.broadcasted_iota(I32, x.shape, 0)
    s = 1
    while s < n:
        x = x + jnp.where(row >= s, pltpu.roll(x, s, 0), 0.0)
        s *= 2
    return x


def _ada_norm(x, g, shift, scale):
    y = x * lax.rsqrt(jnp.mean(x * x, axis=-1, keepdims=True) + EPS)
    return (y * g) * (1.0 + scale) + shift


def _ada_kernel(c_ref, w_ref, b_ref, o_ref):
    o_ref[0] = _mm3(_silu(c_ref[...]), w_ref[0]) + b_ref[0]


def _ada_modulation(c_all, ada_w, ada_b):
    n = c_all.shape[0]
    n_mod = ada_w.shape[0] * ada_w.shape[1]
    w = ada_w.reshape(n_mod, D_MODEL, 3 * D_MODEL)
    b = ada_b.reshape(n_mod, 1, 3 * D_MODEL)
    return pl.pallas_call(
        _ada_kernel,
        out_shape=jax.ShapeDtypeStruct((n_mod, n, 3 * D_MODEL), F32),
        grid=(n_mod, 3),
        in_specs=[
            pl.BlockSpec((n, D_MODEL), lambda i, j: (0, 0)),
            pl.BlockSpec((1, D_MODEL, D_MODEL), lambda i, j: (i, 0, j)),
            pl.BlockSpec((1, 1, D_MODEL), lambda i, j: (i, 0, j)),
        ],
        out_specs=pl.BlockSpec((1, n, D_MODEL), lambda i, j: (i, 0, j)),
        compiler_params=_params("parallel", "parallel"),
        name="ada_modulation",
    )(c_all, w, b)


ROW_TILE = 256

EV_Q, EV_K, EV_V, EV_Z, EV_XBC, EV_SMALL = 0, FOX_DIM, 2 * FOX_DIM, 3 * FOX_DIM, 3 * FOX_DIM + SSD_INNER, 3 * FOX_DIM + SSD_INNER + SSD_CONV_DIM
EV_COLS = EV_SMALL + LANES
OD_RW, OD_QKV, OD_Z, OD_SMALL = 0, RWKV_PROJ, RWKV_PROJ + GDN_CONV_DIM, RWKV_PROJ + GDN_CONV_DIM + GDN_DIM
OD_COLS = OD_SMALL + LANES


def _row_blocking(b, l):
    if l >= ROW_TILE:
        assert l % ROW_TILE == 0
        return 1, ROW_TILE
    assert ROW_TILE % l == 0 and b % (ROW_TILE // l) == 0 and l % 8 == 0
    return ROW_TILE // l, l


def _normed_rows(x_ref, g_ref, sh_ref, sc_ref):
    x = x_ref[...]
    h = _ada_norm(x, g_ref[...], sh_ref[...], sc_ref[...])
    return h.reshape(x.shape[0] * x.shape[1], x.shape[2])


def _head_rms(y, head_mean, gain):
    hi, lo = _split_hi_lo(y * y)
    ms = jnp.dot(hi, head_mean, preferred_element_type=F32) + jnp.dot(lo, head_mean, preferred_element_type=F32)
    return y * lax.rsqrt(ms + EPS) * gain


FOX_HD3 = 3 * FOX_HD


def _head_cat3(x, rhs):
    hi, lo = _split_hi_lo(x)
    pieces = []
    for h in range(FOX_HEADS):
        hs = slice(h * FOX_HD, (h + 1) * FOX_HD)
        pieces += [hi[:, hs], lo[:, hs], hi[:, hs]] if rhs else [hi[:, hs], hi[:, hs], lo[:, hs]]
    return jnp.concatenate(pieces, axis=-1)


def _inproj_even_kernel(x_ref, g_ref, sh_ref, sc_ref, w_ref, hm_ref, qg_ref, kg_ref, bias_ref,
                        q3_ref, k3_ref, k_ref, v_ref, z_ref, xbc_ref, small_ref):
    h = _cat3_lhs(_normed_rows(x_ref, g_ref, sh_ref, sc_ref))
    shp3 = lambda r: (x_ref.shape[0], x_ref.shape[1], r.shape[2])
    cols = lambda lo, hi: jnp.dot(h, w_ref[:, lo:hi], preferred_element_type=F32)
    hm = hm_ref[...]
    q = _head_rms(cols(EV_Q, EV_K), hm, qg_ref[...]) * FOX_SCALE
    q3_ref[...] = _head_cat3(q, rhs=False).reshape(shp3(q3_ref))
    k = _head_rms(cols(EV_K, EV_V), hm, kg_ref[...])
    k_ref[...] = k.reshape(shp3(k_ref))
    k3_ref[...] = _head_cat3(k, rhs=True).reshape(shp3(k3_ref))
    v_ref[...] = cols(EV_V, EV_Z).reshape(shp3(v_ref))
    z_ref[...] = cols(EV_Z, EV_XBC).reshape(shp3(z_ref))
    xbc_ref[...] = cols(EV_XBC, EV_SMALL).reshape(shp3(xbc_ref))
    t = cols(EV_SMALL, EV_COLS) + bias_ref[...]
    lane = lax.broadcasted_iota(I32, t.shape, 1)
    small = jnp.where(lane < FOX_HEADS, -_softplus(-t), jnp.where(lane < FOX_HEADS + SSD_HEADS, _softplus(t), 0.0))
    small_ref[...] = small.reshape(shp3(small_ref))


def _inproj_odd_kernel(x_ref, g_ref, sh_ref, sc_ref, w_ref, bias_ref, alog_ref,
                       rw_ref, qkv_ref, z_ref, small_ref):
    h = _normed_rows(x_ref, g_ref, sh_ref, sc_ref).astype(BF16)
    shp3 = lambda r: (x_ref.shape[0], x_ref.shape[1], r.shape[2])
    rw_ref[...] = jnp.dot(h, w_ref[:, OD_RW:OD_QKV], preferred_element_type=F32).reshape(shp3(rw_ref))
    qkv_ref[...] = jnp.dot(h, w_ref[:, OD_QKV:OD_Z], preferred_element_type=F32).reshape(shp3(qkv_ref))
    z_ref[...] = jnp.dot(h, w_ref[:, OD_Z:OD_SMALL], preferred_element_type=F32).reshape(shp3(z_ref))
    t = jnp.dot(h, w_ref[:, OD_SMALL:OD_COLS], preferred_element_type=F32) + bias_ref[...]
    lane = lax.broadcasted_iota(I32, t.shape, 1)
    small = jnp.where(lane < GDN_HEADS, _sigmoid(t),
                      jnp.where(lane < 2 * GDN_HEADS, -jnp.exp(alog_ref[...]) * _softplus(t), 0.0))
    small_ref[...] = small.reshape(shp3(small_ref))


def _inproj_call(kernel_fn, name, x, g, shift, scale, w, extras, out_cols, out_dtypes):
    b, l, d = x.shape
    kb, tl = _row_blocking(b, l)
    tok = lambda c: pl.BlockSpec((kb, tl, c), lambda i, j: (i, j, 0))
    per_batch = pl.BlockSpec((kb, 1, d), lambda i, j: (i, 0, 0))
    const = _const_spec
    return pl.pallas_call(
        kernel_fn,
        out_shape=[jax.ShapeDtypeStruct((b, l, c), dt) for c, dt in zip(out_cols, out_dtypes)],
        grid=(b // kb, l // tl),
        in_specs=[tok(d), const(g), per_batch, per_batch, const(w)] + [const(e) for e in extras],
        out_specs=[tok(c) for c in out_cols],
        compiler_params=_params("parallel", "parallel"),
        name=name,
    )(x, g, shift, scale, w, *extras)


def _pad_lanes(*pieces):
    v = jnp.concatenate([p.reshape(-1).astype(F32) for p in pieces])
    return jnp.pad(v, (0, LANES - v.shape[0])).reshape(1, LANES)


def _head_mean_matrix(dim, hd):
    r = jnp.arange(dim) // hd
    return jnp.where(r[:, None] == r[None, :], 1.0 / hd, 0.0).astype(BF16)


def _pack_even_w_in(w_in):
    q, k, v, f, z, xbc, dt = jnp.split(w_in, (FOX_DIM, 2 * FOX_DIM, 3 * FOX_DIM, 3 * FOX_DIM + FOX_HEADS,
                                              3 * FOX_DIM + FOX_HEADS + SSD_INNER,
                                              3 * FOX_DIM + FOX_HEADS + SSD_INNER + SSD_CONV_DIM), axis=1)
    pad = jnp.zeros((w_in.shape[0], LANES - FOX_HEADS - SSD_HEADS), w_in.dtype)
    return _cat3_rhs(jnp.concatenate([q, k, v, z, xbc, f, dt, pad], axis=1))


def _pack_odd_w_in(w_in):
    main, small = w_in[:, :OD_SMALL], w_in[:, OD_SMALL:]
    pad = jnp.zeros((w_in.shape[0], LANES - 2 * GDN_HEADS), w_in.dtype)
    return jnp.concatenate([main, small, pad], axis=1).astype(BF16)


def _split3_bf16(x):
    c1 = x.astype(BF16)
    r1 = x - c1.astype(F32)
    c2 = r1.astype(BF16)
    c3 = (r1 - c2.astype(F32)).astype(BF16)
    return c1, c2, c3


def _rows_to_lanes(sel, x):
    nt = lambda a, b: lax.dot_general(a, b, (((1,), (1,)), ((), ())), preferred_element_type=F32)
    c1, c2, c3 = _split3_bf16(x)
    return (nt(sel, c1) + nt(sel, c2)) + nt(sel, c3)


def _fox_cum_kernel(*refs, n_past):
    if n_past:
        past_ref, small_ref, sel_ref, cum_ref, cumt_ref = refs
        seq = jnp.concatenate([past_ref[0], small_ref[0]], axis=0)
    else:
        small_ref, sel_ref, cum_ref, cumt_ref = refs
        seq = small_ref[0]
    cum = _cumsum_rows(seq)
    cum_ref[0] = cum[n_past:]
    cumt_ref[0] = _rows_to_lanes(sel_ref[...], cum)


def _fox_cumulative(small, past_logf):
    b, l, _ = small.shape
    n_past = 0 if past_logf is None else past_logf.shape[1]
    sel = jnp.eye(FOX_HEADS, LANES, dtype=BF16)
    ins, specs = [], []
    if n_past:
        ins.append(past_logf)
        specs.append(pl.BlockSpec((1, n_past, LANES), lambda i: (i, 0, 0)))
    ins += [small, sel]
    specs += [pl.BlockSpec((1, l, LANES), lambda i: (i, 0, 0)), pl.BlockSpec(sel.shape, lambda i: (0, 0))]
    return pl.pallas_call(
        functools.partial(_fox_cum_kernel, n_past=n_past),
        out_shape=[jax.ShapeDtypeStruct((b, l, LANES), F32), jax.ShapeDtypeStruct((b, FOX_HEADS, n_past + l), F32)],
        grid=(b,),
        in_specs=specs,
        out_specs=[pl.BlockSpec((1, l, LANES), lambda i: (i, 0, 0)),
                   pl.BlockSpec((1, FOX_HEADS, n_past + l), lambda i: (i, 0, 0))],
        compiler_params=_params("parallel"),
        name="fox_cumulative",
    )(*ins)


FOX_TILE = 256


def _fox_flash_kernel(q_ref, k_ref, v_ref, cq_ref, ck_ref, o_ref, m_sc, l_sc, acc_sc):
    qi, kj = pl.program_id(1), pl.program_id(2)

    @pl.when(kj == 0)
    def _():
        m_sc[...] = jnp.full(m_sc.shape, -jnp.inf, F32)
        l_sc[...] = jnp.zeros(l_sc.shape, F32)
        acc_sc[...] = jnp.zeros(acc_sc.shape, F32)

    @pl.when(kj <= qi)
    def _():
        q3, k3 = q_ref[0], k_ref[0]
        v_hi, v_lo = _split_hi_lo(v_ref[0])
        cq, ck = cq_ref[0], ck_ref[0]
        row = lax.broadcasted_iota(I32, (FOX_TILE, FOX_TILE), 0)
        col = lax.broadcasted_iota(I32, (FOX_TILE, FOX_TILE), 1)
        visible = jnp.logical_or(col <= row, kj < qi)
        for h in range(FOX_HEADS):
            hs = slice(h * FOX_HD, (h + 1) * FOX_HD)
            hs3 = slice(h * FOX_HD3, (h + 1) * FOX_HD3)
            s = _scores3(q3[:, hs3], k3[:, hs3]) + (cq[:, h:h + 1] - ck[h:h + 1, :])
            s = jnp.where(visible, s, -jnp.inf)
            m_prev = m_sc[h]
            m_new = jnp.maximum(m_prev, jnp.max(s, axis=-1, keepdims=True))
            p = jnp.exp(s - m_new)
            alpha = jnp.exp(m_prev - m_new)
            l_sc[h] = alpha * l_sc[h] + jnp.sum(p, axis=-1, keepdims=True)
            acc_sc[h] = alpha * acc_sc[h] + _pv3(p, v_hi[:, hs], v_lo[:, hs])
            m_sc[h] = m_new

    @pl.when(kj == qi)
    def _():
        o_ref[0] = jnp.concatenate([acc_sc[h] / l_sc[h] for h in range(FOX_HEADS)], axis=-1)


def _scores3(q3, k3):
    return lax.dot_general(q3, k3, (((1,), (1,)), ((), ())), preferred_element_type=F32)


def _pv3(p, v_hi, v_lo):
    p_hi, p_lo = _split_hi_lo(p)
    dot = lambda a, b: jnp.dot(a, b, preferred_element_type=F32)
    return dot(p_hi, v_hi) + (dot(p_hi, v_lo) + dot(p_lo, v_hi))


def _fox_attention_prompt(q3, k3, v, cum, cumt):
    b, l, _ = v.shape
    n = l // FOX_TILE
    qspec = lambda c: pl.BlockSpec((1, FOX_TILE, c), lambda i, j, t: (i, j, 0))
    kspec = lambda c: pl.BlockSpec((1, FOX_TILE, c), lambda i, j, t: (i, jnp.minimum(t, j), 0))
    return pl.pallas_call(
        _fox_flash_kernel,
        out_shape=jax.ShapeDtypeStruct((b, l, FOX_DIM), F32),
        grid=(b, n, n),
        in_specs=[qspec(FOX_HEADS * FOX_HD3), kspec(FOX_HEADS * FOX_HD3), kspec(FOX_DIM), qspec(LANES),
                  pl.BlockSpec((1, FOX_HEADS, FOX_TILE), lambda i, j, t: (i, 0, jnp.minimum(t, j)))],
        out_specs=qspec(FOX_DIM),
        scratch_shapes=[pltpu.VMEM((FOX_HEADS, FOX_TILE, 1), F32), pltpu.VMEM((FOX_HEADS, FOX_TILE, 1), F32),
                        pltpu.VMEM((FOX_HEADS, FOX_TILE, FOX_HD), F32)],
        compiler_params=_params("parallel", "parallel", "arbitrary"),
        name="fox_flash",
    )(q3, k3, v, cum, cumt)


def _fox_decode_kernel(q_ref, kp_ref, vp_ref, kn_ref, vn_ref, cq_ref, ck_ref, o_ref):
    n_past, l = kp_ref.shape[1], q_ref.shape[1]
    q3, kn3 = q_ref[0], kn_ref[0]
    kp3 = _head_cat3(kp_ref[0], rhs=True)
    vp_hi, vp_lo = _split_hi_lo(vp_ref[0])
    vn_hi, vn_lo = _split_hi_lo(vn_ref[0])
    cq, ck = cq_ref[0], ck_ref[0]
    row = lax.broadcasted_iota(I32, (l, l), 0)
    col = lax.broadcasted_iota(I32, (l, l), 1)
    outs = []
    for h in range(FOX_HEADS):
        hs = slice(h * FOX_HD, (h + 1) * FOX_HD)
        hs3 = slice(h * FOX_HD3, (h + 1) * FOX_HD3)
        cqh = cq[:, h:h + 1]
        s_past = _scores3(q3[:, hs3], kp3[:, hs3]) + (cqh - ck[h:h + 1, :n_past])
        s_new = _scores3(q3[:, hs3], kn3[:, hs3]) + (cqh - ck[h:h + 1, n_past:])
        s_new = jnp.where(col <= row, s_new, -jnp.inf)
        m = jnp.maximum(jnp.max(s_past, axis=-1, keepdims=True), jnp.max(s_new, axis=-1, keepdims=True))
        p_past, p_new = jnp.exp(s_past - m), jnp.exp(s_new - m)
        denom = jnp.sum(p_past, axis=-1, keepdims=True) + jnp.sum(p_new, axis=-1, keepdims=True)
        outs.append((_pv3(p_past, vp_hi[:, hs], vp_lo[:, hs]) + _pv3(p_new, vn_hi[:, hs], vn_lo[:, hs])) / denom)
    o_ref[0] = jnp.concatenate(outs, axis=-1)


def _fox_attention_decode(q3, k3, v, k_past, v_past, cum, cumt):
    b, l, _ = v.shape
    n_past = k_past.shape[1]
    new = lambda c: pl.BlockSpec((1, l, c), lambda i: (i, 0, 0))
    past = pl.BlockSpec((1, n_past, FOX_DIM), lambda i: (i, 0, 0))
    return pl.pallas_call(
        _fox_decode_kernel,
        out_shape=jax.ShapeDtypeStruct((b, l, FOX_DIM), F32),
        grid=(b,),
        in_specs=[new(FOX_HEADS * FOX_HD3), past, past, new(FOX_HEADS * FOX_HD3), new(FOX_DIM), new(LANES),
                  pl.BlockSpec((1, FOX_HEADS, n_past + l), lambda i: (i, 0, 0))],
        out_specs=new(FOX_DIM),
        compiler_params=_params("parallel"),
        name="fox_decode",
    )(q3, k_past, v_past, k3, v, cum, cumt)


CONV_TAIL = 8


def _causal_conv(x, tail, w):
    q = x.shape[0]
    row = lax.broadcasted_iota(I32, (CONV_TAIL, x.shape[1]), 0)
    acc = x * w[CONV_W - 1:CONV_W]
    for s in range(1, CONV_W):
        xr = pltpu.roll(x, s, 0)
        head = jnp.where(row < s, pltpu.roll(tail, s, 0), xr[:CONV_TAIL])
        shifted = head if q == CONV_TAIL else jnp.concatenate([head, xr[CONV_TAIL:]], axis=0)
        acc = acc + shifted * w[CONV_W - 1 - s:CONV_W - s]
    return acc


def _conv_tail_from_state(state):
    return jnp.pad(state, ((0, 0), (CONV_TAIL - (CONV_W - 1), 0), (0, 0)))


SSD_CHUNK = 128
SSD_DT_LANE = FOX_HEADS


def _ssd_kernel(xbc_ref, z_ref, small_ref, tail0_ref, h0_ref, cw_ref, cb_ref, alog_ref, drep_ref, ng_ref, sel_ref,
                y_ref, tail_ref, hout_ref, tail_sc, h_sc):
    c = pl.program_id(1)

    @pl.when(c == 0)
    def _():
        tail_sc[...] = tail0_ref[0]
        h_sc[...] = h0_ref[0]

    x = xbc_ref[0]
    q = x.shape[0]
    act = _silu(_causal_conv(x, tail_sc[...], cw_ref[...]) + cb_ref[...])
    tail_sc[...] = x[q - CONV_TAIL:]
    xs = act[:, :SSD_INNER]
    bm = act[:, SSD_INNER:SSD_INNER + SSD_GROUPS * SSD_N]
    cm = act[:, SSD_INNER + SSD_GROUPS * SSD_N:]

    small = small_ref[0]
    lane = lax.broadcasted_iota(I32, small.shape, 1)
    is_dt = jnp.logical_and(lane >= SSD_DT_LANE, lane < SSD_DT_LANE + SSD_HEADS)
    da = jnp.where(is_dt, small * -jnp.exp(alog_ref[...]), 0.0)
    acum = _cumsum_rows(da)
    acum_t = _rows_to_lanes(sel_ref[...], acum)
    trow = lax.broadcasted_iota(I32, (q, q), 0)
    tcol = lax.broadcasted_iota(I32, (q, q), 1)
    heads_per_group = SSD_HEADS // SSD_GROUPS
    ys = []
    for g in range(SSD_GROUPS):
        bg = bm[:, g * SSD_N:(g + 1) * SSD_N]
        cg = cm[:, g * SSD_N:(g + 1) * SSD_N]
        cb = _mm3_nt(cg, bg)
        for h in range(g * heads_per_group, (g + 1) * heads_per_group):
            a_col = acum[:, SSD_DT_LANE + h:SSD_DT_LANE + h + 1]
            dt_col = small[:, SSD_DT_LANE + h:SSD_DT_LANE + h + 1]
            a_last = a_col[q - 1:q]
            seg = jnp.exp(jnp.where(trow >= tcol, a_col - acum_t[h:h + 1, :], -jnp.inf))
            xdt = xs[:, h * SSD_HD:(h + 1) * SSD_HD] * dt_col
            state = h_sc[h]
            ys.append(_mm3(cb * seg, xdt) + jnp.exp(a_col) * _mm3_nt(cg, state))
            h_sc[h] = state * jnp.exp(a_last) + _mm3_tn(xdt * jnp.exp(a_last - a_col), bg)
    y = jnp.concatenate(ys, axis=-1) + drep_ref[...] * xs
    y = y * _silu(z_ref[0])
    gw = SSD_INNER // SSD_GROUPS
    normed = []
    for g in range(SSD_GROUPS):
        yg = y[:, g * gw:(g + 1) * gw]
        normed.append(yg * lax.rsqrt(jnp.mean(yg * yg, axis=-1, keepdims=True) + EPS))
    y_ref[0] = (jnp.concatenate(normed, axis=-1) * ng_ref[...]).astype(y_ref.dtype)

    @pl.when(c == pl.num_programs(1) - 1)
    def _():
        tail_ref[0] = tail_sc[...]
        hout_ref[0] = h_sc[...]


def _ssd_mixer(xbc, z, small, conv_state, h0, conv_w, conv_b, a_log, d_skip, norm_g):
    b, l, _ = xbc.shape
    q = min(SSD_CHUNK, l)
    tail0 = _conv_tail_from_state(conv_state)
    alog = _pad_lanes(jnp.zeros(SSD_DT_LANE), a_log)
    drep = jnp.repeat(d_skip.astype(F32), SSD_HD)[None]
    sel = jnp.eye(SSD_HEADS, LANES, k=SSD_DT_LANE, dtype=BF16)
    tok = lambda c: pl.BlockSpec((1, q, c), lambda i, j: (i, j, 0))
    per_b = lambda a: pl.BlockSpec((1,) + a.shape[1:], lambda i, j: (i,) + (0,) * (a.ndim - 1))
    const = _const_spec
    consts = [conv_w, conv_b[None], alog, drep, norm_g[None], sel]
    y, tail, h_new = pl.pallas_call(
        _ssd_kernel,
        out_shape=[jax.ShapeDtypeStruct((b, l, SSD_INNER), F32),
                   jax.ShapeDtypeStruct((b, CONV_TAIL, SSD_CONV_DIM), F32),
                   jax.ShapeDtypeStruct(h0.shape, F32)],
        grid=(b, l // q),
        in_specs=[tok(SSD_CONV_DIM), tok(SSD_INNER), tok(LANES), per_b(tail0), per_b(h0)] + [const(a) for a in consts],
        out_specs=[tok(SSD_INNER), per_b(tail0), per_b(h0)],
        scratch_shapes=[pltpu.VMEM((CONV_TAIL, SSD_CONV_DIM), F32), pltpu.VMEM(h0.shape[1:], F32)],
        compiler_params=_params("parallel", "arbitrary"),
        name="ssd_mixer",
    )(xbc, z, small, tail0, h0, *consts)
    return y, tail[:, CONV_TAIL - (CONV_W - 1):], h_new


REC_CHUNK = 64


INV_BASE = 8


def _unit_lower_inverse(n):
    c = n.shape[-1]
    mm = _mm3 if n.ndim == 2 else _bmm3
    row = lax.broadcasted_iota(I32, (c, c), 0)
    col = lax.broadcasted_iota(I32, (c, c), 1)
    shift = INV_BASE.bit_length() - 1
    same = lax.shift_right_logical(row, shift) == lax.shift_right_logical(col, shift)
    diag = jnp.where(same, n, 0.0)
    inv = jnp.where(row == col, 1.0, 0.0) + diag
    power = diag
    span = 2
    while span < INV_BASE:
        power = mm(power, power)
        inv = inv + mm(inv, power)
        span *= 2
    size = INV_BASE
    while size < c:
        shift += 1
        size *= 2
        merged = lax.shift_right_logical(row, shift) == lax.shift_right_logical(col, shift)
        off = jnp.where(jnp.logical_and(merged, jnp.logical_not(same)), n, 0.0)
        inv = inv + mm(mm(inv, off), inv)
        same = merged
    return inv


def _shift_rows_by_one(x, tail):
    q = x.shape[0]
    row = lax.broadcasted_iota(I32, (CONV_TAIL, x.shape[1]), 0)
    xr = pltpu.roll(x, 1, 0)
    head = jnp.where(row < 1, pltpu.roll(tail, 1, 0), xr[:CONV_TAIL])
    return head if q == CONV_TAIL else jnp.concatenate([head, xr[CONV_TAIL:]], axis=0)


RW_R, RW_K, RW_V, RW_LORA, RW_GATE = 0, RWKV_DIM, 2 * RWKV_DIM, 3 * RWKV_DIM, 3 * RWKV_DIM + RWKV_DECAY_LORA + RWKV_A_LORA


def _rwkv_kernel(rw_ref, tail0_ref, s0_ref, mu_ref, w0_ref, w2_ref, a0_ref, a2_ref, g2_ref, kk_ref, ka_ref, rk_ref,
                 lng_ref, lnb_ref, hsum_ref, o_ref, sout_ref, tail_sc, s_sc):
    c = pl.program_id(1)

    @pl.when(c == 0)
    def _():
        tail_sc[...] = tail0_ref[0]
        s_sc[...] = s0_ref[0]

    x = rw_ref[0]
    q = x.shape[0]
    mixed = x + (_shift_rows_by_one(x, tail_sc[...]) - x) * mu_ref[...]
    tail_sc[...] = x[q - CONV_TAIL:]
    r = mixed[:, RW_R:RW_K]
    k = mixed[:, RW_K:RW_V]
    v = mixed[:, RW_V:RW_LORA]
    lora = mixed[:, RW_LORA:RW_GATE]
    w_log = -_softplus(-(w0_ref[...] + _mm(jnp.tanh(lora), w2_ref[...]))) - 0.5
    logw = -jnp.exp(w_log)
    icl = _sigmoid(a0_ref[...] + _mm(lora, a2_ref[...]))
    out_gate = _mm(_sigmoid(mixed[:, RW_GATE:]), g2_ref[...])
    kk = k * kk_ref[...]
    kk = kk * lax.rsqrt(_mm(kk * kk, hsum_ref[...]) + EPS)
    k2 = k * (1.0 + (icl - 1.0) * ka_ref[...])
    cum = _cumsum_rows(logw)
    w_run = jnp.exp(cum)
    w_inv = jnp.exp(-cum)
    rt = r * w_run
    at = -kk * jnp.exp(cum - logw)
    bt = kk * icl * w_inv
    kt = k2 * w_inv
    trow = lax.broadcasted_iota(I32, (q, q), 0)
    tcol = lax.broadcasted_iota(I32, (q, q), 1)
    strict, incl = trow > tcol, trow >= tcol
    hd = RWKV_HD
    at, rt, bt, kt, vh, rh, k2h = (_heads(t, hd) for t in (at, rt, bt, kt, v, r, k2))
    a_ab = jnp.where(strict, _bmm_nt(at, bt), 0.0)
    a_ak = jnp.where(strict, _bmm_nt(at, kt), 0.0)
    a_rb = jnp.where(incl, _bmm_nt(rt, bt), 0.0)
    a_rk = jnp.where(incl, _bmm_nt(rt, kt), 0.0)
    s0 = s_sc[...]
    u = _bmm(_unit_lower_inverse(a_ab), _bmm_nt(at, s0) + _bmm(a_ak, vh))
    o = _bmm_nt(rt, s0) + (_bmm(a_rb, u) + _bmm(a_rk, vh))
    s_sc[...] = (s0 + (_bmm_tn(u, bt) + _bmm_tn(vh, kt))) * _heads(w_run[q - 1:q], hd)
    mean = jnp.mean(o, axis=-1, keepdims=True)
    var = jnp.mean(jnp.square(o - mean), axis=-1, keepdims=True)
    o = (o - mean) * lax.rsqrt(var + RWKV_GN_EPS) * _heads(lng_ref[...], hd) + _heads(lnb_ref[...], hd)
    bonus = jnp.sum(rh * k2h * _heads(rk_ref[...], hd), axis=-1, keepdims=True) * vh
    o_ref[0] = (_unheads(o + bonus) * out_gate).astype(o_ref.dtype)

    @pl.when(c == pl.num_programs(1) - 1)
    def _():
        sout_ref[0] = s_sc[...]


def _rwkv_mixer(rw, shift_state, s0, p):
    b, l, _ = rw.shape
    q = min(REC_CHUNK, l)
    tail0 = jnp.pad(shift_state[:, None, :], ((0, 0), (CONV_TAIL - 1, 0), (0, 0)))
    zeros = jnp.zeros((RWKV_DECAY_LORA, RWKV_DIM), F32)
    consts = [p['rwkv_mu'][None], p['rwkv_w0'][None],
              jnp.concatenate([p['rwkv_w2'], zeros], axis=0).astype(BF16), p['rwkv_a0'][None],
              jnp.concatenate([zeros, p['rwkv_a2']], axis=0).astype(BF16), p['rwkv_g2'].astype(BF16),
              p['rwkv_k_k'][None], p['rwkv_k_a'][None], p['rwkv_r_k'].reshape(1, RWKV_DIM),
              p['rwkv_ln_g'][None], p['rwkv_ln_b'][None], _head_mean_matrix(RWKV_DIM, RWKV_HD) * RWKV_HD]
    tok = lambda c: pl.BlockSpec((1, q, c), lambda i, j: (i, j, 0))
    per_b = lambda a: pl.BlockSpec((1,) + a.shape[1:], lambda i, j: (i,) + (0,) * (a.ndim - 1))
    const = _const_spec
    return pl.pallas_call(
        _rwkv_kernel,
        out_shape=[jax.ShapeDtypeStruct((b, l, RWKV_DIM), BF16), jax.ShapeDtypeStruct(s0.shape, F32)],
        grid=(b, l // q),
        in_specs=[tok(RWKV_PROJ), per_b(tail0), per_b(s0)] + [const(a) for a in consts],
        out_specs=[tok(RWKV_DIM), per_b(s0)],
        scratch_shapes=[pltpu.VMEM((CONV_TAIL, RWKV_PROJ), F32), pltpu.VMEM(s0.shape[1:], F32)],
        compiler_params=_params("parallel", "arbitrary"),
        name="rwkv_mixer",
    )(rw, tail0, s0, *consts)


GDN_BETA_LANE, GDN_G_LANE = 0, GDN_HEADS


def _l2norm(x):
    return x * lax.rsqrt(jnp.sum(x * x, axis=-1, keepdims=True) + EPS)


def _gdn_kernel(qkv_ref, z_ref, small_ref, tail0_ref, s0_ref, cw_ref, ng_ref, sel_ref,
                o_ref, tail_ref, sout_ref, tail_sc, s_sc):
    c = pl.program_id(1)

    @pl.when(c == 0)
    def _():
        tail_sc[...] = tail0_ref[0]
        s_sc[...] = s0_ref[0]

    x = qkv_ref[0]
    q = x.shape[0]
    act = _silu(_causal_conv(x, tail_sc[...], cw_ref[...]))
    tail_sc[...] = x[q - CONV_TAIL:]
    small = small_ref[0]
    z = z_ref[0]
    lane = lax.broadcasted_iota(I32, small.shape, 1)
    is_g = jnp.logical_and(lane >= GDN_G_LANE, lane < GDN_G_LANE + GDN_HEADS)
    gam = _cumsum_rows(jnp.where(is_g, small, 0.0))
    gam_t = _rows_to_lanes(sel_ref[...], gam)
    trow = lax.broadcasted_iota(I32, (q, q), 0)
    tcol = lax.broadcasted_iota(I32, (q, q), 1)
    heads = range(GDN_HEADS)
    qh = _l2norm(_heads(act[:, :GDN_DIM], GDN_HD)) * (GDN_HD ** -0.5)
    kh = _l2norm(_heads(act[:, GDN_DIM:2 * GDN_DIM], GDN_HD))
    vh = _heads(act[:, 2 * GDN_DIM:], GDN_HD)
    beta = jnp.stack([small[:, GDN_BETA_LANE + h:GDN_BETA_LANE + h + 1] for h in heads])
    g_col = jnp.stack([gam[:, GDN_G_LANE + h:GDN_G_LANE + h + 1] for h in heads])
    g_row = jnp.stack([gam_t[h:h + 1, :] for h in heads])
    g_last = g_col[:, q - 1:q]
    diff = g_col - g_row
    a_mat = _bmm_nt(kh, kh) * jnp.exp(jnp.where(trow > tcol, diff, -jnp.inf)) * beta
    eg = jnp.exp(g_col)
    uw = _bmm(_unit_lower_inverse(-a_mat), jnp.concatenate([vh * beta, kh * (beta * eg)], axis=2))
    qk = _bmm_nt(qh, kh) * jnp.exp(jnp.where(trow >= tcol, diff, -jnp.inf))
    state = s_sc[...]
    nu = uw[:, :, :GDN_HD] - _bmm(uw[:, :, GDN_HD:], state)
    o = _bmm(qh * eg, state) + _bmm(qk, nu)
    s_sc[...] = state * jnp.exp(g_last) + _bmm_tn(kh * jnp.exp(g_last - g_col), nu)
    o = o * lax.rsqrt(jnp.mean(o * o, axis=-1, keepdims=True) + EPS) * ng_ref[...]
    o_ref[0] = (_unheads(o) * _silu(z)).astype(o_ref.dtype)

    @pl.when(c == pl.num_programs(1) - 1)
    def _():
        tail_ref[0] = tail_sc[...]
        sout_ref[0] = s_sc[...]


def _gdn_mixer(qkv, z, small, conv_state, s0, conv_w, norm_g):
    b, l, _ = qkv.shape
    q = min(REC_CHUNK, l)
    tail0 = _conv_tail_from_state(conv_state)
    sel = jnp.eye(GDN_HEADS, LANES, k=GDN_G_LANE, dtype=BF16)
    consts = [conv_w, norm_g[None], sel]
    tok = lambda c: pl.BlockSpec((1, q, c), lambda i, j: (i, j, 0))
    per_b = lambda a: pl.BlockSpec((1,) + a.shape[1:], lambda i, j: (i,) + (0,) * (a.ndim - 1))
    const = _const_spec
    o, tail, s_new = pl.pallas_call(
        _gdn_kernel,
        out_shape=[jax.ShapeDtypeStruct((b, l, GDN_DIM), BF16),
                   jax.ShapeDtypeStruct((b, CONV_TAIL, GDN_CONV_DIM), F32),
                   jax.ShapeDtypeStruct(s0.shape, F32)],
        grid=(b, l // q),
        in_specs=[tok(GDN_CONV_DIM), tok(GDN_DIM), tok(LANES), per_b(tail0), per_b(s0)] + [const(a) for a in consts],
        out_specs=[tok(GDN_DIM), per_b(tail0), per_b(s0)],
        scratch_shapes=[pltpu.VMEM((CONV_TAIL, GDN_CONV_DIM), F32), pltpu.VMEM(s0.shape[1:], F32)],
        compiler_params=_params("parallel", "arbitrary"),
        name="gdn_mixer",
    )(qkv, z, small, tail0, s0, *consts)
    return o, tail[:, CONV_TAIL - (CONV_W - 1):], s_new


def _outproj_kernel(a_ref, b_ref, w_ref, x_ref, gate_ref, g_ref, sh_ref, sc_ref, rwt_ref,
                    xo_ref, h_ref, logit_ref, *, split_inputs):
    kb, tl, d = x_ref.shape
    rows = kb * tl
    a = a_ref[...].reshape(rows, a_ref.shape[2])
    b = b_ref[...].reshape(rows, b_ref.shape[2])
    lhs = jnp.concatenate([_cat3_lhs(a), _cat3_lhs(b)] if split_inputs else [a.astype(BF16), b.astype(BF16)], axis=-1)
    mix = jnp.dot(lhs, w_ref[...], preferred_element_type=F32)
    x_new = x_ref[...] + gate_ref[...] * mix.reshape(kb, tl, d)
    xo_ref[...] = x_new
    h = _ada_norm(x_new, g_ref[...], sh_ref[...], sc_ref[...])
    h_ref[...] = h.astype(h_ref.dtype)
    logit_ref[...] = lax.dot_general(rwt_ref[...], _cat3_lhs(h.reshape(rows, d)), (((1,), (1,)), ((), ())),
                                     preferred_element_type=F32)


def _outproj_residual(a, b_half, w_out, x, gate, g2, shift2, scale2, router_w, split_inputs):
    b, l, d = x.shape
    kb, tl = _row_blocking(b, l)
    half = a.shape[2]
    pack = _cat3_rhs if split_inputs else (lambda w: w.astype(BF16))
    w = jnp.concatenate([pack(w_out[:half]), pack(w_out[half:])], axis=0)
    rwt = _cat3_rhs(router_w.T, axis=1)
    n_l = l // tl
    tok = lambda c: pl.BlockSpec((kb, tl, c), lambda i, j: (i, j, 0))
    per_batch = pl.BlockSpec((kb, 1, d), lambda i, j: (i, 0, 0))
    const = _const_spec
    return pl.pallas_call(
        functools.partial(_outproj_kernel, split_inputs=split_inputs),
        out_shape=[jax.ShapeDtypeStruct((b, l, d), F32), jax.ShapeDtypeStruct((b, l, d), BF16),
                   jax.ShapeDtypeStruct((N_EXPERTS, b * l), F32)],
        grid=(b // kb, n_l),
        in_specs=[tok(half), tok(b_half.shape[2]), const(w), tok(d), per_batch, const(g2), per_batch, per_batch, const(rwt)],
        out_specs=[tok(d), tok(d), pl.BlockSpec((N_EXPERTS, kb * tl), lambda i, j: (0, i * n_l + j))],
        compiler_params=_params("parallel", "parallel"),
        name="outproj_residual",
    )(a, b_half, w, x, gate, g2, shift2, scale2, rwt)


def _beats(a, b, a_first):
    return jnp.where(a > b, 1, jnp.where(a == b, a_first, 0))


def _router_kernel(logit_ref, bias_ref, gate_ref, pos_ref, count_ref):
    tm = logit_ref.shape[1]
    scores = _sigmoid(logit_ref[...])
    sel = scores + bias_ref[...]
    ng, eg = N_EXPERT_GROUPS, EXPERTS_PER_GROUP
    sub = lax.broadcasted_iota(I32, (eg, tm), 0)
    groups, grp = [], []
    for g in range(ng):
        xg = sel[g * eg:(g + 1) * eg]
        m1 = jnp.max(xg, axis=0, keepdims=True)
        first = jnp.min(jnp.where(xg == m1, sub, eg), axis=0, keepdims=True)
        m2 = jnp.max(jnp.where(sub == first, -jnp.inf, xg), axis=0, keepdims=True)
        groups.append(xg)
        grp.append(m1 + m2)
    masked = []
    for g in range(ng):
        rank = sum(_beats(grp[o], grp[g], 1 if o < g else 0) for o in range(ng) if o != g)
        masked.append(jnp.where(rank < TOPK_GROUPS, groups[g], -jnp.inf))
    selm = jnp.concatenate(masked, axis=0)
    erow = lax.broadcasted_iota(I32, (N_EXPERTS, tm), 0)
    rank = jnp.zeros((N_EXPERTS, tm), I32)
    for o in range(N_EXPERTS):
        rank = rank + _beats(selm[o:o + 1], selm, jnp.where(erow > o, 1, 0))
    gate = jnp.concatenate(
        [jnp.sum(jnp.where(rank == k, scores, 0.0), axis=0, keepdims=True) for k in range(TOP_K)], axis=0)
    gate_ref[...] = gate / jnp.sum(gate, axis=0, keepdims=True) * ROUTED_SCALE
    chosen = jnp.where(rank < TOP_K, 1.0, 0.0)
    earlier = jnp.where(lax.broadcasted_iota(I32, (tm, tm), 0) < lax.broadcasted_iota(I32, (tm, tm), 1), 1.0, 0.0)
    before = _mm(chosen, earlier)
    count = jnp.sum(chosen, axis=1, keepdims=True)
    padded = jnp.ceil(count * (1.0 / SEG_ALIGN)) * SEG_ALIGN
    lower = jnp.where(lax.broadcasted_iota(I32, (N_EXPERTS, N_EXPERTS), 1) < lax.broadcasted_iota(I32, (N_EXPERTS, N_EXPERTS), 0), 1.0, 0.0)
    seg_start = _mm(lower, jnp.broadcast_to(padded, (N_EXPERTS, LANES)))[:, :1]
    slot = seg_start + before
    pos_ref[...] = jnp.concatenate(
        [jnp.sum(jnp.where(rank == k, slot, 0.0), axis=0, keepdims=True) for k in range(TOP_K)], axis=0).astype(I32)
    count_ref[0] = jnp.broadcast_to(count, (N_EXPERTS, LANES)).astype(I32)


def _router(logits, router_b):
    t = logits.shape[1]
    n_tiles = t // ROW_TILE
    bias = router_b.astype(F32)[:, None]
    tile = lambda rows: pl.BlockSpec((rows, ROW_TILE), lambda i: (0, i))
    return pl.pallas_call(
        _router_kernel,
        out_shape=[jax.ShapeDtypeStruct((TOP_K, t), F32), jax.ShapeDtypeStruct((TOP_K, t), I32),
                   jax.ShapeDtypeStruct((n_tiles, N_EXPERTS, LANES), I32)],
        grid=(n_tiles,),
        in_specs=[tile(N_EXPERTS), _const_spec(bias)],
        out_specs=[tile(TOP_K), tile(TOP_K), pl.BlockSpec((1, N_EXPERTS, LANES), lambda i: (i, 0, 0))],
        compiler_params=_params("parallel"),
        name="moe_router",
    )(logits, bias)


SEG_ALIGN = 16
SEG_CHUNK = 32
EXPERT_BLOCK = 512
ONEHOT_CHUNK = 256
XS_COLS = D_MODEL + LANES
LOCAL_ROWS = -(-(TOP_K * ROW_TILE + N_EXPERTS * (SEG_ALIGN - 1)) // ONEHOT_CHUNK) * ONEHOT_CHUNK


def _segment_dmas(n_rows, src_ref, src_row, dst_ref, dst_row, sem, wait):
    def dma(rows, s, d):
        cp = pltpu.make_async_copy(src_ref.at[pl.ds(pl.multiple_of(s, SEG_ALIGN), rows)],
                                   dst_ref.at[pl.ds(pl.multiple_of(d, SEG_ALIGN), rows)], sem)
        cp.wait() if wait else cp.start()

    n_full = lax.shift_right_logical(n_rows, SEG_CHUNK.bit_length() - 1)

    def bulk(q, carry):
        dma(SEG_CHUNK, src_row + q * SEG_CHUNK, dst_row + q * SEG_CHUNK)
        return carry

    lax.fori_loop(0, n_full, bulk, 0)
    off = n_full * SEG_CHUNK
    size = SEG_CHUNK // 2
    while size >= SEG_ALIGN:
        has = (n_rows & size) != 0

        @pl.when(has)
        def _(size=size, off=off):
            dma(size, src_row + off, dst_row + off)

        off = off + jnp.where(has, size, 0)
        size //= 2


def _tile_segments(tile, cnt_ref, lstart_ref, gstart_ref, local_ref, global_ref, sem, to_global, wait):
    base = tile * N_EXPERTS

    def seg(e, carry):
        n, lo, go = cnt_ref[base + e], lstart_ref[base + e], gstart_ref[base + e]
        if to_global:
            _segment_dmas(n, local_ref, lo, global_ref, go, sem, wait)
        else:
            _segment_dmas(n, global_ref, go, local_ref, lo, sem, wait)
        return carry

    lax.fori_loop(0, N_EXPERTS, seg, 0)


def _zero_fill(tail_rows_ref, tail_start_ref, used_ref, zero_ref, xs_ref, sem, wait):
    def tail(e, carry):
        _segment_dmas(tail_rows_ref[e], zero_ref, 0, xs_ref, tail_start_ref[e], sem, wait)
        return carry

    lax.fori_loop(0, N_EXPERTS, tail, 0)

    def block(j, carry):
        cp = pltpu.make_async_copy(zero_ref, xs_ref.at[pl.ds(pl.multiple_of(j * EXPERT_BLOCK, EXPERT_BLOCK), EXPERT_BLOCK)], sem)
        cp.wait() if wait else cp.start()
        return carry

    lax.fori_loop(used_ref[0], xs_ref.shape[0] // EXPERT_BLOCK, block, 0)


def _dispatch_kernel(cnt_ref, lstart_ref, gstart_ref, tail_rows_ref, tail_start_ref, used_ref,
                     h_ref, pos_ref, gate_ref, xs_ref, loc_ref, zero_ref, sem):
    tile = pl.program_id(0) * pl.num_programs(1) + pl.program_id(1)
    last = pl.num_programs(0) * pl.num_programs(1) - 1
    kb, tl, d = h_ref.shape
    tm = kb * tl
    h = h_ref[...].reshape(tm, d)
    pos, gate = pos_ref[...], gate_ref[...]
    row = lax.broadcasted_iota(I32, (ONEHOT_CHUNK, tm), 0)
    lane = lax.broadcasted_iota(I32, (ONEHOT_CHUNK, LANES), 1)
    used_rows = lstart_ref[(tile + 1) * N_EXPERTS - 1] + cnt_ref[(tile + 1) * N_EXPERTS - 1]
    for c in range(LOCAL_ROWS // ONEHOT_CHUNK):
        @pl.when(c * ONEHOT_CHUNK < used_rows)
        def _(c=c):
            onehot = jnp.zeros((ONEHOT_CHUNK, tm), F32)
            gsel = jnp.zeros((ONEHOT_CHUNK, tm), F32)
            for k in range(TOP_K):
                hit = (pos[k:k + 1, :] - c * ONEHOT_CHUNK) == row
                onehot = jnp.where(hit, 1.0, onehot)
                gsel = jnp.where(hit, gate[k:k + 1, :], gsel)
            rows = jnp.dot(onehot.astype(BF16), h, preferred_element_type=F32)
            g1, g2, g3 = (t.astype(F32) for t in _split3_bf16(jnp.sum(gsel, axis=1, keepdims=True)))
            sl = slice(c * ONEHOT_CHUNK, (c + 1) * ONEHOT_CHUNK)
            loc_ref[sl, :d] = rows.astype(BF16)
            loc_ref[sl, d:] = jnp.where(lane == 0, g1, jnp.where(lane == 1, g2, jnp.where(lane == 2, g3, 0.0))).astype(BF16)

    @pl.when(tile == last)
    def _():
        zero_ref[...] = jnp.zeros(zero_ref.shape, zero_ref.dtype)
        _zero_fill(tail_rows_ref, tail_start_ref, used_ref, zero_ref, xs_ref, sem, wait=False)

    _tile_segments(tile, cnt_ref, lstart_ref, gstart_ref, loc_ref, xs_ref, sem, to_global=True, wait=False)
    _tile_segments(tile, cnt_ref, lstart_ref, gstart_ref, loc_ref, xs_ref, sem, to_global=True, wait=True)

    @pl.when(tile == last)
    def _():
        _zero_fill(tail_rows_ref, tail_start_ref, used_ref, zero_ref, xs_ref, sem, wait=True)


def _moe_dispatch(h, pos, gate, plan):
    b, l, d = h.shape
    kb, tl = _row_blocking(b, l)
    n_l = l // tl
    tok = pl.BlockSpec((kb, tl, d), lambda i, j, *_: (i, j, 0))
    per_choice = pl.BlockSpec((TOP_K, kb * tl), lambda i, j, *_: (0, i * n_l + j))
    return pl.pallas_call(
        _dispatch_kernel,
        out_shape=jax.ShapeDtypeStruct((plan['cap'], XS_COLS), BF16),
        grid_spec=pltpu.PrefetchScalarGridSpec(
            num_scalar_prefetch=6,
            grid=(b // kb, n_l),
            in_specs=[tok, per_choice, per_choice],
            out_specs=pl.BlockSpec(memory_space=pl.ANY),
            scratch_shapes=[pltpu.VMEM((LOCAL_ROWS, XS_COLS), BF16), pltpu.VMEM((EXPERT_BLOCK, XS_COLS), BF16),
                            pltpu.SemaphoreType.DMA]),
        compiler_params=_params("arbitrary", "arbitrary"),
        name="moe_dispatch",
    )(plan['seg_rows'], plan['local_start'], plan['global_start'], plan['tail_rows'], plan['tail_start'],
      plan['blocks_used'], h, pos, gate)


def _swiglu(x, w1, w3, w2):
    hid = _silu(jnp.dot(x, w1, preferred_element_type=F32)) * jnp.dot(x, w3, preferred_element_type=F32)
    return jnp.dot(hid.astype(BF16), w2, preferred_element_type=F32)


def _expert_block_kernel(be_ref, nused_ref, x_ref, w1_ref, w3_ref, w2_ref, y_ref, w1_sc, w3_sc, w2_sc):
    j = pl.program_id(0)

    @pl.when(jnp.logical_or(j == 0, be_ref[j] != be_ref[jnp.maximum(j - 1, 0)]))
    def _():
        w1_sc[...] = w1_ref[0].astype(BF16)
        w3_sc[...] = w3_ref[0].astype(BF16)
        w2_sc[...] = w2_ref[0].astype(BF16)

    @pl.when(j < nused_ref[0])
    def _():
        g = x_ref[:, D_MODEL:].astype(F32)
        gate = (g[:, 0:1] + g[:, 1:2]) + g[:, 2:3]
        out = _swiglu(x_ref[:, :D_MODEL], w1_sc[...], w3_sc[...], w2_sc[...])
        y_ref[...] = (out * gate).astype(y_ref.dtype)

    @pl.when(j >= nused_ref[0])
    def _():
        y_ref[...] = jnp.zeros(y_ref.shape, y_ref.dtype)


def _expert_blocks(xs, plan, w1, w3, w2):
    cap = xs.shape[0]
    d = D_MODEL
    weight = lambda shape: pl.BlockSpec((1,) + shape, lambda j, be, nu: (be[j], 0, 0))
    return pl.pallas_call(
        _expert_block_kernel,
        out_shape=jax.ShapeDtypeStruct((cap, d), BF16),
        grid_spec=pltpu.PrefetchScalarGridSpec(
            num_scalar_prefetch=2,
            grid=(cap // EXPERT_BLOCK,),
            in_specs=[pl.BlockSpec((EXPERT_BLOCK, XS_COLS), lambda j, *_: (j, 0)),
                      weight((d, D_EXPERT)), weight((d, D_EXPERT)), weight((D_EXPERT, d))],
            out_specs=pl.BlockSpec((EXPERT_BLOCK, d), lambda j, *_: (j, 0)),
            scratch_shapes=[pltpu.VMEM((d, D_EXPERT), BF16), pltpu.VMEM((d, D_EXPERT), BF16),
                            pltpu.VMEM((D_EXPERT, d), BF16)]),
        compiler_params=_params("arbitrary"),
        name="moe_expert_blocks",
    )(plan['block_expert'], plan['blocks_used'], xs, w1, w3, w2)


def _combine_kernel(cnt_ref, lstart_ref, gstart_ref, ys_ref, pos_ref, h_ref, x_ref, gmod_ref, w1_ref, w3_ref, w2_ref,
                    o_ref, loc_ref, acc_ref, sem):
    tile = pl.program_id(0) * pl.num_programs(1) + pl.program_id(1)
    kb, tl, d = x_ref.shape
    tm = kb * tl

    @pl.when(tile == 0)
    def _():
        loc_ref[...] = jnp.zeros(loc_ref.shape, loc_ref.dtype)

    _tile_segments(tile, cnt_ref, lstart_ref, gstart_ref, loc_ref, ys_ref, sem, to_global=False, wait=False)
    shared = _swiglu(h_ref[...].reshape(tm, d), w1_ref[...], w3_ref[...], w2_ref[...])
    _tile_segments(tile, cnt_ref, lstart_ref, gstart_ref, loc_ref, ys_ref, sem, to_global=False, wait=True)
    pos = pos_ref[...]
    col = lax.broadcasted_iota(I32, (tm, ONEHOT_CHUNK), 1)
    acc_ref[...] = shared
    used_rows = lstart_ref[(tile + 1) * N_EXPERTS - 1] + cnt_ref[(tile + 1) * N_EXPERTS - 1]
    for c in range(LOCAL_ROWS // ONEHOT_CHUNK):
        @pl.when(c * ONEHOT_CHUNK < used_rows)
        def _(c=c):
            onehot = jnp.zeros((tm, ONEHOT_CHUNK), F32)
            for k in range(TOP_K):
                onehot = jnp.where((pos[:, k:k + 1] - c * ONEHOT_CHUNK) == col, 1.0, onehot)
            rows = loc_ref[c * ONEHOT_CHUNK:(c + 1) * ONEHOT_CHUNK, :]
            acc_ref[...] += jnp.dot(onehot.astype(BF16), rows, preferred_element_type=F32)
    o_ref[...] = x_ref[...] + gmod_ref[...] * acc_ref[...].reshape(kb, tl, d)


def _moe_combine(ys, pos_t, h, x, gate_mod, plan, sw1, sw3, sw2):
    b, l, d = x.shape
    kb, tl = _row_blocking(b, l)
    n_l = l // tl
    tok = pl.BlockSpec((kb, tl, d), lambda i, j, *_: (i, j, 0))
    per_batch = pl.BlockSpec((kb, 1, d), lambda i, j, *_: (i, 0, 0))
    return pl.pallas_call(
        _combine_kernel,
        out_shape=jax.ShapeDtypeStruct((b, l, d), F32),
        grid_spec=pltpu.PrefetchScalarGridSpec(
            num_scalar_prefetch=3,
            grid=(b // kb, n_l),
            in_specs=[pl.BlockSpec(memory_space=pl.ANY),
                      pl.BlockSpec((kb * tl, TOP_K), lambda i, j, *_: (i * n_l + j, 0)),
                      tok, tok, per_batch, _const_spec(sw1), _const_spec(sw3), _const_spec(sw2)],
            out_specs=tok,
            scratch_shapes=[pltpu.VMEM((LOCAL_ROWS, D_MODEL), BF16), pltpu.VMEM((kb * tl, d), F32),
                            pltpu.SemaphoreType.DMA]),
        compiler_params=_params("arbitrary", "arbitrary"),
        name="moe_combine",
    )(plan['seg_rows'], plan['local_start'], plan['global_start'], ys, pos_t, h, x, gate_mod, sw1, sw3, sw2)


def _dispatch_plan(counts, n_tok):
    n_tiles = counts.shape[0]
    seg = (counts + SEG_ALIGN - 1) // SEG_ALIGN * SEG_ALIGN
    local_start = jnp.cumsum(seg, axis=1) - seg
    region = jnp.sum(seg, axis=0)
    region_pad = (region + EXPERT_BLOCK - 1) // EXPERT_BLOCK * EXPERT_BLOCK
    region_end = jnp.cumsum(region_pad)
    region_beg = region_end - region_pad
    global_start = region_beg[None, :] + jnp.cumsum(seg, axis=0) - seg
    worst = TOP_K * n_tok + N_EXPERTS * (SEG_ALIGN - 1) * n_tiles + N_EXPERTS * (EXPERT_BLOCK - 1)
    cap = -(-worst // EXPERT_BLOCK) * EXPERT_BLOCK
    block_row = jnp.arange(cap // EXPERT_BLOCK, dtype=I32) * EXPERT_BLOCK
    block_expert = jnp.minimum(jnp.sum((region_end[None, :] <= block_row[:, None]).astype(I32), axis=1), N_EXPERTS - 1)
    flat = lambda a: a.reshape(-1).astype(I32)
    return dict(cap=cap, seg_rows=flat(seg), local_start=flat(local_start), global_start=flat(global_start),
                tail_rows=flat(region_pad - region), tail_start=flat(region_beg + region),
                block_expert=flat(block_expert), blocks_used=flat(region_end[-1:] // EXPERT_BLOCK))


def _moe(h, logits, x, gate_mod, router_b, w1, w3, w2, sw1, sw3, sw2):
    b, l, d = x.shape
    gate, pos, counts = _router(logits, router_b)
    plan = _dispatch_plan(counts[:, :, 0], b * l)
    xs = _moe_dispatch(h, pos, gate, plan)
    ys = _expert_blocks(xs, plan, w1, w3, w2)
    return _moe_combine(ys, pos.T, h, x, gate_mod, plan, sw1, sw3, sw2)


def _split_mod(m):
    m = m[:, None, :]
    return m[..., :D_MODEL], m[..., D_MODEL:2 * D_MODEL], m[..., 2 * D_MODEL:]


def _even_layer(x, mod_a, mod_b, past, w):
    b, l, _ = x.shape
    shift, scale, gate = _split_mod(mod_a)
    q3, k3, k, v, z, xbc, small = _inproj_call(
        _inproj_even_kernel, "inproj_even", x, w['norm_g0'], shift, scale, w['w_in'],
        [w['head_mean'], w['q_gain'], w['k_gain'], w['small_bias']],
        [FOX_HEADS * FOX_HD3, FOX_HEADS * FOX_HD3, FOX_DIM, FOX_DIM, SSD_INNER, SSD_CONV_DIM, LANES],
        [BF16, BF16, F32, F32, F32, F32, F32])
    if past['fox_k'] is None:
        cum, cumt = _fox_cumulative(small, None)
        fox = _fox_attention_prompt(q3, k3, v, cum, cumt)
    else:
        n_past = past['fox_k'].shape[1]
        past_logf = jnp.pad(past['fox_logf'].astype(F32), ((0, 0), (0, 0), (0, LANES - FOX_HEADS)))
        cum, cumt = _fox_cumulative(small, past_logf)
        fox = _fox_attention_decode(q3, k3, v, past['fox_k'].reshape(b, n_past, FOX_DIM),
                                    past['fox_v'].reshape(b, n_past, FOX_DIM), cum, cumt)
    y, conv_new, ssd_new = _ssd_mixer(xbc, z, small, past['ssd_conv'], past['ssd'], w['ssd_conv_w'], w['ssd_conv_b'],
                                      w['ssd_A_log'], w['ssd_D'], w['ssd_norm_g'])
    shift2, scale2, _ = _split_mod(mod_b)
    x, h, logits = _outproj_residual(fox, y, w['w_out'], x, gate, w['norm_g1'], shift2, scale2, w['router_w'],
                                     split_inputs=True)
    new = dict(fox_k=k.reshape(b, l, FOX_HEADS, FOX_HD), fox_v=v.reshape(b, l, FOX_HEADS, FOX_HD),
               fox_logf=small[..., :FOX_HEADS], ssd_conv=conv_new, ssd=ssd_new)
    return x, h, logits, new


def _odd_layer(x, mod_a, mod_b, past, w):
    shift, scale, gate = _split_mod(mod_a)
    rw, qkv, gz, small = _inproj_call(
        _inproj_odd_kernel, "inproj_odd", x, w['norm_g0'], shift, scale, w['w_in'],
        [w['small_bias'], w['small_alog']], [RWKV_PROJ, GDN_CONV_DIM, GDN_DIM, LANES], [F32, F32, F32, F32])
    o_rwkv, rwkv_new = _rwkv_mixer(rw, past['rwkv_shift'], past['rwkv'], w['p'])
    o_gdn, gconv_new, gdn_new = _gdn_mixer(qkv, gz, small, past['gdn_conv'], past['gdn'], w['p']['gdn_conv_w'],
                                           w['p']['gdn_norm_g'])
    shift2, scale2, _ = _split_mod(mod_b)
    x, h, logits = _outproj_residual(o_rwkv, o_gdn, w['w_out'], x, gate, w['norm_g1'], shift2, scale2, w['router_w'],
                                     split_inputs=False)
    new = dict(rwkv_shift=rw[:, -1], rwkv=rwkv_new, gdn_conv=gconv_new, gdn=gdn_new)
    return x, h, logits, new


def _run_trunk(x, mods, past_even, past_odd, layers):
    new_even, new_odd = [], []
    for i, w in enumerate(layers):
        j = i // 2
        if i % 2 == 0:
            past = {n: (None if a is None else a[j]) for n, a in past_even.items()}
            x, h, logits, st = _even_layer(x, mods[2 * i], mods[2 * i + 1], past, w)
            new_even.append(st)
        else:
            x, h, logits, st = _odd_layer(x, mods[2 * i], mods[2 * i + 1], {n: a[j] for n, a in past_odd.items()}, w)
            new_odd.append(st)
        x = _moe(h, logits, x, _split_mod(mods[2 * i + 1])[2], *w['moe'])
    stack = lambda lst: {n: jnp.stack([d[n] for d in lst]) for n in lst[0]}
    return x, stack(new_even), stack(new_odd)


def kernel(x_prompt, x_sample, cache_fox_k, cache_fox_v, cache_fox_logf, state_ssd_conv, state_ssd, state_rwkv_shift, state_rwkv, state_gdn_conv, state_gdn, c_prompt, c_sample, ada_w, ada_b, norm_g, even_w_in, even_w_out, fox_q_g, fox_k_g, fox_f_b, ssd_conv_w, ssd_conv_b, ssd_dt_bias, ssd_A_log, ssd_D, ssd_norm_g, odd_w_in, odd_w_out, rwkv_mu, rwkv_w0, rwkv_w2, rwkv_a0, rwkv_a2, rwkv_g2, rwkv_k_k, rwkv_k_a, rwkv_r_k, rwkv_ln_g, rwkv_ln_b, gdn_conv_w, gdn_A_log, gdn_dt_bias, gdn_norm_g, router_w, router_b, moe_w1, moe_w3, moe_w2, shared_w1, shared_w3, shared_w2):
    depth = ada_w.shape[0]
    ne, no = cache_fox_k.shape[0], state_rwkv.shape[0]
    bp, bs = x_prompt.shape[0], x_sample.shape[0]
    odd_params = dict(rwkv_mu=rwkv_mu, rwkv_w0=rwkv_w0, rwkv_w2=rwkv_w2, rwkv_a0=rwkv_a0, rwkv_a2=rwkv_a2, rwkv_g2=rwkv_g2,
                      rwkv_k_k=rwkv_k_k, rwkv_k_a=rwkv_k_a, rwkv_r_k=rwkv_r_k, rwkv_ln_g=rwkv_ln_g, rwkv_ln_b=rwkv_ln_b,
                      gdn_conv_w=gdn_conv_w, gdn_norm_g=gdn_norm_g)
    layers = []
    for i in range(depth):
        j = i // 2
        w = dict(norm_g0=norm_g[i, 0][None], norm_g1=norm_g[i, 1][None], router_w=router_w[i],
                 moe=(router_b[i], moe_w1[i], moe_w3[i], moe_w2[i],
                      shared_w1[i].astype(BF16), shared_w3[i].astype(BF16), shared_w2[i].astype(BF16)))
        if i % 2 == 0:
            w.update(w_in=_pack_even_w_in(even_w_in[j]), w_out=even_w_out[j],
                     head_mean=_head_mean_matrix(FOX_DIM, FOX_HD),
                     q_gain=jnp.tile(fox_q_g[j], FOX_HEADS)[None], k_gain=jnp.tile(fox_k_g[j], FOX_HEADS)[None],
                     small_bias=_pad_lanes(fox_f_b[j], ssd_dt_bias[j]),
                     ssd_conv_w=ssd_conv_w[j], ssd_conv_b=ssd_conv_b[j], ssd_A_log=ssd_A_log[j], ssd_D=ssd_D[j],
                     ssd_norm_g=ssd_norm_g[j])
        else:
            w.update(w_in=_pack_odd_w_in(odd_w_in[j]), w_out=odd_w_out[j],
                     small_bias=_pad_lanes(jnp.zeros(GDN_HEADS), gdn_dt_bias[j]),
                     small_alog=_pad_lanes(jnp.zeros(GDN_HEADS), gdn_A_log[j]),
                     p={n: a[j] for n, a in odd_params.items()})
        layers.append(w)

    mods = _ada_modulation(jnp.concatenate([c_prompt, c_sample], axis=0), ada_w, ada_b)
    mods_p, mods_s = mods[:, :bp], mods[:, bp:]

    prompt_even = dict(fox_k=None, fox_v=None, fox_logf=None,
                       ssd_conv=jnp.zeros((ne, bp) + state_ssd_conv.shape[2:], F32),
                       ssd=jnp.zeros((ne, bp) + state_ssd.shape[2:], F32))
    prompt_odd = dict(rwkv_shift=jnp.zeros((no, bp) + state_rwkv_shift.shape[2:], F32),
                      rwkv=jnp.zeros((no, bp) + state_rwkv.shape[2:], F32),
                      gdn_conv=jnp.zeros((no, bp) + state_gdn_conv.shape[2:], F32),
                      gdn=jnp.zeros((no, bp) + state_gdn.shape[2:], F32))
    sample_even = dict(fox_k=cache_fox_k, fox_v=cache_fox_v, fox_logf=cache_fox_logf, ssd_conv=state_ssd_conv, ssd=state_ssd)
    sample_odd = dict(rwkv_shift=state_rwkv_shift, rwkv=state_rwkv, gdn_conv=state_gdn_conv, gdn=state_gdn)
    y_prompt, pe, po = _run_trunk(x_prompt, mods_p, prompt_even, prompt_odd, layers)
    y_sample, se, so = _run_trunk(x_sample, mods_s, sample_even, sample_odd, layers)
    return (y_prompt, y_sample,
            pe['fox_k'], pe['fox_v'], pe['fox_logf'], pe['ssd_conv'], pe['ssd'],
            po['rwkv_shift'], po['rwkv'], po['gdn_conv'], po['gdn'],
            se['fox_k'], se['fox_v'], se['fox_logf'], se['ssd_conv'], se['ssd'],
            so['rwkv_shift'], so['rwkv'], so['gdn_conv'], so['gdn'])
```

```python
import functools
import math

import jax
import jax.numpy as jnp
from jax import lax
from jax.experimental import pallas as pl
from jax.experimental.pallas import tpu as pltpu

F32 = jnp.float32
BF16 = jnp.bfloat16
I32 = jnp.int32

D_MODEL = 1024
CONV_W = 4
EPS = 1e-6

FOX_HEADS = 8
FOX_HD = 64
FOX_DIM = FOX_HEADS * FOX_HD
FOX_SCALE = 1.0 / math.sqrt(FOX_HD)

SSD_HEADS = 8
SSD_HD = 64
SSD_INNER = SSD_HEADS * SSD_HD
SSD_GROUPS = 2
SSD_N = 128
SSD_CONV_DIM = SSD_INNER + 2 * SSD_GROUPS * SSD_N

RWKV_HEADS = 8
RWKV_HD = 64
RWKV_DIM = RWKV_HEADS * RWKV_HD
RWKV_DECAY_LORA = 64
RWKV_A_LORA = 64
RWKV_GATE_LORA = 128
RWKV_PROJ = 3 * RWKV_DIM + RWKV_DECAY_LORA + RWKV_A_LORA + RWKV_GATE_LORA
RWKV_GN_EPS = 64e-5

GDN_HEADS = 4
GDN_HD = 128
GDN_DIM = GDN_HEADS * GDN_HD
GDN_CONV_DIM = 3 * GDN_DIM

N_EXPERTS = 64
TOP_K = 8
N_EXPERT_GROUPS = 8
TOPK_GROUPS = 4
EXPERTS_PER_GROUP = N_EXPERTS // N_EXPERT_GROUPS
D_EXPERT = 256
D_SHARED = 256
ROUTED_SCALE = 2.5

LANES = 128
VMEM_LIMIT_BYTES = 56 * 1024 * 1024


def _params(*semantics):
    return pltpu.CompilerParams(dimension_semantics=semantics, vmem_limit_bytes=VMEM_LIMIT_BYTES)


def _const_spec(a):
    return pl.BlockSpec(a.shape, lambda *_: (0,) * a.ndim, pipeline_mode=pl.Buffered(1))


def _mm(a, b):
    return jnp.dot(a.astype(BF16), b.astype(BF16), preferred_element_type=F32)


def _mm_nt(a, b):
    return lax.dot_general(a.astype(BF16), b.astype(BF16), (((1,), (1,)), ((), ())), preferred_element_type=F32)


def _mm_tn(a, b):
    return lax.dot_general(a.astype(BF16), b.astype(BF16), (((0,), (0,)), ((), ())), preferred_element_type=F32)


def _split_hi_lo(x):
    hi = x.astype(BF16)
    return hi, (x - hi.astype(F32)).astype(BF16)


def _dot3(dims, a, b, batch=((), ())):
    a_hi, a_lo = _split_hi_lo(a)
    b_hi, b_lo = _split_hi_lo(b)
    dot = lambda u, w: lax.dot_general(u, w, (dims, batch), preferred_element_type=F32)
    return dot(a_hi, b_hi) + (dot(a_hi, b_lo) + dot(a_lo, b_hi))


_mm3 = functools.partial(_dot3, ((1,), (0,)))
_mm3_nt = functools.partial(_dot3, ((1,), (1,)))
_mm3_tn = functools.partial(_dot3, ((0,), (0,)))

_HEAD_BATCH = ((0,), (0,))


def _bdot(dims, a, b):
    return lax.dot_general(a.astype(BF16), b.astype(BF16), (dims, _HEAD_BATCH), preferred_element_type=F32)


_bmm = functools.partial(_bdot, ((2,), (1,)))
_bmm_nt = functools.partial(_bdot, ((2,), (2,)))
_bmm_tn = functools.partial(_bdot, ((1,), (1,)))
_bmm3 = functools.partial(_dot3, ((2,), (1,)), batch=_HEAD_BATCH)


def _heads(x, hd):
    return jnp.stack([x[:, h * hd:(h + 1) * hd] for h in range(x.shape[1] // hd)], axis=0)


def _unheads(x):
    return jnp.concatenate([x[h] for h in range(x.shape[0])], axis=-1)


def _cat3_lhs(x):
    hi, lo = _split_hi_lo(x)
    return jnp.concatenate([hi, hi, lo], axis=-1)


def _split_hi_lo_outside(w):
    w = w.astype(F32)
    hi = lax.bitcast_convert_type(lax.bitcast_convert_type(w, jnp.uint32) & jnp.uint32(0xFFFF0000), F32)
    return hi.astype(BF16), (w - hi).astype(BF16)


def _cat3_rhs(w, axis=0):
    hi, lo = _split_hi_lo_outside(w)
    return jnp.concatenate([hi, lo, hi], axis=axis)


def _sigmoid(x):
    return 1.0 / (1.0 + jnp.exp(-x))


def _silu(x):
    return x * _sigmoid(x)


def _softplus(x):
    return jnp.maximum(x, 0.0) + jnp.log1p(jnp.exp(-jnp.abs(x)))


def _cumsum_rows(x):
    n = x.shape[0]
    row = lax.broadcasted_iota(I32, x.shape, 0)
    s = 1
    while s < n:
        x = x + jnp.where(row >= s, pltpu.roll(x, s, 0), 0.0)
        s *= 2
    return x


def _ada_norm(x, g, shift, scale):
    y = x * lax.rsqrt(jnp.mean(x * x, axis=-1, keepdims=True) + EPS)
    return (y * g) * (1.0 + scale) + shift


def _ada_kernel(c_ref, w_ref, b_ref, o_ref):
    o_ref[0] = _mm3(_silu(c_ref[...]), w_ref[0]) + b_ref[0]


def _ada_modulation(c_all, ada_w, ada_b):
    n = c_all.shape[0]
    n_mod = ada_w.shape[0] * ada_w.shape[1]
    w = ada_w.reshape(n_mod, D_MODEL, 3 * D_MODEL)
    b = ada_b.reshape(n_mod, 1, 3 * D_MODEL)
    return pl.pallas_call(
        _ada_kernel,
        out_shape=jax.ShapeDtypeStruct((n_mod, n, 3 * D_MODEL), F32),
        grid=(n_mod, 3),
        in_specs=[
            pl.BlockSpec((n, D_MODEL), lambda i, j: (0, 0)),
            pl.BlockSpec((1, D_MODEL, D_MODEL), lambda i, j: (i, 0, j)),
            pl.BlockSpec((1, 1, D_MODEL), lambda i, j: (i, 0, j)),
        ],
        out_specs=pl.BlockSpec((1, n, D_MODEL), lambda i, j: (i, 0, j)),
        compiler_params=_params("parallel", "parallel"),
        name="ada_modulation",
    )(c_all, w, b)


ROW_TILE = 256

EV_Q, EV_K, EV_V, EV_Z, EV_XBC, EV_SMALL = 0, FOX_DIM, 2 * FOX_DIM, 3 * FOX_DIM, 3 * FOX_DIM + SSD_INNER, 3 * FOX_DIM + SSD_INNER + SSD_CONV_DIM
EV_COLS = EV_SMALL + LANES
OD_RW, OD_QKV, OD_Z, OD_SMALL = 0, RWKV_PROJ, RWKV_PROJ + GDN_CONV_DIM, RWKV_PROJ + GDN_CONV_DIM + GDN_DIM
OD_COLS = OD_SMALL + LANES


def _row_blocking(b, l):
    if l >= ROW_TILE:
        assert l % ROW_TILE == 0
        return 1, ROW_TILE
    assert ROW_TILE % l == 0 and b % (ROW_TILE // l) == 0 and l % 8 == 0
    return ROW_TILE // l, l


def _normed_rows(x_ref, g_ref, sh_ref, sc_ref):
    x = x_ref[...]
    h = _ada_norm(x, g_ref[...], sh_ref[...], sc_ref[...])
    return h.reshape(x.shape[0] * x.shape[1], x.shape[2])


def _head_rms(y, head_mean, gain):
    hi, lo = _split_hi_lo(y * y)
    ms = jnp.dot(hi, head_mean, preferred_element_type=F32) + jnp.dot(lo, head_mean, preferred_element_type=F32)
    return y * lax.rsqrt(ms + EPS) * gain


FOX_HD3 = 3 * FOX_HD


def _head_cat3(x, rhs):
    hi, lo = _split_hi_lo(x)
    pieces = []
    for h in range(FOX_HEADS):
        hs = slice(h * FOX_HD, (h + 1) * FOX_HD)
        pieces += [hi[:, hs], lo[:, hs], hi[:, hs]] if rhs else [hi[:, hs], hi[:, hs], lo[:, hs]]
    return jnp.concatenate(pieces, axis=-1)


def _inproj_even_kernel(x_ref, g_ref, sh_ref, sc_ref, w_ref, hm_ref, qg_ref, kg_ref, bias_ref,
                        q3_ref, k3_ref, k_ref, v_ref, z_ref, xbc_ref, small_ref):
    h = _cat3_lhs(_normed_rows(x_ref, g_ref, sh_ref, sc_ref))
    shp3 = lambda r: (x_ref.shape[0], x_ref.shape[1], r.shape[2])
    cols = lambda lo, hi: jnp.dot(h, w_ref[:, lo:hi], preferred_element_type=F32)
    hm = hm_ref[...]
    q = _head_rms(cols(EV_Q, EV_K), hm, qg_ref[...]) * FOX_SCALE
    q3_ref[...] = _head_cat3(q, rhs=False).reshape(shp3(q3_ref))
    k = _head_rms(cols(EV_K, EV_V), hm, kg_ref[...])
    k_ref[...] = k.reshape(shp3(k_ref))
    k3_ref[...] = _head_cat3(k, rhs=True).reshape(shp3(k3_ref))
    v_ref[...] = cols(EV_V, EV_Z).reshape(shp3(v_ref))
    z_ref[...] = cols(EV_Z, EV_XBC).reshape(shp3(z_ref))
    xbc_ref[...] = cols(EV_XBC, EV_SMALL).reshape(shp3(xbc_ref))
    t = cols(EV_SMALL, EV_COLS) + bias_ref[...]
    lane = lax.broadcasted_iota(I32, t.shape, 1)
    small = jnp.where(lane < FOX_HEADS, -_softplus(-t), jnp.where(lane < FOX_HEADS + SSD_HEADS, _softplus(t), 0.0))
    small_ref[...] = small.reshape(shp3(small_ref))


def _inproj_odd_kernel(x_ref, g_ref, sh_ref, sc_ref, w_ref, bias_ref, alog_ref,
                       rw_ref, qkv_ref, z_ref, small_ref):
    h = _normed_rows(x_ref, g_ref, sh_ref, sc_ref).astype(BF16)
    shp3 = lambda r: (x_ref.shape[0], x_ref.shape[1], r.shape[2])
    rw_ref[...] = jnp.dot(h, w_ref[:, OD_RW:OD_QKV], preferred_element_type=F32).reshape(shp3(rw_ref))
    qkv_ref[...] = jnp.dot(h, w_ref[:, OD_QKV:OD_Z], preferred_element_type=F32).reshape(shp3(qkv_ref))
    z_ref[...] = jnp.dot(h, w_ref[:, OD_Z:OD_SMALL], preferred_element_type=F32).reshape(shp3(z_ref))
    t = jnp.dot(h, w_ref[:, OD_SMALL:OD_COLS], preferred_element_type=F32) + bias_ref[...]
    lane = lax.broadcasted_iota(I32, t.shape, 1)
    small = jnp.where(lane < GDN_HEADS, _sigmoid(t),
                      jnp.where(lane < 2 * GDN_HEADS, -jnp.exp(alog_ref[...]) * _softplus(t), 0.0))
    small_ref[...] = small.reshape(shp3(small_ref))


def _inproj_call(kernel_fn, name, x, g, shift, scale, w, extras, out_cols, out_dtypes):
    b, l, d = x.shape
    kb, tl = _row_blocking(b, l)
    tok = lambda c: pl.BlockSpec((kb, tl, c), lambda i, j: (i, j, 0))
    per_batch = pl.BlockSpec((kb, 1, d), lambda i, j: (i, 0, 0))
    const = _const_spec
    return pl.pallas_call(
        kernel_fn,
        out_shape=[jax.ShapeDtypeStruct((b, l, c), dt) for c, dt in zip(out_cols, out_dtypes)],
        grid=(b // kb, l // tl),
        in_specs=[tok(d), const(g), per_batch, per_batch, const(w)] + [const(e) for e in extras],
        out_specs=[tok(c) for c in out_cols],
        compiler_params=_params("parallel", "parallel"),
        name=name,
    )(x, g, shift, scale, w, *extras)


def _pad_lanes(*pieces):
    v = jnp.concatenate([p.reshape(-1).astype(F32) for p in pieces])
    return jnp.pad(v, (0, LANES - v.shape[0])).reshape(1, LANES)


def _head_mean_matrix(dim, hd):
    r = jnp.arange(dim) // hd
    return jnp.where(r[:, None] == r[None, :], 1.0 / hd, 0.0).astype(BF16)


def _pack_even_w_in(w_in):
    q, k, v, f, z, xbc, dt = jnp.split(w_in, (FOX_DIM, 2 * FOX_DIM, 3 * FOX_DIM, 3 * FOX_DIM + FOX_HEADS,
                                              3 * FOX_DIM + FOX_HEADS + SSD_INNER,
                                              3 * FOX_DIM + FOX_HEADS + SSD_INNER + SSD_CONV_DIM), axis=1)
    pad = jnp.zeros((w_in.shape[0], LANES - FOX_HEADS - SSD_HEADS), w_in.dtype)
    return _cat3_rhs(jnp.concatenate([q, k, v, z, xbc, f, dt, pad], axis=1))


def _pack_odd_w_in(w_in):
    main, small = w_in[:, :OD_SMALL], w_in[:, OD_SMALL:]
    pad = jnp.zeros((w_in.shape[0], LANES - 2 * GDN_HEADS), w_in.dtype)
    return jnp.concatenate([main, small, pad], axis=1).astype(BF16)


def _split3_bf16(x):
    c1 = x.astype(BF16)
    r1 = x - c1.astype(F32)
    c2 = r1.astype(BF16)
    c3 = (r1 - c2.astype(F32)).astype(BF16)
    return c1, c2, c3


def _rows_to_lanes(sel, x):
    nt = lambda a, b: lax.dot_general(a, b, (((1,), (1,)), ((), ())), preferred_element_type=F32)
    c1, c2, c3 = _split3_bf16(x)
    return (nt(sel, c1) + nt(sel, c2)) + nt(sel, c3)


def _fox_cum_kernel(*refs, n_past):
    if n_past:
        past_ref, small_ref, sel_ref, cum_ref, cumt_ref = refs
        seq = jnp.concatenate([past_ref[0], small_ref[0]], axis=0)
    else:
        small_ref, sel_ref, cum_ref, cumt_ref = refs
        seq = small_ref[0]
    cum = _cumsum_rows(seq)
    cum_ref[0] = cum[n_past:]
    cumt_ref[0] = _rows_to_lanes(sel_ref[...], cum)


def _fox_cumulative(small, past_logf):
    b, l, _ = small.shape
    n_past = 0 if past_logf is None else past_logf.shape[1]
    sel = jnp.eye(FOX_HEADS, LANES, dtype=BF16)
    ins, specs = [], []
    if n_past:
        ins.append(past_logf)
        specs.append(pl.BlockSpec((1, n_past, LANES), lambda i: (i, 0, 0)))
    ins += [small, sel]
    specs += [pl.BlockSpec((1, l, LANES), lambda i: (i, 0, 0)), pl.BlockSpec(sel.shape, lambda i: (0, 0))]
    return pl.pallas_call(
        functools.partial(_fox_cum_kernel, n_past=n_past),
        out_shape=[jax.ShapeDtypeStruct((b, l, LANES), F32), jax.ShapeDtypeStruct((b, FOX_HEADS, n_past + l), F32)],
        grid=(b,),
        in_specs=specs,
        out_specs=[pl.BlockSpec((1, l, LANES), lambda i: (i, 0, 0)),
                   pl.BlockSpec((1, FOX_HEADS, n_past + l), lambda i: (i, 0, 0))],
        compiler_params=_params("parallel"),
        name="fox_cumulative",
    )(*ins)


FOX_TILE = 256


def _fox_flash_kernel(q_ref, k_ref, v_ref, cq_ref, ck_ref, o_ref, m_sc, l_sc, acc_sc):
    qi, kj = pl.program_id(1), pl.program_id(2)

    @pl.when(kj == 0)
    def _():
        m_sc[...] = jnp.full(m_sc.shape, -jnp.inf, F32)
        l_sc[...] = jnp.zeros(l_sc.shape, F32)
        acc_sc[...] = jnp.zeros(acc_sc.shape, F32)

    def step(on_diagonal):
        q3, k3 = q_ref[0], k_ref[0]
        v_hi, v_lo = _split_hi_lo(v_ref[0])
        cq, ck = cq_ref[0], ck_ref[0]
        if on_diagonal:
            visible = (lax.broadcasted_iota(I32, (FOX_TILE, FOX_TILE), 1)
                       <= lax.broadcasted_iota(I32, (FOX_TILE, FOX_TILE), 0))
        for h in range(FOX_HEADS):
            hs = slice(h * FOX_HD, (h + 1) * FOX_HD)
            hs3 = slice(h * FOX_HD3, (h + 1) * FOX_HD3)
            s = _scores3(q3[:, hs3], k3[:, hs3]) + (cq[:, h:h + 1] - ck[h:h + 1, :])
            if on_diagonal:
                s = jnp.where(visible, s, -jnp.inf)
            m_prev = m_sc[h]
            m_new = jnp.maximum(m_prev, jnp.max(s, axis=-1, keepdims=True))
            p = jnp.exp(s - m_new)
            alpha = jnp.exp(m_prev - m_new)
            l_sc[h] = alpha * l_sc[h] + jnp.sum(p, axis=-1, keepdims=True)
            acc_sc[h] = alpha * acc_sc[h] + _pv3(p, v_hi[:, hs], v_lo[:, hs])
            m_sc[h] = m_new

    @pl.when(kj < qi)
    def _():
        step(on_diagonal=False)

    @pl.when(kj == qi)
    def _():
        step(on_diagonal=True)
        o_ref[0] = jnp.concatenate([acc_sc[h] / l_sc[h] for h in range(FOX_HEADS)], axis=-1)


def _scores3(q3, k3):
    return lax.dot_general(q3, k3, (((1,), (1,)), ((), ())), preferred_element_type=F32)


def _pv3(p, v_hi, v_lo):
    p_hi, p_lo = _split_hi_lo(p)
    dot = lambda a, b: jnp.dot(a, b, preferred_element_type=F32)
    return dot(p_hi, v_hi) + (dot(p_hi, v_lo) + dot(p_lo, v_hi))


def _fox_attention_prompt(q3, k3, v, cum, cumt):
    b, l, _ = v.shape
    n = l // FOX_TILE
    qspec = lambda c: pl.BlockSpec((1, FOX_TILE, c), lambda i, j, t: (i, j, 0))
    kspec = lambda c: pl.BlockSpec((1, FOX_TILE, c), lambda i, j, t: (i, jnp.minimum(t, j), 0))
    return pl.pallas_call(
        _fox_flash_kernel,
        out_shape=jax.ShapeDtypeStruct((b, l, FOX_DIM), F32),
        grid=(b, n, n),
        in_specs=[qspec(FOX_HEADS * FOX_HD3), kspec(FOX_HEADS * FOX_HD3), kspec(FOX_DIM), qspec(LANES),
                  pl.BlockSpec((1, FOX_HEADS, FOX_TILE), lambda i, j, t: (i, 0, jnp.minimum(t, j)))],
        out_specs=qspec(FOX_DIM),
        scratch_shapes=[pltpu.VMEM((FOX_HEADS, FOX_TILE, 1), F32), pltpu.VMEM((FOX_HEADS, FOX_TILE, 1), F32),
                        pltpu.VMEM((FOX_HEADS, FOX_TILE, FOX_HD), F32)],
        compiler_params=_params("parallel", "parallel", "arbitrary"),
        name="fox_flash",
    )(q3, k3, v, cum, cumt)


def _fox_decode_kernel(q_ref, kp_ref, vp_ref, kn_ref, vn_ref, cq_ref, ck_ref, o_ref):
    n_past, l = kp_ref.shape[1], q_ref.shape[1]
    q3, kn3 = q_ref[0], kn_ref[0]
    kp3 = _head_cat3(kp_ref[0], rhs=True)
    vp_hi, vp_lo = _split_hi_lo(vp_ref[0])
    vn_hi, vn_lo = _split_hi_lo(vn_ref[0])
    cq, ck = cq_ref[0], ck_ref[0]
    row = lax.broadcasted_iota(I32, (l, l), 0)
    col = lax.broadcasted_iota(I32, (l, l), 1)
    outs = []
    for h in range(FOX_HEADS):
        hs = slice(h * FOX_HD, (h + 1) * FOX_HD)
        hs3 = slice(h * FOX_HD3, (h + 1) * FOX_HD3)
        cqh = cq[:, h:h + 1]
        s_past = _scores3(q3[:, hs3], kp3[:, hs3]) + (cqh - ck[h:h + 1, :n_past])
        s_new = _scores3(q3[:, hs3], kn3[:, hs3]) + (cqh - ck[h:h + 1, n_past:])
        s_new = jnp.where(col <= row, s_new, -jnp.inf)
        m = jnp.maximum(jnp.max(s_past, axis=-1, keepdims=True), jnp.max(s_new, axis=-1, keepdims=True))
        p_past, p_new = jnp.exp(s_past - m), jnp.exp(s_new - m)
        denom = jnp.sum(p_past, axis=-1, keepdims=True) + jnp.sum(p_new, axis=-1, keepdims=True)
        outs.append((_pv3(p_past, vp_hi[:, hs], vp_lo[:, hs]) + _pv3(p_new, vn_hi[:, hs], vn_lo[:, hs])) / denom)
    o_ref[0] = jnp.concatenate(outs, axis=-1)


def _fox_attention_decode(q3, k3, v, k_past, v_past, cum, cumt):
    b, l, _ = v.shape
    n_past = k_past.shape[1]
    new = lambda c: pl.BlockSpec((1, l, c), lambda i: (i, 0, 0))
    past = pl.BlockSpec((1, n_past, FOX_DIM), lambda i: (i, 0, 0))
    return pl.pallas_call(
        _fox_decode_kernel,
        out_shape=jax.ShapeDtypeStruct((b, l, FOX_DIM), F32),
        grid=(b,),
        in_specs=[new(FOX_HEADS * FOX_HD3), past, past, new(FOX_HEADS * FOX_HD3), new(FOX_DIM), new(LANES),
                  pl.BlockSpec((1, FOX_HEADS, n_past + l), lambda i: (i, 0, 0))],
        out_specs=new(FOX_DIM),
        compiler_params=_params("parallel"),
        name="fox_decode",
    )(q3, k_past, v_past, k3, v, cum, cumt)


CONV_TAIL = 8


def _causal_conv(x, tail, w):
    q = x.shape[0]
    row = lax.broadcasted_iota(I32, (CONV_TAIL, x.shape[1]), 0)
    acc = x * w[CONV_W - 1:CONV_W]
    for s in range(1, CONV_W):
        xr = pltpu.roll(x, s, 0)
        head = jnp.where(row < s, pltpu.roll(tail, s, 0), xr[:CONV_TAIL])
        shifted = head if q == CONV_TAIL else jnp.concatenate([head, xr[CONV_TAIL:]], axis=0)
        acc = acc + shifted * w[CONV_W - 1 - s:CONV_W - s]
    return acc


def _conv_tail_from_state(state):
    return jnp.pad(state, ((0, 0), (CONV_TAIL - (CONV_W - 1), 0), (0, 0)))


SSD_CHUNK = 128
SSD_DT_LANE = FOX_HEADS


def _ssd_kernel(xbc_ref, z_ref, small_ref, tail0_ref, h0_ref, cw_ref, cb_ref, alog_ref, drep_ref, ng_ref, sel_ref,
                y_ref, tail_ref, hout_ref, tail_sc, h_sc):
    c = pl.program_id(1)

    @pl.when(c == 0)
    def _():
        tail_sc[...] = tail0_ref[0]
        h_sc[...] = h0_ref[0]

    x = xbc_ref[0]
    q = x.shape[0]
    act = _silu(_causal_conv(x, tail_sc[...], cw_ref[...]) + cb_ref[...])
    tail_sc[...] = x[q - CONV_TAIL:]
    xs = act[:, :SSD_INNER]
    bm = act[:, SSD_INNER:SSD_INNER + SSD_GROUPS * SSD_N]
    cm = act[:, SSD_INNER + SSD_GROUPS * SSD_N:]

    small = small_ref[0]
    lane = lax.broadcasted_iota(I32, small.shape, 1)
    is_dt = jnp.logical_and(lane >= SSD_DT_LANE, lane < SSD_DT_LANE + SSD_HEADS)
    da = jnp.where(is_dt, small * -jnp.exp(alog_ref[...]), 0.0)
    acum = _cumsum_rows(da)
    acum_t = _rows_to_lanes(sel_ref[...], acum)
    trow = lax.broadcasted_iota(I32, (q, q), 0)
    tcol = lax.broadcasted_iota(I32, (q, q), 1)
    heads_per_group = SSD_HEADS // SSD_GROUPS
    ys = []
    for g in range(SSD_GROUPS):
        bg = bm[:, g * SSD_N:(g + 1) * SSD_N]
        cg = cm[:, g * SSD_N:(g + 1) * SSD_N]
        cb = _mm3_nt(cg, bg)
        for h in range(g * heads_per_group, (g + 1) * heads_per_group):
            a_col = acum[:, SSD_DT_LANE + h:SSD_DT_LANE + h + 1]
            dt_col = small[:, SSD_DT_LANE + h:SSD_DT_LANE + h + 1]
            a_last = a_col[q - 1:q]
            seg = jnp.exp(jnp.where(trow >= tcol, a_col - acum_t[h:h + 1, :], -jnp.inf))
            xdt = xs[:, h * SSD_HD:(h + 1) * SSD_HD] * dt_col
            state = h_sc[h]
            ys.append(_mm3(cb * seg, xdt) + jnp.exp(a_col) * _mm3_nt(cg, state))
            h_sc[h] = state * jnp.exp(a_last) + _mm3_tn(xdt * jnp.exp(a_last - a_col), bg)
    y = jnp.concatenate(ys, axis=-1) + drep_ref[...] * xs
    y = y * _silu(z_ref[0])
    gw = SSD_INNER // SSD_GROUPS
    normed = []
    for g in range(SSD_GROUPS):
        yg = y[:, g * gw:(g + 1) * gw]
        normed.append(yg * lax.rsqrt(jnp.mean(yg * yg, axis=-1, keepdims=True) + EPS))
    y_ref[0] = (jnp.concatenate(normed, axis=-1) * ng_ref[...]).astype(y_ref.dtype)

    @pl.when(c == pl.num_programs(1) - 1)
    def _():
        tail_ref[0] = tail_sc[...]
        hout_ref[0] = h_sc[...]


def _ssd_mixer(xbc, z, small, conv_state, h0, conv_w, conv_b, a_log, d_skip, norm_g):
    b, l, _ = xbc.shape
    q = min(SSD_CHUNK, l)
    tail0 = _conv_tail_from_state(conv_state)
    alog = _pad_lanes(jnp.zeros(SSD_DT_LANE), a_log)
    drep = jnp.repeat(d_skip.astype(F32), SSD_HD)[None]
    sel = jnp.eye(SSD_HEADS, LANES, k=SSD_DT_LANE, dtype=BF16)
    tok = lambda c: pl.BlockSpec((1, q, c), lambda i, j: (i, j, 0))
    per_b = lambda a: pl.BlockSpec((1,) + a.shape[1:], lambda i, j: (i,) + (0,) * (a.ndim - 1))
    const = _const_spec
    consts = [conv_w, conv_b[None], alog, drep, norm_g[None], sel]
    y, tail, h_new = pl.pallas_call(
        _ssd_kernel,
        out_shape=[jax.ShapeDtypeStruct((b, l, SSD_INNER), F32),
                   jax.ShapeDtypeStruct((b, CONV_TAIL, SSD_CONV_DIM), F32),
                   jax.ShapeDtypeStruct(h0.shape, F32)],
        grid=(b, l // q),
        in_specs=[tok(SSD_CONV_DIM), tok(SSD_INNER), tok(LANES), per_b(tail0), per_b(h0)] + [const(a) for a in consts],
        out_specs=[tok(SSD_INNER), per_b(tail0), per_b(h0)],
        scratch_shapes=[pltpu.VMEM((CONV_TAIL, SSD_CONV_DIM), F32), pltpu.VMEM(h0.shape[1:], F32)],
        compiler_params=_params("parallel", "arbitrary"),
        name="ssd_mixer",
    )(xbc, z, small, tail0, h0, *consts)
    return y, tail[:, CONV_TAIL - (CONV_W - 1):], h_new


REC_CHUNK = 64


INV_BASE = 8


def _unit_lower_inverse(n):
    c = n.shape[-1]
    mm = _mm3 if n.ndim == 2 else _bmm3
    merge_mm = _mm if n.ndim == 2 else _bmm
    row = lax.broadcasted_iota(I32, (c, c), 0)
    col = lax.broadcasted_iota(I32, (c, c), 1)
    shift = INV_BASE.bit_length() - 1
    same = lax.shift_right_logical(row, shift) == lax.shift_right_logical(col, shift)
    diag = jnp.where(same, n, 0.0)
    inv = jnp.where(row == col, 1.0, 0.0) + diag
    power = diag
    span = 2
    while span < INV_BASE:
        power = mm(power, power)
        inv = inv + mm(inv, power)
        span *= 2
    size = INV_BASE
    while size < c:
        shift += 1
        size *= 2
        merged = lax.shift_right_logical(row, shift) == lax.shift_right_logical(col, shift)
        off = jnp.where(jnp.logical_and(merged, jnp.logical_not(same)), n, 0.0)
        inv = inv + merge_mm(merge_mm(inv, off), inv)
        same = merged
    return inv


def _shift_rows_by_one(x, tail):
    q = x.shape[0]
    row = lax.broadcasted_iota(I32, (CONV_TAIL, x.shape[1]), 0)
    xr = pltpu.roll(x, 1, 0)
    head = jnp.where(row < 1, pltpu.roll(tail, 1, 0), xr[:CONV_TAIL])
    return head if q == CONV_TAIL else jnp.concatenate([head, xr[CONV_TAIL:]], axis=0)


RW_R, RW_K, RW_V, RW_LORA, RW_GATE = 0, RWKV_DIM, 2 * RWKV_DIM, 3 * RWKV_DIM, 3 * RWKV_DIM + RWKV_DECAY_LORA + RWKV_A_LORA


def _rwkv_kernel(rw_ref, tail0_ref, s0_ref, mu_ref, w0_ref, w2_ref, a0_ref, a2_ref, g2_ref, kk_ref, ka_ref, rk_ref,
                 lng_ref, lnb_ref, hsum_ref, o_ref, sout_ref, tail_sc, s_sc):
    c = pl.program_id(1)

    @pl.when(c == 0)
    def _():
        tail_sc[...] = tail0_ref[0]
        s_sc[...] = s0_ref[0]

    x = rw_ref[0]
    q = x.shape[0]
    mixed = x + (_shift_rows_by_one(x, tail_sc[...]) - x) * mu_ref[...]
    tail_sc[...] = x[q - CONV_TAIL:]
    r = mixed[:, RW_R:RW_K]
    k = mixed[:, RW_K:RW_V]
    v = mixed[:, RW_V:RW_LORA]
    lora = mixed[:, RW_LORA:RW_GATE]
    w_log = -_softplus(-(w0_ref[...] + _mm(jnp.tanh(lora), w2_ref[...]))) - 0.5
    logw = -jnp.exp(w_log)
    icl = _sigmoid(a0_ref[...] + _mm(lora, a2_ref[...]))
    out_gate = _mm(_sigmoid(mixed[:, RW_GATE:]), g2_ref[...])
    kk = k * kk_ref[...]
    kk = kk * lax.rsqrt(_mm(kk * kk, hsum_ref[...]) + EPS)
    k2 = k * (1.0 + (icl - 1.0) * ka_ref[...])
    cum = _cumsum_rows(logw)
    w_run = jnp.exp(cum)
    w_inv = jnp.exp(-cum)
    rt = r * w_run
    at = -kk * jnp.exp(cum - logw)
    bt = kk * icl * w_inv
    kt = k2 * w_inv
    trow = lax.broadcasted_iota(I32, (q, q), 0)
    tcol = lax.broadcasted_iota(I32, (q, q), 1)
    strict, incl = trow > tcol, trow >= tcol
    hd = RWKV_HD
    at, rt, bt, kt, vh, rh, k2h = (_heads(t, hd) for t in (at, rt, bt, kt, v, r, k2))
    a_ab = jnp.where(strict, _bmm_nt(at, bt), 0.0)
    a_ak = jnp.where(strict, _bmm_nt(at, kt), 0.0)
    a_rb = jnp.where(incl, _bmm_nt(rt, bt), 0.0)
    a_rk = jnp.where(incl, _bmm_nt(rt, kt), 0.0)
    s0 = s_sc[...]
    u = _bmm(_unit_lower_inverse(a_ab), _bmm_nt(at, s0) + _bmm(a_ak, vh))
    o = _bmm_nt(rt, s0) + (_bmm(a_rb, u) + _bmm(a_rk, vh))
    s_sc[...] = (s0 + (_bmm_tn(u, bt) + _bmm_tn(vh, kt))) * _heads(w_run[q - 1:q], hd)
    mean = jnp.mean(o, axis=-1, keepdims=True)
    var = jnp.mean(jnp.square(o - mean), axis=-1, keepdims=True)
    o = (o - mean) * lax.rsqrt(var + RWKV_GN_EPS) * _heads(lng_ref[...], hd) + _heads(lnb_ref[...], hd)
    bonus = jnp.sum(rh * k2h * _heads(rk_ref[...], hd), axis=-1, keepdims=True) * vh
    o_ref[0] = (_unheads(o + bonus) * out_gate).astype(o_ref.dtype)

    @pl.when(c == pl.num_programs(1) - 1)
    def _():
        sout_ref[0] = s_sc[...]


def _rwkv_mixer(rw, shift_state, s0, p):
    b, l, _ = rw.shape
    q = min(REC_CHUNK, l)
    tail0 = jnp.pad(shift_state[:, None, :], ((0, 0), (CONV_TAIL - 1, 0), (0, 0)))
    zeros = jnp.zeros((RWKV_DECAY_LORA, RWKV_DIM), F32)
    consts = [p['rwkv_mu'][None], p['rwkv_w0'][None],
              jnp.concatenate([p['rwkv_w2'], zeros], axis=0).astype(BF16), p['rwkv_a0'][None],
              jnp.concatenate([zeros, p['rwkv_a2']], axis=0).astype(BF16), p['rwkv_g2'].astype(BF16),
              p['rwkv_k_k'][None], p['rwkv_k_a'][None], p['rwkv_r_k'].reshape(1, RWKV_DIM),
              p['rwkv_ln_g'][None], p['rwkv_ln_b'][None], _head_mean_matrix(RWKV_DIM, RWKV_HD) * RWKV_HD]
    tok = lambda c: pl.BlockSpec((1, q, c), lambda i, j: (i, j, 0))
    per_b = lambda a: pl.BlockSpec((1,) + a.shape[1:], lambda i, j: (i,) + (0,) * (a.ndim - 1))
    const = _const_spec
    return pl.pallas_call(
        _rwkv_kernel,
        out_shape=[jax.ShapeDtypeStruct((b, l, RWKV_DIM), BF16), jax.ShapeDtypeStruct(s0.shape, F32)],
        grid=(b, l // q),
        in_specs=[tok(RWKV_PROJ), per_b(tail0), per_b(s0)] + [const(a) for a in consts],
        out_specs=[tok(RWKV_DIM), per_b(s0)],
        scratch_shapes=[pltpu.VMEM((CONV_TAIL, RWKV_PROJ), F32), pltpu.VMEM(s0.shape[1:], F32)],
        compiler_params=_params("parallel", "arbitrary"),
        name="rwkv_mixer",
    )(rw, tail0, s0, *consts)


GDN_BETA_LANE, GDN_G_LANE = 0, GDN_HEADS


def _l2norm(x):
    return x * lax.rsqrt(jnp.sum(x * x, axis=-1, keepdims=True) + EPS)


def _gdn_kernel(qkv_ref, z_ref, small_ref, tail0_ref, s0_ref, cw_ref, ng_ref, sel_ref,
                o_ref, tail_ref, sout_ref, tail_sc, s_sc):
    c = pl.program_id(1)

    @pl.when(c == 0)
    def _():
        tail_sc[...] = tail0_ref[0]
        s_sc[...] = s0_ref[0]

    x = qkv_ref[0]
    q = x.shape[0]
    act = _silu(_causal_conv(x, tail_sc[...], cw_ref[...]))
    tail_sc[...] = x[q - CONV_TAIL:]
    small = small_ref[0]
    z = z_ref[0]
    lane = lax.broadcasted_iota(I32, small.shape, 1)
    is_g = jnp.logical_and(lane >= GDN_G_LANE, lane < GDN_G_LANE + GDN_HEADS)
    gam = _cumsum_rows(jnp.where(is_g, small, 0.0))
    gam_t = _rows_to_lanes(sel_ref[...], gam)
    trow = lax.broadcasted_iota(I32, (q, q), 0)
    tcol = lax.broadcasted_iota(I32, (q, q), 1)
    heads = range(GDN_HEADS)
    qh = _l2norm(_heads(act[:, :GDN_DIM], GDN_HD)) * (GDN_HD ** -0.5)
    kh = _l2norm(_heads(act[:, GDN_DIM:2 * GDN_DIM], GDN_HD))
    vh = _heads(act[:, 2 * GDN_DIM:], GDN_HD)
    beta = jnp.stack([small[:, GDN_BETA_LANE + h:GDN_BETA_LANE + h + 1] for h in heads])
    g_col = jnp.stack([gam[:, GDN_G_LANE + h:GDN_G_LANE + h + 1] for h in heads])
    g_row = jnp.stack([gam_t[h:h + 1, :] for h in heads])
    g_last = g_col[:, q - 1:q]
    diff = g_col - g_row
    a_mat = _bmm_nt(kh, kh) * jnp.exp(jnp.where(trow > tcol, diff, -jnp.inf)) * beta
    eg = jnp.exp(g_col)
    uw = _bmm(_unit_lower_inverse(-a_mat), jnp.concatenate([vh * beta, kh * (beta * eg)], axis=2))
    qk = _bmm_nt(qh, kh) * jnp.exp(jnp.where(trow >= tcol, diff, -jnp.inf))
    state = s_sc[...]
    nu = uw[:, :, :GDN_HD] - _bmm(uw[:, :, GDN_HD:], state)
    o = _bmm(qh * eg, state) + _bmm(qk, nu)
    s_sc[...] = state * jnp.exp(g_last) + _bmm_tn(kh * jnp.exp(g_last - g_col), nu)
    o = o * lax.rsqrt(jnp.mean(o * o, axis=-1, keepdims=True) + EPS) * ng_ref[...]
    o_ref[0] = (_unheads(o) * _silu(z)).astype(o_ref.dtype)

    @pl.when(c == pl.num_programs(1) - 1)
    def _():
        tail_ref[0] = tail_sc[...]
        sout_ref[0] = s_sc[...]


def _gdn_mixer(qkv, z, small, conv_state, s0, conv_w, norm_g):
    b, l, _ = qkv.shape
    q = min(REC_CHUNK, l)
    tail0 = _conv_tail_from_state(conv_state)
    sel = jnp.eye(GDN_HEADS, LANES, k=GDN_G_LANE, dtype=BF16)
    consts = [conv_w, norm_g[None], sel]
    tok = lambda c: pl.BlockSpec((1, q, c), lambda i, j: (i, j, 0))
    per_b = lambda a: pl.BlockSpec((1,) + a.shape[1:], lambda i, j: (i,) + (0,) * (a.ndim - 1))
    const = _const_spec
    o, tail, s_new = pl.pallas_call(
        _gdn_kernel,
        out_shape=[jax.ShapeDtypeStruct((b, l, GDN_DIM), BF16),
                   jax.ShapeDtypeStruct((b, CONV_TAIL, GDN_CONV_DIM), F32),
                   jax.ShapeDtypeStruct(s0.shape, F32)],
        grid=(b, l // q),
        in_specs=[tok(GDN_CONV_DIM), tok(GDN_DIM), tok(LANES), per_b(tail0), per_b(s0)] + [const(a) for a in consts],
        out_specs=[tok(GDN_DIM), per_b(tail0), per_b(s0)],
        scratch_shapes=[pltpu.VMEM((CONV_TAIL, GDN_CONV_DIM), F32), pltpu.VMEM(s0.shape[1:], F32)],
        compiler_params=_params("parallel", "arbitrary"),
        name="gdn_mixer",
    )(qkv, z, small, tail0, s0, *consts)
    return o, tail[:, CONV_TAIL - (CONV_W - 1):], s_new


def _outproj_kernel(a_ref, b_ref, w_ref, x_ref, gate_ref, g_ref, sh_ref, sc_ref, rwt_ref,
                    xo_ref, h_ref, logit_ref, *, split_inputs):
    kb, tl, d = x_ref.shape
    rows = kb * tl
    a = a_ref[...].reshape(rows, a_ref.shape[2])
    b = b_ref[...].reshape(rows, b_ref.shape[2])
    lhs = jnp.concatenate([_cat3_lhs(a), _cat3_lhs(b)] if split_inputs else [a.astype(BF16), b.astype(BF16)], axis=-1)
    mix = jnp.dot(lhs, w_ref[...], preferred_element_type=F32)
    x_new = x_ref[...] + gate_ref[...] * mix.reshape(kb, tl, d)
    xo_ref[...] = x_new
    h = _ada_norm(x_new, g_ref[...], sh_ref[...], sc_ref[...])
    h_ref[...] = h.astype(h_ref.dtype)
    logit_ref[...] = lax.dot_general(rwt_ref[...], _cat3_lhs(h.reshape(rows, d)), (((1,), (1,)), ((), ())),
                                     preferred_element_type=F32)


def _outproj_residual(a, b_half, w_out, x, gate, g2, shift2, scale2, router_w, split_inputs):
    b, l, d = x.shape
    kb, tl = _row_blocking(b, l)
    half = a.shape[2]
    pack = _cat3_rhs if split_inputs else (lambda w: w.astype(BF16))
    w = jnp.concatenate([pack(w_out[:half]), pack(w_out[half:])], axis=0)
    rwt = _cat3_rhs(router_w.T, axis=1)
    n_l = l // tl
    tok = lambda c: pl.BlockSpec((kb, tl, c), lambda i, j: (i, j, 0))
    per_batch = pl.BlockSpec((kb, 1, d), lambda i, j: (i, 0, 0))
    const = _const_spec
    return pl.pallas_call(
        functools.partial(_outproj_kernel, split_inputs=split_inputs),
        out_shape=[jax.ShapeDtypeStruct((b, l, d), F32), jax.ShapeDtypeStruct((b, l, d), BF16),
                   jax.ShapeDtypeStruct((N_EXPERTS, b * l), F32)],
        grid=(b // kb, n_l),
        in_specs=[tok(half), tok(b_half.shape[2]), const(w), tok(d), per_batch, const(g2), per_batch, per_batch, const(rwt)],
        out_specs=[tok(d), tok(d), pl.BlockSpec((N_EXPERTS, kb * tl), lambda i, j: (0, i * n_l + j))],
        compiler_params=_params("parallel", "parallel"),
        name="outproj_residual",
    )(a, b_half, w, x, gate, g2, shift2, scale2, rwt)


def _beats(a, b, a_first):
    return jnp.where(a > b, 1, jnp.where(a == b, a_first, 0))


def _router_kernel(logit_ref, bias_ref, gate_ref, pos_ref, count_ref):
    tm = logit_ref.shape[1]
    scores = _sigmoid(logit_ref[...])
    sel = scores + bias_ref[...]
    ng, eg = N_EXPERT_GROUPS, EXPERTS_PER_GROUP
    sub = lax.broadcasted_iota(I32, (eg, tm), 0)
    groups, grp = [], []
    for g in range(ng):
        xg = sel[g * eg:(g + 1) * eg]
        m1 = jnp.max(xg, axis=0, keepdims=True)
        first = jnp.min(jnp.where(xg == m1, sub, eg), axis=0, keepdims=True)
        m2 = jnp.max(jnp.where(sub == first, -jnp.inf, xg), axis=0, keepdims=True)
        groups.append(xg)
        grp.append(m1 + m2)
    masked = []
    for g in range(ng):
        rank = sum(_beats(grp[o], grp[g], 1 if o < g else 0) for o in range(ng) if o != g)
        masked.append(jnp.where(rank < TOPK_GROUPS, groups[g], -jnp.inf))
    selm = jnp.concatenate(masked, axis=0)
    erow = lax.broadcasted_iota(I32, (N_EXPERTS, tm), 0)
    rank = jnp.zeros((N_EXPERTS, tm), I32)
    for o in range(N_EXPERTS):
        rank = rank + _beats(selm[o:o + 1], selm, jnp.where(erow > o, 1, 0))
    gate = jnp.concatenate(
        [jnp.sum(jnp.where(rank == k, scores, 0.0), axis=0, keepdims=True) for k in range(TOP_K)], axis=0)
    gate_ref[...] = gate / jnp.sum(gate, axis=0, keepdims=True) * ROUTED_SCALE
    chosen = jnp.where(rank < TOP_K, 1.0, 0.0)
    earlier = jnp.where(lax.broadcasted_iota(I32, (tm, tm), 0) < lax.broadcasted_iota(I32, (tm, tm), 1), 1.0, 0.0)
    before = _mm(chosen, earlier)
    count = jnp.sum(chosen, axis=1, keepdims=True)
    padded = jnp.ceil(count * (1.0 / SEG_ALIGN)) * SEG_ALIGN
    lower = jnp.where(lax.broadcasted_iota(I32, (N_EXPERTS, N_EXPERTS), 1) < lax.broadcasted_iota(I32, (N_EXPERTS, N_EXPERTS), 0), 1.0, 0.0)
    seg_start = _mm(lower, jnp.broadcast_to(padded, (N_EXPERTS, LANES)))[:, :1]
    slot = seg_start + before
    pos_ref[...] = jnp.concatenate(
        [jnp.sum(jnp.where(rank == k, slot, 0.0), axis=0, keepdims=True) for k in range(TOP_K)], axis=0).astype(I32)
    count_ref[0] = jnp.broadcast_to(count, (N_EXPERTS, LANES)).astype(I32)


def _router(logits, router_b):
    t = logits.shape[1]
    n_tiles = t // ROW_TILE
    bias = router_b.astype(F32)[:, None]
    tile = lambda rows: pl.BlockSpec((rows, ROW_TILE), lambda i: (0, i))
    return pl.pallas_call(
        _router_kernel,
        out_shape=[jax.ShapeDtypeStruct((TOP_K, t), F32), jax.ShapeDtypeStruct((TOP_K, t), I32),
                   jax.ShapeDtypeStruct((n_tiles, N_EXPERTS, LANES), I32)],
        grid=(n_tiles,),
        in_specs=[tile(N_EXPERTS), _const_spec(bias)],
        out_specs=[tile(TOP_K), tile(TOP_K), pl.BlockSpec((1, N_EXPERTS, LANES), lambda i: (i, 0, 0))],
        compiler_params=_params("parallel"),
        name="moe_router",
    )(logits, bias)


SEG_ALIGN = 16
SEG_CHUNK = 32
EXPERT_BLOCK = 512
ONEHOT_CHUNK = 256
XS_COLS = D_MODEL + LANES
LOCAL_ROWS = -(-(TOP_K * ROW_TILE + N_EXPERTS * (SEG_ALIGN - 1)) // ONEHOT_CHUNK) * ONEHOT_CHUNK


def _segment_dmas(n_rows, src_ref, src_row, dst_ref, dst_row, sem, wait):
    def dma(rows, s, d):
        cp = pltpu.make_async_copy(src_ref.at[pl.ds(pl.multiple_of(s, SEG_ALIGN), rows)],
                                   dst_ref.at[pl.ds(pl.multiple_of(d, SEG_ALIGN), rows)], sem)
        cp.wait() if wait else cp.start()

    n_full = lax.shift_right_logical(n_rows, SEG_CHUNK.bit_length() - 1)

    def bulk(q, carry):
        dma(SEG_CHUNK, src_row + q * SEG_CHUNK, dst_row + q * SEG_CHUNK)
        return carry

    lax.fori_loop(0, n_full, bulk, 0)
    off = n_full * SEG_CHUNK
    size = SEG_CHUNK // 2
    while size >= SEG_ALIGN:
        has = (n_rows & size) != 0

        @pl.when(has)
        def _(size=size, off=off):
            dma(size, src_row + off, dst_row + off)

        off = off + jnp.where(has, size, 0)
        size //= 2


def _tile_segments(tile, cnt_ref, lstart_ref, gstart_ref, local_ref, global_ref, sem, to_global, wait):
    base = tile * N_EXPERTS

    def seg(e, carry):
        n, lo, go = cnt_ref[base + e], lstart_ref[base + e], gstart_ref[base + e]
        if to_global:
            _segment_dmas(n, local_ref, lo, global_ref, go, sem, wait)
        else:
            _segment_dmas(n, global_ref, go, local_ref, lo, sem, wait)
        return carry

    lax.fori_loop(0, N_EXPERTS, seg, 0)


def _zero_fill(tail_rows_ref, tail_start_ref, used_ref, zero_ref, xs_ref, sem, wait):
    def tail(e, carry):
        _segment_dmas(tail_rows_ref[e], zero_ref, 0, xs_ref, tail_start_ref[e], sem, wait)
        return carry

    lax.fori_loop(0, N_EXPERTS, tail, 0)

    def block(j, carry):
        cp = pltpu.make_async_copy(zero_ref, xs_ref.at[pl.ds(pl.multiple_of(j * EXPERT_BLOCK, EXPERT_BLOCK), EXPERT_BLOCK)], sem)
        cp.wait() if wait else cp.start()
        return carry

    lax.fori_loop(used_ref[0], xs_ref.shape[0] // EXPERT_BLOCK, block, 0)


def _dispatch_kernel(cnt_ref, lstart_ref, gstart_ref, tail_rows_ref, tail_start_ref, used_ref,
                     h_ref, pos_ref, gate_ref, xs_ref, loc_ref, zero_ref, sem):
    tile = pl.program_id(0) * pl.num_programs(1) + pl.program_id(1)
    last = pl.num_programs(0) * pl.num_programs(1) - 1
    kb, tl, d = h_ref.shape
    tm = kb * tl
    h = h_ref[...].reshape(tm, d)
    pos, gate = pos_ref[...], gate_ref[...]
    row = lax.broadcasted_iota(I32, (ONEHOT_CHUNK, tm), 0)
    lane = lax.broadcasted_iota(I32, (ONEHOT_CHUNK, LANES), 1)
    used_rows = lstart_ref[(tile + 1) * N_EXPERTS - 1] + cnt_ref[(tile + 1) * N_EXPERTS - 1]
    for c in range(LOCAL_ROWS // ONEHOT_CHUNK):
        @pl.when(c * ONEHOT_CHUNK < used_rows)
        def _(c=c):
            gsel = jnp.zeros((ONEHOT_CHUNK, tm), F32)
            for k in range(TOP_K):
                gsel = jnp.where((pos[k:k + 1, :] - c * ONEHOT_CHUNK) == row, gate[k:k + 1, :], gsel)
            onehot = jnp.where(gsel != 0.0, 1.0, 0.0)
            rows = jnp.dot(onehot.astype(BF16), h, preferred_element_type=F32)
            g1, g2, g3 = (t.astype(F32) for t in _split3_bf16(jnp.sum(gsel, axis=1, keepdims=True)))
            sl = slice(c * ONEHOT_CHUNK, (c + 1) * ONEHOT_CHUNK)
            loc_ref[sl, :d] = rows.astype(BF16)
            loc_ref[sl, d:] = jnp.where(lane == 0, g1, jnp.where(lane == 1, g2, jnp.where(lane == 2, g3, 0.0))).astype(BF16)

    @pl.when(tile == last)
    def _():
        zero_ref[...] = jnp.zeros(zero_ref.shape, zero_ref.dtype)
        _zero_fill(tail_rows_ref, tail_start_ref, used_ref, zero_ref, xs_ref, sem, wait=False)

    _tile_segments(tile, cnt_ref, lstart_ref, gstart_ref, loc_ref, xs_ref, sem, to_global=True, wait=False)
    _tile_segments(tile, cnt_ref, lstart_ref, gstart_ref, loc_ref, xs_ref, sem, to_global=True, wait=True)

    @pl.when(tile == last)
    def _():
        _zero_fill(tail_rows_ref, tail_start_ref, used_ref, zero_ref, xs_ref, sem, wait=True)


def _moe_dispatch(h, pos, gate, plan):
    b, l, d = h.shape
    kb, tl = _row_blocking(b, l)
    n_l = l // tl
    tok = pl.BlockSpec((kb, tl, d), lambda i, j, *_: (i, j, 0))
    per_choice = pl.BlockSpec((TOP_K, kb * tl), lambda i, j, *_: (0, i * n_l + j))
    return pl.pallas_call(
        _dispatch_kernel,
        out_shape=jax.ShapeDtypeStruct((plan['cap'], XS_COLS), BF16),
        grid_spec=pltpu.PrefetchScalarGridSpec(
            num_scalar_prefetch=6,
            grid=(b // kb, n_l),
            in_specs=[tok, per_choice, per_choice],
            out_specs=pl.BlockSpec(memory_space=pl.ANY),
            scratch_shapes=[pltpu.VMEM((LOCAL_ROWS, XS_COLS), BF16), pltpu.VMEM((EXPERT_BLOCK, XS_COLS), BF16),
                            pltpu.SemaphoreType.DMA]),
        compiler_params=_params("arbitrary", "arbitrary"),
        name="moe_dispatch",
    )(plan['seg_rows'], plan['local_start'], plan['global_start'], plan['tail_rows'], plan['tail_start'],
      plan['blocks_used'], h, pos, gate)


def _swiglu(x, w1, w3, w2):
    hid = _silu(jnp.dot(x, w1, preferred_element_type=F32)) * jnp.dot(x, w3, preferred_element_type=F32)
    return jnp.dot(hid.astype(BF16), w2, preferred_element_type=F32)


def _expert_block_kernel(be_ref, nused_ref, x_ref, w1_ref, w3_ref, w2_ref, y_ref, w1_sc, w3_sc, w2_sc):
    j = pl.program_id(0)

    @pl.when(jnp.logical_or(j == 0, be_ref[j] != be_ref[jnp.maximum(j - 1, 0)]))
    def _():
        w1_sc[...] = w1_ref[0, 0].astype(BF16)
        w3_sc[...] = w3_ref[0, 0].astype(BF16)
        w2_sc[...] = w2_ref[0, 0].astype(BF16)

    @pl.when(j < nused_ref[0])
    def _():
        g = x_ref[:, D_MODEL:].astype(F32)
        gate = (g[:, 0:1] + g[:, 1:2]) + g[:, 2:3]
        out = _swiglu(x_ref[:, :D_MODEL], w1_sc[...], w3_sc[...], w2_sc[...])
        y_ref[...] = (out * gate).astype(y_ref.dtype)

    @pl.when(j >= nused_ref[0])
    def _():
        y_ref[...] = jnp.zeros(y_ref.shape, y_ref.dtype)


def _expert_blocks(xs, plan, experts):
    layer, w1, w3, w2 = experts
    cap = xs.shape[0]
    d = D_MODEL
    weight = lambda shape: pl.BlockSpec((1, 1) + shape, lambda j, be, nu: (layer, be[j], 0, 0))
    return pl.pallas_call(
        _expert_block_kernel,
        out_shape=jax.ShapeDtypeStruct((cap, d), BF16),
        grid_spec=pltpu.PrefetchScalarGridSpec(
            num_scalar_prefetch=2,
            grid=(cap // EXPERT_BLOCK,),
            in_specs=[pl.BlockSpec((EXPERT_BLOCK, XS_COLS), lambda j, *_: (j, 0)),
                      weight((d, D_EXPERT)), weight((d, D_EXPERT)), weight((D_EXPERT, d))],
            out_specs=pl.BlockSpec((EXPERT_BLOCK, d), lambda j, *_: (j, 0)),
            scratch_shapes=[pltpu.VMEM((d, D_EXPERT), BF16), pltpu.VMEM((d, D_EXPERT), BF16),
                            pltpu.VMEM((D_EXPERT, d), BF16)]),
        compiler_params=_params("arbitrary"),
        name="moe_expert_blocks",
    )(plan['block_expert'], plan['blocks_used'], xs, w1, w3, w2)


def _combine_kernel(cnt_ref, lstart_ref, gstart_ref, ys_ref, pos_ref, h_ref, x_ref, gmod_ref, w1_ref, w3_ref, w2_ref,
                    o_ref, loc_ref, onehot_ref, sem):
    tile = pl.program_id(0) * pl.num_programs(1) + pl.program_id(1)
    kb, tl, d = x_ref.shape
    tm = kb * tl

    @pl.when(tile == 0)
    def _():
        loc_ref[...] = jnp.zeros(loc_ref.shape, loc_ref.dtype)

    _tile_segments(tile, cnt_ref, lstart_ref, gstart_ref, loc_ref, ys_ref, sem, to_global=False, wait=False)
    shared = _swiglu(h_ref[...].reshape(tm, d), w1_ref[...], w3_ref[...], w2_ref[...])
    _tile_segments(tile, cnt_ref, lstart_ref, gstart_ref, loc_ref, ys_ref, sem, to_global=False, wait=True)
    pos = pos_ref[...]
    col = lax.broadcasted_iota(I32, (tm, ONEHOT_CHUNK), 1)
    used_rows = lstart_ref[(tile + 1) * N_EXPERTS - 1] + cnt_ref[(tile + 1) * N_EXPERTS - 1]
    for c in range(LOCAL_ROWS // ONEHOT_CHUNK):
        cols = slice(c * ONEHOT_CHUNK, (c + 1) * ONEHOT_CHUNK)

        @pl.when(c * ONEHOT_CHUNK < used_rows)
        def _(c=c, cols=cols):
            onehot = jnp.zeros((tm, ONEHOT_CHUNK), F32)
            for k in range(TOP_K):
                onehot = jnp.where((pos[:, k:k + 1] - c * ONEHOT_CHUNK) == col, 1.0, onehot)
            onehot_ref[:, cols] = onehot.astype(BF16)

        @pl.when(c * ONEHOT_CHUNK >= used_rows)
        def _(cols=cols):
            onehot_ref[:, cols] = jnp.zeros((tm, ONEHOT_CHUNK), BF16)

    routed = jnp.dot(onehot_ref[...], loc_ref[...], preferred_element_type=F32)
    o_ref[...] = x_ref[...] + gmod_ref[...] * (routed + shared).reshape(kb, tl, d)


def _moe_combine(ys, pos_t, h, x, gate_mod, plan, sw1, sw3, sw2):
    b, l, d = x.shape
    kb, tl = _row_blocking(b, l)
    n_l = l // tl
    tok = pl.BlockSpec((kb, tl, d), lambda i, j, *_: (i, j, 0))
    per_batch = pl.BlockSpec((kb, 1, d), lambda i, j, *_: (i, 0, 0))
    return pl.pallas_call(
        _combine_kernel,
        out_shape=jax.ShapeDtypeStruct((b, l, d), F32),
        grid_spec=pltpu.PrefetchScalarGridSpec(
            num_scalar_prefetch=3,
            grid=(b // kb, n_l),
            in_specs=[pl.BlockSpec(memory_space=pl.ANY),
                      pl.BlockSpec((kb * tl, TOP_K), lambda i, j, *_: (i * n_l + j, 0)),
                      tok, tok, per_batch, _const_spec(sw1), _const_spec(sw3), _const_spec(sw2)],
            out_specs=tok,
            scratch_shapes=[pltpu.VMEM((LOCAL_ROWS, D_MODEL), BF16), pltpu.VMEM((kb * tl, LOCAL_ROWS), BF16),
                            pltpu.SemaphoreType.DMA]),
        compiler_params=_params("arbitrary", "arbitrary"),
        name="moe_combine",
    )(plan['seg_rows'], plan['local_start'], plan['global_start'], ys, pos_t, h, x, gate_mod, sw1, sw3, sw2)


def _dispatch_plan(counts, n_tok):
    n_tiles = counts.shape[0]
    seg = (counts + SEG_ALIGN - 1) // SEG_ALIGN * SEG_ALIGN
    local_start = jnp.cumsum(seg, axis=1) - seg
    region = jnp.sum(seg, axis=0)
    region_pad = (region + EXPERT_BLOCK - 1) // EXPERT_BLOCK * EXPERT_BLOCK
    region_end = jnp.cumsum(region_pad)
    region_beg = region_end - region_pad
    global_start = region_beg[None, :] + jnp.cumsum(seg, axis=0) - seg
    worst = TOP_K * n_tok + N_EXPERTS * (SEG_ALIGN - 1) * n_tiles + N_EXPERTS * (EXPERT_BLOCK - 1)
    cap = -(-worst // EXPERT_BLOCK) * EXPERT_BLOCK
    block_row = jnp.arange(cap // EXPERT_BLOCK, dtype=I32) * EXPERT_BLOCK
    block_expert = jnp.minimum(jnp.sum((region_end[None, :] <= block_row[:, None]).astype(I32), axis=1), N_EXPERTS - 1)
    flat = lambda a: a.reshape(-1).astype(I32)
    return dict(cap=cap, seg_rows=flat(seg), local_start=flat(local_start), global_start=flat(global_start),
                tail_rows=flat(region_pad - region), tail_start=flat(region_beg + region),
                block_expert=flat(block_expert), blocks_used=flat(region_end[-1:] // EXPERT_BLOCK))


def _moe(h, logits, x, gate_mod, router_b, experts, sw1, sw3, sw2):
    b, l, d = x.shape
    gate, pos, counts = _router(logits, router_b)
    plan = _dispatch_plan(counts[:, :, 0], b * l)
    xs = _moe_dispatch(h, pos, gate, plan)
    ys = _expert_blocks(xs, plan, experts)
    return _moe_combine(ys, pos.T, h, x, gate_mod, plan, sw1, sw3, sw2)


def _split_mod(m):
    m = m[:, None, :]
    return m[..., :D_MODEL], m[..., D_MODEL:2 * D_MODEL], m[..., 2 * D_MODEL:]


def _even_layer(x, mod_a, mod_b, past, w):
    b, l, _ = x.shape
    shift, scale, gate = _split_mod(mod_a)
    q3, k3, k, v, z, xbc, small = _inproj_call(
        _inproj_even_kernel, "inproj_even", x, w['norm_g0'], shift, scale, w['w_in'],
        [w['head_mean'], w['q_gain'], w['k_gain'], w['small_bias']],
        [FOX_HEADS * FOX_HD3, FOX_HEADS * FOX_HD3, FOX_DIM, FOX_DIM, SSD_INNER, SSD_CONV_DIM, LANES],
        [BF16, BF16, F32, F32, F32, F32, F32])
    if past['fox_k'] is None:
        cum, cumt = _fox_cumulative(small, None)
        fox = _fox_attention_prompt(q3, k3, v, cum, cumt)
    else:
        n_past = past['fox_k'].shape[1]
        past_logf = jnp.pad(past['fox_logf'].astype(F32), ((0, 0), (0, 0), (0, LANES - FOX_HEADS)))
        cum, cumt = _fox_cumulative(small, past_logf)
        fox = _fox_attention_decode(q3, k3, v, past['fox_k'].reshape(b, n_past, FOX_DIM),
                                    past['fox_v'].reshape(b, n_past, FOX_DIM), cum, cumt)
    y, conv_new, ssd_new = _ssd_mixer(xbc, z, small, past['ssd_conv'], past['ssd'], w['ssd_conv_w'], w['ssd_conv_b'],
                                      w['ssd_A_log'], w['ssd_D'], w['ssd_norm_g'])
    shift2, scale2, _ = _split_mod(mod_b)
    x, h, logits = _outproj_residual(fox, y, w['w_out'], x, gate, w['norm_g1'], shift2, scale2, w['router_w'],
                                     split_inputs=True)
    new = dict(fox_k=k.reshape(b, l, FOX_HEADS, FOX_HD), fox_v=v.reshape(b, l, FOX_HEADS, FOX_HD),
               fox_logf=small[..., :FOX_HEADS], ssd_conv=conv_new, ssd=ssd_new)
    return x, h, logits, new


def _odd_layer(x, mod_a, mod_b, past, w):
    shift, scale, gate = _split_mod(mod_a)
    rw, qkv, gz, small = _inproj_call(
        _inproj_odd_kernel, "inproj_odd", x, w['norm_g0'], shift, scale, w['w_in'],
        [w['small_bias'], w['small_alog']], [RWKV_PROJ, GDN_CONV_DIM, GDN_DIM, LANES], [F32, F32, F32, F32])
    o_rwkv, rwkv_new = _rwkv_mixer(rw, past['rwkv_shift'], past['rwkv'], w['p'])
    o_gdn, gconv_new, gdn_new = _gdn_mixer(qkv, gz, small, past['gdn_conv'], past['gdn'], w['p']['gdn_conv_w'],
                                           w['p']['gdn_norm_g'])
    shift2, scale2, _ = _split_mod(mod_b)
    x, h, logits = _outproj_residual(o_rwkv, o_gdn, w['w_out'], x, gate, w['norm_g1'], shift2, scale2, w['router_w'],
                                     split_inputs=False)
    new = dict(rwkv_shift=rw[:, -1], rwkv=rwkv_new, gdn_conv=gconv_new, gdn=gdn_new)
    return x, h, logits, new


def _run_trunk(x, mods, past_even, past_odd, layers):
    new_even, new_odd = [], []
    for i, w in enumerate(layers):
        j = i // 2
        if i % 2 == 0:
            past = {n: (None if a is None else a[j]) for n, a in past_even.items()}
            x, h, logits, st = _even_layer(x, mods[2 * i], mods[2 * i + 1], past, w)
            new_even.append(st)
        else:
            x, h, logits, st = _odd_layer(x, mods[2 * i], mods[2 * i + 1], {n: a[j] for n, a in past_odd.items()}, w)
            new_odd.append(st)
        x = _moe(h, logits, x, _split_mod(mods[2 * i + 1])[2], *w['moe'])
    stack = lambda lst: {n: jnp.stack([d[n] for d in lst]) for n in lst[0]}
    return x, stack(new_even), stack(new_odd)


def kernel(x_prompt, x_sample, cache_fox_k, cache_fox_v, cache_fox_logf, state_ssd_conv, state_ssd, state_rwkv_shift, state_rwkv, state_gdn_conv, state_gdn, c_prompt, c_sample, ada_w, ada_b, norm_g, even_w_in, even_w_out, fox_q_g, fox_k_g, fox_f_b, ssd_conv_w, ssd_conv_b, ssd_dt_bias, ssd_A_log, ssd_D, ssd_norm_g, odd_w_in, odd_w_out, rwkv_mu, rwkv_w0, rwkv_w2, rwkv_a0, rwkv_a2, rwkv_g2, rwkv_k_k, rwkv_k_a, rwkv_r_k, rwkv_ln_g, rwkv_ln_b, gdn_conv_w, gdn_A_log, gdn_dt_bias, gdn_norm_g, router_w, router_b, moe_w1, moe_w3, moe_w2, shared_w1, shared_w3, shared_w2):
    depth = ada_w.shape[0]
    ne, no = cache_fox_k.shape[0], state_rwkv.shape[0]
    bp, bs = x_prompt.shape[0], x_sample.shape[0]
    odd_params = dict(rwkv_mu=rwkv_mu, rwkv_w0=rwkv_w0, rwkv_w2=rwkv_w2, rwkv_a0=rwkv_a0, rwkv_a2=rwkv_a2, rwkv_g2=rwkv_g2,
                      rwkv_k_k=rwkv_k_k, rwkv_k_a=rwkv_k_a, rwkv_r_k=rwkv_r_k, rwkv_ln_g=rwkv_ln_g, rwkv_ln_b=rwkv_ln_b,
                      gdn_conv_w=gdn_conv_w, gdn_norm_g=gdn_norm_g)
    layers = []
    for i in range(depth):
        j = i // 2
        w = dict(norm_g0=norm_g[i, 0][None], norm_g1=norm_g[i, 1][None], router_w=router_w[i],
                 moe=(router_b[i], (i, moe_w1, moe_w3, moe_w2),
                      shared_w1[i].astype(BF16), shared_w3[i].astype(BF16), shared_w2[i].astype(BF16)))
        if i % 2 == 0:
            w.update(w_in=_pack_even_w_in(even_w_in[j]), w_out=even_w_out[j],
                     head_mean=_head_mean_matrix(FOX_DIM, FOX_HD),
                     q_gain=jnp.tile(fox_q_g[j], FOX_HEADS)[None], k_gain=jnp.tile(fox_k_g[j], FOX_HEADS)[None],
                     small_bias=_pad_lanes(fox_f_b[j], ssd_dt_bias[j]),
                     ssd_conv_w=ssd_conv_w[j], ssd_conv_b=ssd_conv_b[j], ssd_A_log=ssd_A_log[j], ssd_D=ssd_D[j],
                     ssd_norm_g=ssd_norm_g[j])
        else:
            w.update(w_in=_pack_odd_w_in(odd_w_in[j]), w_out=odd_w_out[j],
                     small_bias=_pad_lanes(jnp.zeros(GDN_HEADS), gdn_dt_bias[j]),
                     small_alog=_pad_lanes(jnp.zeros(GDN_HEADS), gdn_A_log[j]),
                     p={n: a[j] for n, a in odd_params.items()})
        layers.append(w)

    mods = _ada_modulation(jnp.concatenate([c_prompt, c_sample], axis=0), ada_w, ada_b)
    mods_p, mods_s = mods[:, :bp], mods[:, bp:]

    prompt_even = dict(fox_k=None, fox_v=None, fox_logf=None,
                       ssd_conv=jnp.zeros((ne, bp) + state_ssd_conv.shape[2:], F32),
                       ssd=jnp.zeros((ne, bp) + state_ssd.shape[2:], F32))
    prompt_odd = dict(rwkv_shift=jnp.zeros((no, bp) + state_rwkv_shift.shape[2:], F32),
                      rwkv=jnp.zeros((no, bp) + state_rwkv.shape[2:], F32),
                      gdn_conv=jnp.zeros((no, bp) + state_gdn_conv.shape[2:], F32),
                      gdn=jnp.zeros((no, bp) + state_gdn.shape[2:], F32))
    sample_even = dict(fox_k=cache_fox_k, fox_v=cache_fox_v, fox_logf=cache_fox_logf, ssd_conv=state_ssd_conv, ssd=state_ssd)
    sample_odd = dict(rwkv_shift=state_rwkv_shift, rwkv=state_rwkv, gdn_conv=state_gdn_conv, gdn=state_gdn)
    y_prompt, pe, po = _run_trunk(x_prompt, mods_p, prompt_even, prompt_odd, layers)
    y_sample, se, so = _run_trunk(x_sample, mods_s, sample_even, sample_odd, layers)
    return (y_prompt, y_sample,
            pe['fox_k'], pe['fox_v'], pe['fox_logf'], pe['ssd_conv'], pe['ssd'],
            po['rwkv_shift'], po['rwkv'], po['gdn_conv'], po['gdn'],
            se['fox_k'], se['fox_v'], se['fox_logf'], se['ssd_conv'], se['ssd'],
            so['rwkv_shift'], so['rwkv'], so['gdn_conv'], so['gdn'])
```

```python
import functools
import math

import jax
import jax.numpy as jnp
from jax import lax
from jax.experimental import pallas as pl
from jax.experimental.pallas import tpu as pltpu

F32 = jnp.float32
BF16 = jnp.bfloat16
I32 = jnp.int32

D_MODEL = 1024
CONV_W = 4
EPS = 1e-6

FOX_HEADS = 8
FOX_HD = 64
FOX_DIM = FOX_HEADS * FOX_HD
FOX_SCALE = 1.0 / math.sqrt(FOX_HD)

SSD_HEADS = 8
SSD_HD = 64
SSD_INNER = SSD_HEADS * SSD_HD
SSD_GROUPS = 2
SSD_N = 128
SSD_CONV_DIM = SSD_INNER + 2 * SSD_GROUPS * SSD_N

RWKV_HEADS = 8
RWKV_HD = 64
RWKV_DIM = RWKV_HEADS * RWKV_HD
RWKV_DECAY_LORA = 64
RWKV_A_LORA = 64
RWKV_GATE_LORA = 128
RWKV_PROJ = 3 * RWKV_DIM + RWKV_DECAY_LORA + RWKV_A_LORA + RWKV_GATE_LORA
RWKV_GN_EPS = 64e-5

GDN_HEADS = 4
GDN_HD = 128
GDN_DIM = GDN_HEADS * GDN_HD
GDN_CONV_DIM = 3 * GDN_DIM

N_EXPERTS = 64
TOP_K = 8
N_EXPERT_GROUPS = 8
TOPK_GROUPS = 4
EXPERTS_PER_GROUP = N_EXPERTS // N_EXPERT_GROUPS
D_EXPERT = 256
D_SHARED = 256
ROUTED_SCALE = 2.5

LANES = 128
VMEM_LIMIT_BYTES = 56 * 1024 * 1024


def _params(*semantics):
    return pltpu.CompilerParams(dimension_semantics=semantics, vmem_limit_bytes=VMEM_LIMIT_BYTES)


def _const_spec(a):
    return pl.BlockSpec(a.shape, lambda *_: (0,) * a.ndim, pipeline_mode=pl.Buffered(1))


def _mm(a, b):
    return jnp.dot(a.astype(BF16), b.astype(BF16), preferred_element_type=F32)


def _mm_nt(a, b):
    return lax.dot_general(a.astype(BF16), b.astype(BF16), (((1,), (1,)), ((), ())), preferred_element_type=F32)


def _mm_tn(a, b):
    return lax.dot_general(a.astype(BF16), b.astype(BF16), (((0,), (0,)), ((), ())), preferred_element_type=F32)


def _split_hi_lo(x):
    hi = x.astype(BF16)
    return hi, (x - hi.astype(F32)).astype(BF16)


def _dot3(dims, a, b, batch=((), ())):
    a_hi, a_lo = _split_hi_lo(a)
    b_hi, b_lo = _split_hi_lo(b)
    dot = lambda u, w: lax.dot_general(u, w, (dims, batch), preferred_element_type=F32)
    return dot(a_hi, b_hi) + (dot(a_hi, b_lo) + dot(a_lo, b_hi))


_mm3 = functools.partial(_dot3, ((1,), (0,)))
_mm3_nt = functools.partial(_dot3, ((1,), (1,)))
_mm3_tn = functools.partial(_dot3, ((0,), (0,)))

_HEAD_BATCH = ((0,), (0,))


def _bdot(dims, a, b):
    return lax.dot_general(a.astype(BF16), b.astype(BF16), (dims, _HEAD_BATCH), preferred_element_type=F32)


_bmm = functools.partial(_bdot, ((2,), (1,)))
_bmm_nt = functools.partial(_bdot, ((2,), (2,)))
_bmm_tn = functools.partial(_bdot, ((1,), (1,)))
_bmm3 = functools.partial(_dot3, ((2,), (1,)), batch=_HEAD_BATCH)


def _heads(x, hd):
    return jnp.stack([x[:, h * hd:(h + 1) * hd] for h in range(x.shape[1] // hd)], axis=0)


def _unheads(x):
    return jnp.concatenate([x[h] for h in range(x.shape[0])], axis=-1)


def _cat3_lhs(x):
    hi, lo = _split_hi_lo(x)
    return jnp.concatenate([hi, hi, lo], axis=-1)


def _split_hi_lo_outside(w):
    w = w.astype(F32)
    hi = lax.bitcast_convert_type(lax.bitcast_convert_type(w, jnp.uint32) & jnp.uint32(0xFFFF0000), F32)
    return hi.astype(BF16), (w - hi).astype(BF16)


def _cat3_rhs(w, axis=0):
    hi, lo = _split_hi_lo_outside(w)
    return jnp.concatenate([hi, lo, hi], axis=axis)


def _sigmoid(x):
    return 1.0 / (1.0 + jnp.exp(-x))


def _silu(x):
    return x * _sigmoid(x)


def _softplus(x):
    return jnp.maximum(x, 0.0) + jnp.log1p(jnp.exp(-jnp.abs(x)))


def _cumsum_rows(x):
    n = x.shape[0]
    row = lax.broadcasted_iota(I32, x.shape, 0)
    s = 1
    while s < n:
        x = x + jnp.where(row >= s, pltpu.roll(x, s, 0), 0.0)
        s *= 2
    return x


def _ada_norm(x, g, shift, scale):
    y = x * lax.rsqrt(jnp.mean(x * x, axis=-1, keepdims=True) + EPS)
    return (y * g) * (1.0 + scale) + shift


def _ada_kernel(c_ref, w_ref, b_ref, o_ref):
    o_ref[0] = _mm3(_silu(c_ref[...]), w_ref[0]) + b_ref[0]


def _ada_modulation(c_all, ada_w, ada_b):
    n = c_all.shape[0]
    n_mod = ada_w.shape[0] * ada_w.shape[1]
    w = ada_w.reshape(n_mod, D_MODEL, 3 * D_MODEL)
    b = ada_b.reshape(n_mod, 1, 3 * D_MODEL)
    return pl.pallas_call(
        _ada_kernel,
        out_shape=jax.ShapeDtypeStruct((n_mod, n, 3 * D_MODEL), F32),
        grid=(n_mod, 3),
        in_specs=[
            pl.BlockSpec((n, D_MODEL), lambda i, j: (0, 0)),
            pl.BlockSpec((1, D_MODEL, D_MODEL), lambda i, j: (i, 0, j)),
            pl.BlockSpec((1, 1, D_MODEL), lambda i, j: (i, 0, j)),
        ],
        out_specs=pl.BlockSpec((1, n, D_MODEL), lambda i, j: (i, 0, j)),
        compiler_params=_params("parallel", "parallel"),
        name="ada_modulation",
    )(c_all, w, b)


ROW_TILE = 256

EV_Q, EV_K, EV_V, EV_Z, EV_XBC, EV_SMALL = 0, FOX_DIM, 2 * FOX_DIM, 3 * FOX_DIM, 3 * FOX_DIM + SSD_INNER, 3 * FOX_DIM + SSD_INNER + SSD_CONV_DIM
EV_COLS = EV_SMALL + LANES
OD_RW, OD_QKV, OD_Z, OD_SMALL = 0, RWKV_PROJ, RWKV_PROJ + GDN_CONV_DIM, RWKV_PROJ + GDN_CONV_DIM + GDN_DIM
OD_COLS = OD_SMALL + LANES


def _row_blocking(b, l):
    if l >= ROW_TILE:
        assert l % ROW_TILE == 0
        return 1, ROW_TILE
    assert ROW_TILE % l == 0 and b % (ROW_TILE // l) == 0 and l % 8 == 0
    return ROW_TILE // l, l


def _normed_rows(x_ref, g_ref, sh_ref, sc_ref):
    x = x_ref[...]
    h = _ada_norm(x, g_ref[...], sh_ref[...], sc_ref[...])
    return h.reshape(x.shape[0] * x.shape[1], x.shape[2])


def _head_rms(y, head_mean, gain):
    hi, lo = _split_hi_lo(y * y)
    ms = jnp.dot(hi, head_mean, preferred_element_type=F32) + jnp.dot(lo, head_mean, preferred_element_type=F32)
    return y * lax.rsqrt(ms + EPS) * gain


FOX_HD3 = 3 * FOX_HD


def _head_cat3(x, rhs):
    hi, lo = _split_hi_lo(x)
    pieces = []
    for h in range(FOX_HEADS):
        hs = slice(h * FOX_HD, (h + 1) * FOX_HD)
        pieces += [hi[:, hs], lo[:, hs], hi[:, hs]] if rhs else [hi[:, hs], hi[:, hs], lo[:, hs]]
    return jnp.concatenate(pieces, axis=-1)


def _inproj_even_kernel(x_ref, g_ref, sh_ref, sc_ref, w_ref, hm_ref, qg_ref, kg_ref, bias_ref,
                        q3_ref, k3_ref, k_ref, v_ref, z_ref, xbc_ref, small_ref):
    h = _cat3_lhs(_normed_rows(x_ref, g_ref, sh_ref, sc_ref))
    shp3 = lambda r: (x_ref.shape[0], x_ref.shape[1], r.shape[2])
    cols = lambda lo, hi: jnp.dot(h, w_ref[:, lo:hi], preferred_element_type=F32)
    hm = hm_ref[...]
    q = _head_rms(cols(EV_Q, EV_K), hm, qg_ref[...]) * FOX_SCALE
    q3_ref[...] = _head_cat3(q, rhs=False).reshape(shp3(q3_ref))
    k = _head_rms(cols(EV_K, EV_V), hm, kg_ref[...])
    k_ref[...] = k.reshape(shp3(k_ref))
    k3_ref[...] = _head_cat3(k, rhs=True).reshape(shp3(k3_ref))
    v_ref[...] = cols(EV_V, EV_Z).reshape(shp3(v_ref))
    z_ref[...] = cols(EV_Z, EV_XBC).reshape(shp3(z_ref))
    xbc_ref[...] = cols(EV_XBC, EV_SMALL).reshape(shp3(xbc_ref))
    t = cols(EV_SMALL, EV_COLS) + bias_ref[...]
    lane = lax.broadcasted_iota(I32, t.shape, 1)
    small = jnp.where(lane < FOX_HEADS, -_softplus(-t), jnp.where(lane < FOX_HEADS + SSD_HEADS, _softplus(t), 0.0))
    small_ref[...] = small.reshape(shp3(small_ref))


def _inproj_odd_kernel(x_ref, g_ref, sh_ref, sc_ref, w_ref, bias_ref, alog_ref,
                       rw_ref, qkv_ref, z_ref, small_ref):
    h = _normed_rows(x_ref, g_ref, sh_ref, sc_ref).astype(BF16)
    shp3 = lambda r: (x_ref.shape[0], x_ref.shape[1], r.shape[2])
    rw_ref[...] = jnp.dot(h, w_ref[:, OD_RW:OD_QKV], preferred_element_type=F32).reshape(shp3(rw_ref))
    qkv_ref[...] = jnp.dot(h, w_ref[:, OD_QKV:OD_Z], preferred_element_type=F32).reshape(shp3(qkv_ref))
    z_ref[...] = jnp.dot(h, w_ref[:, OD_Z:OD_SMALL], preferred_element_type=F32).reshape(shp3(z_ref))
    t = jnp.dot(h, w_ref[:, OD_SMALL:OD_COLS], preferred_element_type=F32) + bias_ref[...]
    lane = lax.broadcasted_iota(I32, t.shape, 1)
    small = jnp.where(lane < GDN_HEADS, _sigmoid(t),
                      jnp.where(lane < 2 * GDN_HEADS, -jnp.exp(alog_ref[...]) * _softplus(t), 0.0))
    small_ref[...] = small.reshape(shp3(small_ref))


def _inproj_call(kernel_fn, name, x, g, shift, scale, w, extras, out_cols, out_dtypes):
    b, l, d = x.shape
    kb, tl = _row_blocking(b, l)
    tok = lambda c: pl.BlockSpec((kb, tl, c), lambda i, j: (i, j, 0))
    per_batch = pl.BlockSpec((kb, 1, d), lambda i, j: (i, 0, 0))
    const = _const_spec
    return pl.pallas_call(
        kernel_fn,
        out_shape=[jax.ShapeDtypeStruct((b, l, c), dt) for c, dt in zip(out_cols, out_dtypes)],
        grid=(b // kb, l // tl),
        in_specs=[tok(d), const(g), per_batch, per_batch, const(w)] + [const(e) for e in extras],
        out_specs=[tok(c) for c in out_cols],
        compiler_params=_params("parallel", "parallel"),
        name=name,
    )(x, g, shift, scale, w, *extras)


def _pad_lanes(*pieces):
    v = jnp.concatenate([p.reshape(-1).astype(F32) for p in pieces])
    return jnp.pad(v, (0, LANES - v.shape[0])).reshape(1, LANES)


def _head_mean_matrix(dim, hd):
    r = jnp.arange(dim) // hd
    return jnp.where(r[:, None] == r[None, :], 1.0 / hd, 0.0).astype(BF16)


def _pack_even_w_in(w_in):
    q, k, v, f, z, xbc, dt = jnp.split(w_in, (FOX_DIM, 2 * FOX_DIM, 3 * FOX_DIM, 3 * FOX_DIM + FOX_HEADS,
                                              3 * FOX_DIM + FOX_HEADS + SSD_INNER,
                                              3 * FOX_DIM + FOX_HEADS + SSD_INNER + SSD_CONV_DIM), axis=1)
    pad = jnp.zeros((w_in.shape[0], LANES - FOX_HEADS - SSD_HEADS), w_in.dtype)
    return _cat3_rhs(jnp.concatenate([q, k, v, z, xbc, f, dt, pad], axis=1))


def _pack_odd_w_in(w_in):
    main, small = w_in[:, :OD_SMALL], w_in[:, OD_SMALL:]
    pad = jnp.zeros((w_in.shape[0], LANES - 2 * GDN_HEADS), w_in.dtype)
    return jnp.concatenate([main, small, pad], axis=1).astype(BF16)


def _split3_bf16(x):
    c1 = x.astype(BF16)
    r1 = x - c1.astype(F32)
    c2 = r1.astype(BF16)
    c3 = (r1 - c2.astype(F32)).astype(BF16)
    return c1, c2, c3


def _rows_to_lanes(sel, x):
    nt = lambda a, b: lax.dot_general(a, b, (((1,), (1,)), ((), ())), preferred_element_type=F32)
    c1, c2, c3 = _split3_bf16(x)
    return (nt(sel, c1) + nt(sel, c2)) + nt(sel, c3)


def _fox_cum_kernel(*refs, n_past):
    if n_past:
        past_ref, small_ref, sel_ref, cum_ref, cumt_ref = refs
        seq = jnp.concatenate([past_ref[0], small_ref[0]], axis=0)
    else:
        small_ref, sel_ref, cum_ref, cumt_ref = refs
        seq = small_ref[0]
    cum = _cumsum_rows(seq)
    cum_ref[0] = cum[n_past:]
    cumt_ref[0] = _rows_to_lanes(sel_ref[...], cum)


def _fox_cumulative(small, past_logf):
    b, l, _ = small.shape
    n_past = 0 if past_logf is None else past_logf.shape[1]
    sel = jnp.eye(FOX_HEADS, LANES, dtype=BF16)
    ins, specs = [], []
    if n_past:
        ins.append(past_logf)
        specs.append(pl.BlockSpec((1, n_past, LANES), lambda i: (i, 0, 0)))
    ins += [small, sel]
    specs += [pl.BlockSpec((1, l, LANES), lambda i: (i, 0, 0)), pl.BlockSpec(sel.shape, lambda i: (0, 0))]
    return pl.pallas_call(
        functools.partial(_fox_cum_kernel, n_past=n_past),
        out_shape=[jax.ShapeDtypeStruct((b, l, LANES), F32), jax.ShapeDtypeStruct((b, FOX_HEADS, n_past + l), F32)],
        grid=(b,),
        in_specs=specs,
        out_specs=[pl.BlockSpec((1, l, LANES), lambda i: (i, 0, 0)),
                   pl.BlockSpec((1, FOX_HEADS, n_past + l), lambda i: (i, 0, 0))],
        compiler_params=_params("parallel"),
        name="fox_cumulative",
    )(*ins)


FOX_TILE = 512


def _fox_flash_kernel(q_ref, k_ref, v_ref, cq_ref, ck_ref, o_ref, m_sc, l_sc, acc_sc):
    qi, kj = pl.program_id(1), pl.program_id(2)

    @pl.when(kj == 0)
    def _():
        m_sc[...] = jnp.full(m_sc.shape, -jnp.inf, F32)
        l_sc[...] = jnp.zeros(l_sc.shape, F32)
        acc_sc[...] = jnp.zeros(acc_sc.shape, F32)

    def step(on_diagonal):
        q3, k3 = q_ref[0], k_ref[0]
        v_hi, v_lo = _split_hi_lo(v_ref[0])
        cq, ck = cq_ref[0], ck_ref[0]
        if on_diagonal:
            visible = (lax.broadcasted_iota(I32, (FOX_TILE, FOX_TILE), 1)
                       <= lax.broadcasted_iota(I32, (FOX_TILE, FOX_TILE), 0))
        for h in range(FOX_HEADS):
            hs = slice(h * FOX_HD, (h + 1) * FOX_HD)
            hs3 = slice(h * FOX_HD3, (h + 1) * FOX_HD3)
            s = _scores3(q3[:, hs3], k3[:, hs3]) + (cq[:, h:h + 1] - ck[h:h + 1, :])
            if on_diagonal:
                s = jnp.where(visible, s, -jnp.inf)
            m_prev = m_sc[h]
            m_new = jnp.maximum(m_prev, jnp.max(s, axis=-1, keepdims=True))
            p = jnp.exp(s - m_new)
            alpha = jnp.exp(m_prev - m_new)
            l_sc[h] = alpha * l_sc[h] + jnp.sum(p, axis=-1, keepdims=True)
            acc_sc[h] = alpha * acc_sc[h] + _pv3(p, v_hi[:, hs], v_lo[:, hs])
            m_sc[h] = m_new

    @pl.when(kj < qi)
    def _():
        step(on_diagonal=False)

    @pl.when(kj == qi)
    def _():
        step(on_diagonal=True)
        o_ref[0] = jnp.concatenate([acc_sc[h] / l_sc[h] for h in range(FOX_HEADS)], axis=-1)


def _scores3(q3, k3):
    return lax.dot_general(q3, k3, (((1,), (1,)), ((), ())), preferred_element_type=F32)


def _pv3(p, v_hi, v_lo):
    p_hi, p_lo = _split_hi_lo(p)
    dot = lambda a, b: jnp.dot(a, b, preferred_element_type=F32)
    return dot(p_hi, v_hi) + (dot(p_hi, v_lo) + dot(p_lo, v_hi))


def _fox_attention_prompt(q3, k3, v, cum, cumt):
    b, l, _ = v.shape
    n = l // FOX_TILE
    qspec = lambda c: pl.BlockSpec((1, FOX_TILE, c), lambda i, j, t: (i, j, 0))
    kspec = lambda c: pl.BlockSpec((1, FOX_TILE, c), lambda i, j, t: (i, jnp.minimum(t, j), 0))
    return pl.pallas_call(
        _fox_flash_kernel,
        out_shape=jax.ShapeDtypeStruct((b, l, FOX_DIM), F32),
        grid=(b, n, n),
        in_specs=[qspec(FOX_HEADS * FOX_HD3), kspec(FOX_HEADS * FOX_HD3), kspec(FOX_DIM), qspec(LANES),
                  pl.BlockSpec((1, FOX_HEADS, FOX_TILE), lambda i, j, t: (i, 0, jnp.minimum(t, j)))],
        out_specs=qspec(FOX_DIM),
        scratch_shapes=[pltpu.VMEM((FOX_HEADS, FOX_TILE, 1), F32), pltpu.VMEM((FOX_HEADS, FOX_TILE, 1), F32),
                        pltpu.VMEM((FOX_HEADS, FOX_TILE, FOX_HD), F32)],
        compiler_params=_params("parallel", "parallel", "arbitrary"),
        name="fox_flash",
    )(q3, k3, v, cum, cumt)


def _fox_decode_kernel(q_ref, kp_ref, vp_ref, kn_ref, vn_ref, cq_ref, ck_ref, o_ref):
    n_past, l = kp_ref.shape[1], q_ref.shape[1]
    q3, kn3 = q_ref[0], kn_ref[0]
    kp3 = _head_cat3(kp_ref[0], rhs=True)
    vp_hi, vp_lo = _split_hi_lo(vp_ref[0])
    vn_hi, vn_lo = _split_hi_lo(vn_ref[0])
    cq, ck = cq_ref[0], ck_ref[0]
    row = lax.broadcasted_iota(I32, (l, l), 0)
    col = lax.broadcasted_iota(I32, (l, l), 1)
    outs = []
    for h in range(FOX_HEADS):
        hs = slice(h * FOX_HD, (h + 1) * FOX_HD)
        hs3 = slice(h * FOX_HD3, (h + 1) * FOX_HD3)
        cqh = cq[:, h:h + 1]
        s_past = _scores3(q3[:, hs3], kp3[:, hs3]) + (cqh - ck[h:h + 1, :n_past])
        s_new = _scores3(q3[:, hs3], kn3[:, hs3]) + (cqh - ck[h:h + 1, n_past:])
        s_new = jnp.where(col <= row, s_new, -jnp.inf)
        m = jnp.maximum(jnp.max(s_past, axis=-1, keepdims=True), jnp.max(s_new, axis=-1, keepdims=True))
        p_past, p_new = jnp.exp(s_past - m), jnp.exp(s_new - m)
        denom = jnp.sum(p_past, axis=-1, keepdims=True) + jnp.sum(p_new, axis=-1, keepdims=True)
        outs.append((_pv3(p_past, vp_hi[:, hs], vp_lo[:, hs]) + _pv3(p_new, vn_hi[:, hs], vn_lo[:, hs])) / denom)
    o_ref[0] = jnp.concatenate(outs, axis=-1)


def _fox_attention_decode(q3, k3, v, k_past, v_past, cum, cumt):
    b, l, _ = v.shape
    n_past = k_past.shape[1]
    new = lambda c: pl.BlockSpec((1, l, c), lambda i: (i, 0, 0))
    past = pl.BlockSpec((1, n_past, FOX_DIM), lambda i: (i, 0, 0))
    return pl.pallas_call(
        _fox_decode_kernel,
        out_shape=jax.ShapeDtypeStruct((b, l, FOX_DIM), F32),
        grid=(b,),
        in_specs=[new(FOX_HEADS * FOX_HD3), past, past, new(FOX_HEADS * FOX_HD3), new(FOX_DIM), new(LANES),
                  pl.BlockSpec((1, FOX_HEADS, n_past + l), lambda i: (i, 0, 0))],
        out_specs=new(FOX_DIM),
        compiler_params=_params("parallel"),
        name="fox_decode",
    )(q3, k_past, v_past, k3, v, cum, cumt)


CONV_TAIL = 8


def _causal_conv(x, tail, w):
    q = x.shape[0]
    row = lax.broadcasted_iota(I32, (CONV_TAIL, x.shape[1]), 0)
    acc = x * w[CONV_W - 1:CONV_W]
    for s in range(1, CONV_W):
        xr = pltpu.roll(x, s, 0)
        head = jnp.where(row < s, pltpu.roll(tail, s, 0), xr[:CONV_TAIL])
        shifted = head if q == CONV_TAIL else jnp.concatenate([head, xr[CONV_TAIL:]], axis=0)
        acc = acc + shifted * w[CONV_W - 1 - s:CONV_W - s]
    return acc


def _conv_tail_from_state(state):
    return jnp.pad(state, ((0, 0), (CONV_TAIL - (CONV_W - 1), 0), (0, 0)))


SSD_CHUNK = 128
SSD_DT_LANE = FOX_HEADS


def _ssd_kernel(xbc_ref, z_ref, small_ref, tail0_ref, h0_ref, cw_ref, cb_ref, alog_ref, drep_ref, ng_ref, sel_ref,
                y_ref, tail_ref, hout_ref, tail_sc, h_sc):
    c = pl.program_id(1)

    @pl.when(c == 0)
    def _():
        tail_sc[...] = tail0_ref[0]
        h_sc[...] = h0_ref[0]

    x = xbc_ref[0]
    q = x.shape[0]
    act = _silu(_causal_conv(x, tail_sc[...], cw_ref[...]) + cb_ref[...])
    tail_sc[...] = x[q - CONV_TAIL:]
    xs = act[:, :SSD_INNER]
    bm = act[:, SSD_INNER:SSD_INNER + SSD_GROUPS * SSD_N]
    cm = act[:, SSD_INNER + SSD_GROUPS * SSD_N:]

    small = small_ref[0]
    lane = lax.broadcasted_iota(I32, small.shape, 1)
    is_dt = jnp.logical_and(lane >= SSD_DT_LANE, lane < SSD_DT_LANE + SSD_HEADS)
    da = jnp.where(is_dt, small * -jnp.exp(alog_ref[...]), 0.0)
    acum = _cumsum_rows(da)
    acum_t = _rows_to_lanes(sel_ref[...], acum)
    trow = lax.broadcasted_iota(I32, (q, q), 0)
    tcol = lax.broadcasted_iota(I32, (q, q), 1)
    heads_per_group = SSD_HEADS // SSD_GROUPS
    ys = []
    for g in range(SSD_GROUPS):
        bg = bm[:, g * SSD_N:(g + 1) * SSD_N]
        cg = cm[:, g * SSD_N:(g + 1) * SSD_N]
        cb = _mm3_nt(cg, bg)
        for h in range(g * heads_per_group, (g + 1) * heads_per_group):
            a_col = acum[:, SSD_DT_LANE + h:SSD_DT_LANE + h + 1]
            dt_col = small[:, SSD_DT_LANE + h:SSD_DT_LANE + h + 1]
            a_last = a_col[q - 1:q]
            seg = jnp.exp(jnp.where(trow >= tcol, a_col - acum_t[h:h + 1, :], -jnp.inf))
            xdt = xs[:, h * SSD_HD:(h + 1) * SSD_HD] * dt_col
            state = h_sc[h]
            ys.append(_mm3(cb * seg, xdt) + jnp.exp(a_col) * _mm3_nt(cg, state))
            h_sc[h] = state * jnp.exp(a_last) + _mm3_tn(xdt * jnp.exp(a_last - a_col), bg)
    y = jnp.concatenate(ys, axis=-1) + drep_ref[...] * xs
    y = y * _silu(z_ref[0])
    gw = SSD_INNER // SSD_GROUPS
    normed = []
    for g in range(SSD_GROUPS):
        yg = y[:, g * gw:(g + 1) * gw]
        normed.append(yg * lax.rsqrt(jnp.mean(yg * yg, axis=-1, keepdims=True) + EPS))
    y_ref[0] = (jnp.concatenate(normed, axis=-1) * ng_ref[...]).astype(y_ref.dtype)

    @pl.when(c == pl.num_programs(1) - 1)
    def _():
        tail_ref[0] = tail_sc[...]
        hout_ref[0] = h_sc[...]


def _ssd_mixer(xbc, z, small, conv_state, h0, conv_w, conv_b, a_log, d_skip, norm_g):
    b, l, _ = xbc.shape
    q = min(SSD_CHUNK, l)
    tail0 = _conv_tail_from_state(conv_state)
    alog = _pad_lanes(jnp.zeros(SSD_DT_LANE), a_log)
    drep = jnp.repeat(d_skip.astype(F32), SSD_HD)[None]
    sel = jnp.eye(SSD_HEADS, LANES, k=SSD_DT_LANE, dtype=BF16)
    tok = lambda c: pl.BlockSpec((1, q, c), lambda i, j: (i, j, 0))
    per_b = lambda a: pl.BlockSpec((1,) + a.shape[1:], lambda i, j: (i,) + (0,) * (a.ndim - 1))
    const = _const_spec
    consts = [conv_w, conv_b[None], alog, drep, norm_g[None], sel]
    y, tail, h_new = pl.pallas_call(
        _ssd_kernel,
        out_shape=[jax.ShapeDtypeStruct((b, l, SSD_INNER), F32),
                   jax.ShapeDtypeStruct((b, CONV_TAIL, SSD_CONV_DIM), F32),
                   jax.ShapeDtypeStruct(h0.shape, F32)],
        grid=(b, l // q),
        in_specs=[tok(SSD_CONV_DIM), tok(SSD_INNER), tok(LANES), per_b(tail0), per_b(h0)] + [const(a) for a in consts],
        out_specs=[tok(SSD_INNER), per_b(tail0), per_b(h0)],
        scratch_shapes=[pltpu.VMEM((CONV_TAIL, SSD_CONV_DIM), F32), pltpu.VMEM(h0.shape[1:], F32)],
        compiler_params=_params("parallel", "arbitrary"),
        name="ssd_mixer",
    )(xbc, z, small, tail0, h0, *consts)
    return y, tail[:, CONV_TAIL - (CONV_W - 1):], h_new


REC_CHUNK = 64


INV_BASE = 8


def _unit_lower_inverse(n):
    c = n.shape[-1]
    mm = _mm3 if n.ndim == 2 else _bmm3
    merge_mm = _mm if n.ndim == 2 else _bmm
    row = lax.broadcasted_iota(I32, (c, c), 0)
    col = lax.broadcasted_iota(I32, (c, c), 1)
    shift = INV_BASE.bit_length() - 1
    same = lax.shift_right_logical(row, shift) == lax.shift_right_logical(col, shift)
    diag = jnp.where(same, n, 0.0)
    inv = jnp.where(row == col, 1.0, 0.0) + diag
    power = diag
    span = 2
    while span < INV_BASE:
        power = mm(power, power)
        inv = inv + mm(inv, power)
        span *= 2
    size = INV_BASE
    while size < c:
        shift += 1
        size *= 2
        merged = lax.shift_right_logical(row, shift) == lax.shift_right_logical(col, shift)
        off = jnp.where(jnp.logical_and(merged, jnp.logical_not(same)), n, 0.0)
        inv = inv + merge_mm(merge_mm(inv, off), inv)
        same = merged
    return inv


def _shift_rows_by_one(x, tail):
    q = x.shape[0]
    row = lax.broadcasted_iota(I32, (CONV_TAIL, x.shape[1]), 0)
    xr = pltpu.roll(x, 1, 0)
    head = jnp.where(row < 1, pltpu.roll(tail, 1, 0), xr[:CONV_TAIL])
    return head if q == CONV_TAIL else jnp.concatenate([head, xr[CONV_TAIL:]], axis=0)


RW_R, RW_K, RW_V, RW_LORA, RW_GATE = 0, RWKV_DIM, 2 * RWKV_DIM, 3 * RWKV_DIM, 3 * RWKV_DIM + RWKV_DECAY_LORA + RWKV_A_LORA


def _rwkv_kernel(rw_ref, tail0_ref, s0_ref, mu_ref, w0_ref, w2_ref, a0_ref, a2_ref, g2_ref, kk_ref, ka_ref, rk_ref,
                 lng_ref, lnb_ref, hsum_ref, o_ref, sout_ref, tail_sc, s_sc):
    c = pl.program_id(1)

    @pl.when(c == 0)
    def _():
        tail_sc[...] = tail0_ref[0]
        s_sc[...] = s0_ref[0]

    x = rw_ref[0]
    q = x.shape[0]
    mixed = x + (_shift_rows_by_one(x, tail_sc[...]) - x) * mu_ref[...]
    tail_sc[...] = x[q - CONV_TAIL:]
    r = mixed[:, RW_R:RW_K]
    k = mixed[:, RW_K:RW_V]
    v = mixed[:, RW_V:RW_LORA]
    lora = mixed[:, RW_LORA:RW_GATE]
    w_log = -_softplus(-(w0_ref[...] + _mm(jnp.tanh(lora), w2_ref[...]))) - 0.5
    logw = -jnp.exp(w_log)
    icl = _sigmoid(a0_ref[...] + _mm(lora, a2_ref[...]))
    out_gate = _mm(_sigmoid(mixed[:, RW_GATE:]), g2_ref[...])
    kk = k * kk_ref[...]
    kk = kk * lax.rsqrt(_mm(kk * kk, hsum_ref[...]) + EPS)
    k2 = k * (1.0 + (icl - 1.0) * ka_ref[...])
    cum = _cumsum_rows(logw)
    w_run = jnp.exp(cum)
    w_inv = jnp.exp(-cum)
    rt = r * w_run
    at = -kk * jnp.exp(cum - logw)
    bt = kk * icl * w_inv
    kt = k2 * w_inv
    trow = lax.broadcasted_iota(I32, (q, q), 0)
    tcol = lax.broadcasted_iota(I32, (q, q), 1)
    strict, incl = trow > tcol, trow >= tcol
    hd = RWKV_HD
    at, rt, bt, kt, vh, rh, k2h = (_heads(t, hd) for t in (at, rt, bt, kt, v, r, k2))
    a_ab = jnp.where(strict, _bmm_nt(at, bt), 0.0)
    a_ak = jnp.where(strict, _bmm_nt(at, kt), 0.0)
    a_rb = jnp.where(incl, _bmm_nt(rt, bt), 0.0)
    a_rk = jnp.where(incl, _bmm_nt(rt, kt), 0.0)
    s0 = s_sc[...]
    u = _bmm(_unit_lower_inverse(a_ab), _bmm_nt(at, s0) + _bmm(a_ak, vh))
    o = _bmm_nt(rt, s0) + (_bmm(a_rb, u) + _bmm(a_rk, vh))
    s_sc[...] = (s0 + (_bmm_tn(u, bt) + _bmm_tn(vh, kt))) * _heads(w_run[q - 1:q], hd)
    mean = jnp.mean(o, axis=-1, keepdims=True)
    var = jnp.mean(jnp.square(o - mean), axis=-1, keepdims=True)
    o = (o - mean) * lax.rsqrt(var + RWKV_GN_EPS) * _heads(lng_ref[...], hd) + _heads(lnb_ref[...], hd)
    bonus = jnp.sum(rh * k2h * _heads(rk_ref[...], hd), axis=-1, keepdims=True) * vh
    o_ref[0] = (_unheads(o + bonus) * out_gate).astype(o_ref.dtype)

    @pl.when(c == pl.num_programs(1) - 1)
    def _():
        sout_ref[0] = s_sc[...]


def _rwkv_mixer(rw, shift_state, s0, p):
    b, l, _ = rw.shape
    q = min(REC_CHUNK, l)
    tail0 = jnp.pad(shift_state[:, None, :], ((0, 0), (CONV_TAIL - 1, 0), (0, 0)))
    zeros = jnp.zeros((RWKV_DECAY_LORA, RWKV_DIM), F32)
    consts = [p['rwkv_mu'][None], p['rwkv_w0'][None],
              jnp.concatenate([p['rwkv_w2'], zeros], axis=0).astype(BF16), p['rwkv_a0'][None],
              jnp.concatenate([zeros, p['rwkv_a2']], axis=0).astype(BF16), p['rwkv_g2'].astype(BF16),
              p['rwkv_k_k'][None], p['rwkv_k_a'][None], p['rwkv_r_k'].reshape(1, RWKV_DIM),
              p['rwkv_ln_g'][None], p['rwkv_ln_b'][None], _head_mean_matrix(RWKV_DIM, RWKV_HD) * RWKV_HD]
    tok = lambda c: pl.BlockSpec((1, q, c), lambda i, j: (i, j, 0))
    per_b = lambda a: pl.BlockSpec((1,) + a.shape[1:], lambda i, j: (i,) + (0,) * (a.ndim - 1))
    const = _const_spec
    return pl.pallas_call(
        _rwkv_kernel,
        out_shape=[jax.ShapeDtypeStruct((b, l, RWKV_DIM), BF16), jax.ShapeDtypeStruct(s0.shape, F32)],
        grid=(b, l // q),
        in_specs=[tok(RWKV_PROJ), per_b(tail0), per_b(s0)] + [const(a) for a in consts],
        out_specs=[tok(RWKV_DIM), per_b(s0)],
        scratch_shapes=[pltpu.VMEM((CONV_TAIL, RWKV_PROJ), F32), pltpu.VMEM(s0.shape[1:], F32)],
        compiler_params=_params("parallel", "arbitrary"),
        name="rwkv_mixer",
    )(rw, tail0, s0, *consts)


GDN_BETA_LANE, GDN_G_LANE = 0, GDN_HEADS


def _l2norm(x):
    return x * lax.rsqrt(jnp.sum(x * x, axis=-1, keepdims=True) + EPS)


def _gdn_kernel(qkv_ref, z_ref, small_ref, tail0_ref, s0_ref, cw_ref, ng_ref, sel_ref,
                o_ref, tail_ref, sout_ref, tail_sc, s_sc):
    c = pl.program_id(1)

    @pl.when(c == 0)
    def _():
        tail_sc[...] = tail0_ref[0]
        s_sc[...] = s0_ref[0]

    x = qkv_ref[0]
    q = x.shape[0]
    act = _silu(_causal_conv(x, tail_sc[...], cw_ref[...]))
    tail_sc[...] = x[q - CONV_TAIL:]
    small = small_ref[0]
    z = z_ref[0]
    lane = lax.broadcasted_iota(I32, small.shape, 1)
    is_g = jnp.logical_and(lane >= GDN_G_LANE, lane < GDN_G_LANE + GDN_HEADS)
    gam = _cumsum_rows(jnp.where(is_g, small, 0.0))
    gam_t = _rows_to_lanes(sel_ref[...], gam)
    trow = lax.broadcasted_iota(I32, (q, q), 0)
    tcol = lax.broadcasted_iota(I32, (q, q), 1)
    heads = range(GDN_HEADS)
    qh = _l2norm(_heads(act[:, :GDN_DIM], GDN_HD)) * (GDN_HD ** -0.5)
    kh = _l2norm(_heads(act[:, GDN_DIM:2 * GDN_DIM], GDN_HD))
    vh = _heads(act[:, 2 * GDN_DIM:], GDN_HD)
    beta = jnp.stack([small[:, GDN_BETA_LANE + h:GDN_BETA_LANE + h + 1] for h in heads])
    g_col = jnp.stack([gam[:, GDN_G_LANE + h:GDN_G_LANE + h + 1] for h in heads])
    g_row = jnp.stack([gam_t[h:h + 1, :] for h in heads])
    g_last = g_col[:, q - 1:q]
    diff = g_col - g_row
    a_mat = _bmm_nt(kh, kh) * jnp.exp(jnp.where(trow > tcol, diff, -jnp.inf)) * beta
    eg = jnp.exp(g_col)
    uw = _bmm(_unit_lower_inverse(-a_mat), jnp.concatenate([vh * beta, kh * (beta * eg)], axis=2))
    qk = _bmm_nt(qh, kh) * jnp.exp(jnp.where(trow >= tcol, diff, -jnp.inf))
    state = s_sc[...]
    nu = uw[:, :, :GDN_HD] - _bmm(uw[:, :, GDN_HD:], state)
    o = _bmm(qh * eg, state) + _bmm(qk, nu)
    s_sc[...] = state * jnp.exp(g_last) + _bmm_tn(kh * jnp.exp(g_last - g_col), nu)
    o = o * lax.rsqrt(jnp.mean(o * o, axis=-1, keepdims=True) + EPS) * ng_ref[...]
    o_ref[0] = (_unheads(o) * _silu(z)).astype(o_ref.dtype)

    @pl.when(c == pl.num_programs(1) - 1)
    def _():
        tail_ref[0] = tail_sc[...]
        sout_ref[0] = s_sc[...]


def _gdn_mixer(qkv, z, small, conv_state, s0, conv_w, norm_g):
    b, l, _ = qkv.shape
    q = min(REC_CHUNK, l)
    tail0 = _conv_tail_from_state(conv_state)
    sel = jnp.eye(GDN_HEADS, LANES, k=GDN_G_LANE, dtype=BF16)
    consts = [conv_w, norm_g[None], sel]
    tok = lambda c: pl.BlockSpec((1, q, c), lambda i, j: (i, j, 0))
    per_b = lambda a: pl.BlockSpec((1,) + a.shape[1:], lambda i, j: (i,) + (0,) * (a.ndim - 1))
    const = _const_spec
    o, tail, s_new = pl.pallas_call(
        _gdn_kernel,
        out_shape=[jax.ShapeDtypeStruct((b, l, GDN_DIM), BF16),
                   jax.ShapeDtypeStruct((b, CONV_TAIL, GDN_CONV_DIM), F32),
                   jax.ShapeDtypeStruct(s0.shape, F32)],
        grid=(b, l // q),
        in_specs=[tok(GDN_CONV_DIM), tok(GDN_DIM), tok(LANES), per_b(tail0), per_b(s0)] + [const(a) for a in consts],
        out_specs=[tok(GDN_DIM), per_b(tail0), per_b(s0)],
        scratch_shapes=[pltpu.VMEM((CONV_TAIL, GDN_CONV_DIM), F32), pltpu.VMEM(s0.shape[1:], F32)],
        compiler_params=_params("parallel", "arbitrary"),
        name="gdn_mixer",
    )(qkv, z, small, tail0, s0, *consts)
    return o, tail[:, CONV_TAIL - (CONV_W - 1):], s_new


def _outproj_kernel(a_ref, b_ref, w_ref, x_ref, gate_ref, g_ref, sh_ref, sc_ref, rwt_ref,
                    xo_ref, h_ref, logit_ref, *, split_inputs):
    kb, tl, d = x_ref.shape
    rows = kb * tl
    a = a_ref[...].reshape(rows, a_ref.shape[2])
    b = b_ref[...].reshape(rows, b_ref.shape[2])
    lhs = jnp.concatenate([_cat3_lhs(a), _cat3_lhs(b)] if split_inputs else [a.astype(BF16), b.astype(BF16)], axis=-1)
    mix = jnp.dot(lhs, w_ref[...], preferred_element_type=F32)
    x_new = x_ref[...] + gate_ref[...] * mix.reshape(kb, tl, d)
    xo_ref[...] = x_new
    h = _ada_norm(x_new, g_ref[...], sh_ref[...], sc_ref[...])
    h_ref[...] = h.astype(h_ref.dtype)
    logit_ref[...] = lax.dot_general(rwt_ref[...], _cat3_lhs(h.reshape(rows, d)), (((1,), (1,)), ((), ())),
                                     preferred_element_type=F32)


def _outproj_residual(a, b_half, w_out, x, gate, g2, shift2, scale2, router_w, split_inputs):
    b, l, d = x.shape
    kb, tl = _row_blocking(b, l)
    half = a.shape[2]
    pack = _cat3_rhs if split_inputs else (lambda w: w.astype(BF16))
    w = jnp.concatenate([pack(w_out[:half]), pack(w_out[half:])], axis=0)
    rwt = _cat3_rhs(router_w.T, axis=1)
    n_l = l // tl
    tok = lambda c: pl.BlockSpec((kb, tl, c), lambda i, j: (i, j, 0))
    per_batch = pl.BlockSpec((kb, 1, d), lambda i, j: (i, 0, 0))
    const = _const_spec
    return pl.pallas_call(
        functools.partial(_outproj_kernel, split_inputs=split_inputs),
        out_shape=[jax.ShapeDtypeStruct((b, l, d), F32), jax.ShapeDtypeStruct((b, l, d), BF16),
                   jax.ShapeDtypeStruct((N_EXPERTS, b * l), F32)],
        grid=(b // kb, n_l),
        in_specs=[tok(half), tok(b_half.shape[2]), const(w), tok(d), per_batch, const(g2), per_batch, per_batch, const(rwt)],
        out_specs=[tok(d), tok(d), pl.BlockSpec((N_EXPERTS, kb * tl), lambda i, j: (0, i * n_l + j))],
        compiler_params=_params("parallel", "parallel"),
        name="outproj_residual",
    )(a, b_half, w, x, gate, g2, shift2, scale2, rwt)


def _beats(a, b, a_first):
    return jnp.where(a > b, 1, jnp.where(a == b, a_first, 0))


def _router_kernel(logit_ref, bias_ref, gate_ref, pos_ref, count_ref):
    tm = logit_ref.shape[1]
    scores = _sigmoid(logit_ref[...])
    sel = scores + bias_ref[...]
    ng, eg = N_EXPERT_GROUPS, EXPERTS_PER_GROUP
    sub = lax.broadcasted_iota(I32, (eg, tm), 0)
    groups, grp = [], []
    for g in range(ng):
        xg = sel[g * eg:(g + 1) * eg]
        m1 = jnp.max(xg, axis=0, keepdims=True)
        first = jnp.min(jnp.where(xg == m1, sub, eg), axis=0, keepdims=True)
        m2 = jnp.max(jnp.where(sub == first, -jnp.inf, xg), axis=0, keepdims=True)
        groups.append(xg)
        grp.append(m1 + m2)
    masked = []
    for g in range(ng):
        rank = sum(_beats(grp[o], grp[g], 1 if o < g else 0) for o in range(ng) if o != g)
        masked.append(jnp.where(rank < TOPK_GROUPS, groups[g], -jnp.inf))
    selm = jnp.concatenate(masked, axis=0)
    erow = lax.broadcasted_iota(I32, (N_EXPERTS, tm), 0)
    rank = jnp.zeros((N_EXPERTS, tm), I32)
    for o in range(N_EXPERTS):
        rank = rank + _beats(selm[o:o + 1], selm, jnp.where(erow > o, 1, 0))
    gate = jnp.concatenate(
        [jnp.sum(jnp.where(rank == k, scores, 0.0), axis=0, keepdims=True) for k in range(TOP_K)], axis=0)
    gate_ref[...] = gate / jnp.sum(gate, axis=0, keepdims=True) * ROUTED_SCALE
    chosen = jnp.where(rank < TOP_K, 1.0, 0.0)
    earlier = jnp.where(lax.broadcasted_iota(I32, (tm, tm), 0) < lax.broadcasted_iota(I32, (tm, tm), 1), 1.0, 0.0)
    before = _mm(chosen, earlier)
    count = jnp.sum(chosen, axis=1, keepdims=True)
    padded = jnp.ceil(count * (1.0 / SEG_ALIGN)) * SEG_ALIGN
    lower = jnp.where(lax.broadcasted_iota(I32, (N_EXPERTS, N_EXPERTS), 1) < lax.broadcasted_iota(I32, (N_EXPERTS, N_EXPERTS), 0), 1.0, 0.0)
    seg_start = _mm(lower, jnp.broadcast_to(padded, (N_EXPERTS, LANES)))[:, :1]
    slot = seg_start + before
    pos_ref[...] = jnp.concatenate(
        [jnp.sum(jnp.where(rank == k, slot, 0.0), axis=0, keepdims=True) for k in range(TOP_K)], axis=0).astype(I32)
    count_ref[0] = jnp.broadcast_to(count, (N_EXPERTS, LANES)).astype(I32)


def _router(logits, router_b):
    t = logits.shape[1]
    n_tiles = t // ROW_TILE
    bias = router_b.astype(F32)[:, None]
    tile = lambda rows: pl.BlockSpec((rows, ROW_TILE), lambda i: (0, i))
    return pl.pallas_call(
        _router_kernel,
        out_shape=[jax.ShapeDtypeStruct((TOP_K, t), F32), jax.ShapeDtypeStruct((TOP_K, t), I32),
                   jax.ShapeDtypeStruct((n_tiles, N_EXPERTS, LANES), I32)],
        grid=(n_tiles,),
        in_specs=[tile(N_EXPERTS), _const_spec(bias)],
        out_specs=[tile(TOP_K), tile(TOP_K), pl.BlockSpec((1, N_EXPERTS, LANES), lambda i: (i, 0, 0))],
        compiler_params=_params("parallel"),
        name="moe_router",
    )(logits, bias)


SEG_ALIGN = 16
SEG_CHUNK = 32
EXPERT_BLOCK = 512
ONEHOT_CHUNK = 256
XS_COLS = D_MODEL + LANES
LOCAL_ROWS = -(-(TOP_K * ROW_TILE + N_EXPERTS * (SEG_ALIGN - 1)) // ONEHOT_CHUNK) * ONEHOT_CHUNK


def _segment_dmas(n_rows, src_ref, src_row, dst_ref, dst_row, sem, wait):
    def dma(rows, s, d):
        cp = pltpu.make_async_copy(src_ref.at[pl.ds(pl.multiple_of(s, SEG_ALIGN), rows)],
                                   dst_ref.at[pl.ds(pl.multiple_of(d, SEG_ALIGN), rows)], sem)
        cp.wait() if wait else cp.start()

    n_full = lax.shift_right_logical(n_rows, SEG_CHUNK.bit_length() - 1)

    def bulk(q, carry):
        dma(SEG_CHUNK, src_row + q * SEG_CHUNK, dst_row + q * SEG_CHUNK)
        return carry

    lax.fori_loop(0, n_full, bulk, 0)
    off = n_full * SEG_CHUNK
    size = SEG_CHUNK // 2
    while size >= SEG_ALIGN:
        has = (n_rows & size) != 0

        @pl.when(has)
        def _(size=size, off=off):
            dma(size, src_row + off, dst_row + off)

        off = off + jnp.where(has, size, 0)
        size //= 2


def _tile_segments(tile, cnt_ref, lstart_ref, gstart_ref, local_ref, global_ref, sem, to_global, wait):
    base = tile * N_EXPERTS

    def seg(e, carry):
        n, lo, go = cnt_ref[base + e], lstart_ref[base + e], gstart_ref[base + e]
        if to_global:
            _segment_dmas(n, local_ref, lo, global_ref, go, sem, wait)
        else:
            _segment_dmas(n, global_ref, go, local_ref, lo, sem, wait)
        return carry

    lax.fori_loop(0, N_EXPERTS, seg, 0)


def _zero_fill(tail_rows_ref, tail_start_ref, used_ref, zero_ref, xs_ref, sem, wait):
    def tail(e, carry):
        _segment_dmas(tail_rows_ref[e], zero_ref, 0, xs_ref, tail_start_ref[e], sem, wait)
        return carry

    lax.fori_loop(0, N_EXPERTS, tail, 0)

    def block(j, carry):
        cp = pltpu.make_async_copy(zero_ref, xs_ref.at[pl.ds(pl.multiple_of(j * EXPERT_BLOCK, EXPERT_BLOCK), EXPERT_BLOCK)], sem)
        cp.wait() if wait else cp.start()
        return carry

    lax.fori_loop(used_ref[0], xs_ref.shape[0] // EXPERT_BLOCK, block, 0)


def _dispatch_kernel(cnt_ref, lstart_ref, gstart_ref, tail_rows_ref, tail_start_ref, used_ref,
                     h_ref, pos_ref, gate_ref, xs_ref, loc_ref, zero_ref, sem):
    tile = pl.program_id(0) * pl.num_programs(1) + pl.program_id(1)
    last = pl.num_programs(0) * pl.num_programs(1) - 1
    kb, tl, d = h_ref.shape
    tm = kb * tl
    h = h_ref[...].reshape(tm, d)
    pos, gate = pos_ref[...], gate_ref[...]
    row = lax.broadcasted_iota(I32, (ONEHOT_CHUNK, tm), 0)
    lane = lax.broadcasted_iota(I32, (ONEHOT_CHUNK, LANES), 1)
    used_rows = lstart_ref[(tile + 1) * N_EXPERTS - 1] + cnt_ref[(tile + 1) * N_EXPERTS - 1]
    for c in range(LOCAL_ROWS // ONEHOT_CHUNK):
        @pl.when(c * ONEHOT_CHUNK < used_rows)
        def _(c=c):
            gsel = jnp.zeros((ONEHOT_CHUNK, tm), F32)
            for k in range(TOP_K):
                gsel = jnp.where((pos[k:k + 1, :] - c * ONEHOT_CHUNK) == row, gate[k:k + 1, :], gsel)
            onehot = jnp.where(gsel != 0.0, 1.0, 0.0)
            rows = jnp.dot(onehot.astype(BF16), h, preferred_element_type=F32)
            g1, g2, g3 = (t.astype(F32) for t in _split3_bf16(jnp.sum(gsel, axis=1, keepdims=True)))
            sl = slice(c * ONEHOT_CHUNK, (c + 1) * ONEHOT_CHUNK)
            loc_ref[sl, :d] = rows.astype(BF16)
            loc_ref[sl, d:] = jnp.where(lane == 0, g1, jnp.where(lane == 1, g2, jnp.where(lane == 2, g3, 0.0))).astype(BF16)

    @pl.when(tile == last)
    def _():
        zero_ref[...] = jnp.zeros(zero_ref.shape, zero_ref.dtype)
        _zero_fill(tail_rows_ref, tail_start_ref, used_ref, zero_ref, xs_ref, sem, wait=False)

    _tile_segments(tile, cnt_ref, lstart_ref, gstart_ref, loc_ref, xs_ref, sem, to_global=True, wait=False)
    _tile_segments(tile, cnt_ref, lstart_ref, gstart_ref, loc_ref, xs_ref, sem, to_global=True, wait=True)

    @pl.when(tile == last)
    def _():
        _zero_fill(tail_rows_ref, tail_start_ref, used_ref, zero_ref, xs_ref, sem, wait=True)


def _moe_dispatch(h, pos, gate, plan):
    b, l, d = h.shape
    kb, tl = _row_blocking(b, l)
    n_l = l // tl
    tok = pl.BlockSpec((kb, tl, d), lambda i, j, *_: (i, j, 0))
    per_choice = pl.BlockSpec((TOP_K, kb * tl), lambda i, j, *_: (0, i * n_l + j))
    return pl.pallas_call(
        _dispatch_kernel,
        out_shape=jax.ShapeDtypeStruct((plan['cap'], XS_COLS), BF16),
        grid_spec=pltpu.PrefetchScalarGridSpec(
            num_scalar_prefetch=6,
            grid=(b // kb, n_l),
            in_specs=[tok, per_choice, per_choice],
            out_specs=pl.BlockSpec(memory_space=pl.ANY),
            scratch_shapes=[pltpu.VMEM((LOCAL_ROWS, XS_COLS), BF16), pltpu.VMEM((EXPERT_BLOCK, XS_COLS), BF16),
                            pltpu.SemaphoreType.DMA]),
        compiler_params=_params("arbitrary", "arbitrary"),
        name="moe_dispatch",
    )(plan['seg_rows'], plan['local_start'], plan['global_start'], plan['tail_rows'], plan['tail_start'],
      plan['blocks_used'], h, pos, gate)


def _swiglu(x, w1, w3, w2):
    hid = _silu(jnp.dot(x, w1, preferred_element_type=F32)) * jnp.dot(x, w3, preferred_element_type=F32)
    return jnp.dot(hid.astype(BF16), w2, preferred_element_type=F32)


def _expert_block_kernel(be_ref, nused_ref, x_ref, w1_ref, w3_ref, w2_ref, y_ref, w1_sc, w3_sc, w2_sc):
    j = pl.program_id(0)

    @pl.when(jnp.logical_or(j == 0, be_ref[j] != be_ref[jnp.maximum(j - 1, 0)]))
    def _():
        w1_sc[...] = w1_ref[0, 0].astype(BF16)
        w3_sc[...] = w3_ref[0, 0].astype(BF16)
        w2_sc[...] = w2_ref[0, 0].astype(BF16)

    @pl.when(j < nused_ref[0])
    def _():
        g = x_ref[:, D_MODEL:].astype(F32)
        gate = (g[:, 0:1] + g[:, 1:2]) + g[:, 2:3]
        out = _swiglu(x_ref[:, :D_MODEL], w1_sc[...], w3_sc[...], w2_sc[...])
        y_ref[...] = (out * gate).astype(y_ref.dtype)

    @pl.when(j >= nused_ref[0])
    def _():
        y_ref[...] = jnp.zeros(y_ref.shape, y_ref.dtype)


def _expert_blocks(xs, plan, experts):
    layer, w1, w3, w2 = experts
    cap = xs.shape[0]
    d = D_MODEL
    weight = lambda shape: pl.BlockSpec((1, 1) + shape, lambda j, be, nu: (layer, be[j], 0, 0))
    return pl.pallas_call(
        _expert_block_kernel,
        out_shape=jax.ShapeDtypeStruct((cap, d), BF16),
        grid_spec=pltpu.PrefetchScalarGridSpec(
            num_scalar_prefetch=2,
            grid=(cap // EXPERT_BLOCK,),
            in_specs=[pl.BlockSpec((EXPERT_BLOCK, XS_COLS), lambda j, *_: (j, 0)),
                      weight((d, D_EXPERT)), weight((d, D_EXPERT)), weight((D_EXPERT, d))],
            out_specs=pl.BlockSpec((EXPERT_BLOCK, d), lambda j, *_: (j, 0)),
            scratch_shapes=[pltpu.VMEM((d, D_EXPERT), BF16), pltpu.VMEM((d, D_EXPERT), BF16),
                            pltpu.VMEM((D_EXPERT, d), BF16)]),
        compiler_params=_params("arbitrary"),
        name="moe_expert_blocks",
    )(plan['block_expert'], plan['blocks_used'], xs, w1, w3, w2)


def _combine_kernel(cnt_ref, lstart_ref, gstart_ref, ys_ref, pos_ref, h_ref, x_ref, gmod_ref, w1_ref, w3_ref, w2_ref,
                    o_ref, loc_ref, onehot_ref, sem):
    tile = pl.program_id(0) * pl.num_programs(1) + pl.program_id(1)
    kb, tl, d = x_ref.shape
    tm = kb * tl

    @pl.when(tile == 0)
    def _():
        loc_ref[...] = jnp.zeros(loc_ref.shape, loc_ref.dtype)

    _tile_segments(tile, cnt_ref, lstart_ref, gstart_ref, loc_ref, ys_ref, sem, to_global=False, wait=False)
    shared = _swiglu(h_ref[...].reshape(tm, d), w1_ref[...], w3_ref[...], w2_ref[...])
    _tile_segments(tile, cnt_ref, lstart_ref, gstart_ref, loc_ref, ys_ref, sem, to_global=False, wait=True)
    pos = pos_ref[...]
    col = lax.broadcasted_iota(I32, (tm, ONEHOT_CHUNK), 1)
    used_rows = lstart_ref[(tile + 1) * N_EXPERTS - 1] + cnt_ref[(tile + 1) * N_EXPERTS - 1]
    for c in range(LOCAL_ROWS // ONEHOT_CHUNK):
        cols = slice(c * ONEHOT_CHUNK, (c + 1) * ONEHOT_CHUNK)

        @pl.when(c * ONEHOT_CHUNK < used_rows)
        def _(c=c, cols=cols):
            onehot = jnp.zeros((tm, ONEHOT_CHUNK), F32)
            for k in range(TOP_K):
                onehot = jnp.where((pos[:, k:k + 1] - c * ONEHOT_CHUNK) == col, 1.0, onehot)
            onehot_ref[:, cols] = onehot.astype(BF16)

        @pl.when(c * ONEHOT_CHUNK >= used_rows)
        def _(cols=cols):
            onehot_ref[:, cols] = jnp.zeros((tm, ONEHOT_CHUNK), BF16)

    routed = jnp.dot(onehot_ref[...], loc_ref[...], preferred_element_type=F32)
    o_ref[...] = x_ref[...] + gmod_ref[...] * (routed + shared).reshape(kb, tl, d)


def _moe_combine(ys, pos_t, h, x, gate_mod, plan, sw1, sw3, sw2):
    b, l, d = x.shape
    kb, tl = _row_blocking(b, l)
    n_l = l // tl
    tok = pl.BlockSpec((kb, tl, d), lambda i, j, *_: (i, j, 0))
    per_batch = pl.BlockSpec((kb, 1, d), lambda i, j, *_: (i, 0, 0))
    return pl.pallas_call(
        _combine_kernel,
        out_shape=jax.ShapeDtypeStruct((b, l, d), F32),
        grid_spec=pltpu.PrefetchScalarGridSpec(
            num_scalar_prefetch=3,
            grid=(b // kb, n_l),
            in_specs=[pl.BlockSpec(memory_space=pl.ANY),
                      pl.BlockSpec((kb * tl, TOP_K), lambda i, j, *_: (i * n_l + j, 0)),
                      tok, tok, per_batch, _const_spec(sw1), _const_spec(sw3), _const_spec(sw2)],
            out_specs=tok,
            scratch_shapes=[pltpu.VMEM((LOCAL_ROWS, D_MODEL), BF16), pltpu.VMEM((kb * tl, LOCAL_ROWS), BF16),
                            pltpu.SemaphoreType.DMA]),
        compiler_params=_params("arbitrary", "arbitrary"),
        name="moe_combine",
    )(plan['seg_rows'], plan['local_start'], plan['global_start'], ys, pos_t, h, x, gate_mod, sw1, sw3, sw2)


def _dispatch_plan(counts, n_tok):
    n_tiles = counts.shape[0]
    seg = (counts + SEG_ALIGN - 1) // SEG_ALIGN * SEG_ALIGN
    local_start = jnp.cumsum(seg, axis=1) - seg
    region = jnp.sum(seg, axis=0)
    region_pad = (region + EXPERT_BLOCK - 1) // EXPERT_BLOCK * EXPERT_BLOCK
    region_end = jnp.cumsum(region_pad)
    region_beg = region_end - region_pad
    global_start = region_beg[None, :] + jnp.cumsum(seg, axis=0) - seg
    worst = TOP_K * n_tok + N_EXPERTS * (SEG_ALIGN - 1) * n_tiles + N_EXPERTS * (EXPERT_BLOCK - 1)
    cap = -(-worst // EXPERT_BLOCK) * EXPERT_BLOCK
    block_row = jnp.arange(cap // EXPERT_BLOCK, dtype=I32) * EXPERT_BLOCK
    block_expert = jnp.minimum(jnp.sum((region_end[None, :] <= block_row[:, None]).astype(I32), axis=1), N_EXPERTS - 1)
    flat = lambda a: a.reshape(-1).astype(I32)
    return dict(cap=cap, seg_rows=flat(seg), local_start=flat(local_start), global_start=flat(global_start),
                tail_rows=flat(region_pad - region), tail_start=flat(region_beg + region),
                block_expert=flat(block_expert), blocks_used=flat(region_end[-1:] // EXPERT_BLOCK))


def _moe(h, logits, x, gate_mod, router_b, experts, sw1, sw3, sw2):
    b, l, d = x.shape
    gate, pos, counts = _router(logits, router_b)
    plan = _dispatch_plan(counts[:, :, 0], b * l)
    xs = _moe_dispatch(h, pos, gate, plan)
    ys = _expert_blocks(xs, plan, experts)
    return _moe_combine(ys, pos.T, h, x, gate_mod, plan, sw1, sw3, sw2)


def _split_mod(m):
    m = m[:, None, :]
    return m[..., :D_MODEL], m[..., D_MODEL:2 * D_MODEL], m[..., 2 * D_MODEL:]


def _even_layer(x, mod_a, mod_b, past, w):
    b, l, _ = x.shape
    shift, scale, gate = _split_mod(mod_a)
    q3, k3, k, v, z, xbc, small = _inproj_call(
        _inproj_even_kernel, "inproj_even", x, w['norm_g0'], shift, scale, w['w_in'],
        [w['head_mean'], w['q_gain'], w['k_gain'], w['small_bias']],
        [FOX_HEADS * FOX_HD3, FOX_HEADS * FOX_HD3, FOX_DIM, FOX_DIM, SSD_INNER, SSD_CONV_DIM, LANES],
        [BF16, BF16, F32, F32, F32, F32, F32])
    if past['fox_k'] is None:
        cum, cumt = _fox_cumulative(small, None)
        fox = _fox_attention_prompt(q3, k3, v, cum, cumt)
    else:
        n_past = past['fox_k'].shape[1]
        past_logf = jnp.pad(past['fox_logf'].astype(F32), ((0, 0), (0, 0), (0, LANES - FOX_HEADS)))
        cum, cumt = _fox_cumulative(small, past_logf)
        fox = _fox_attention_decode(q3, k3, v, past['fox_k'].reshape(b, n_past, FOX_DIM),
                                    past['fox_v'].reshape(b, n_past, FOX_DIM), cum, cumt)
    y, conv_new, ssd_new = _ssd_mixer(xbc, z, small, past['ssd_conv'], past['ssd'], w['ssd_conv_w'], w['ssd_conv_b'],
                                      w['ssd_A_log'], w['ssd_D'], w['ssd_norm_g'])
    shift2, scale2, _ = _split_mod(mod_b)
    x, h, logits = _outproj_residual(fox, y, w['w_out'], x, gate, w['norm_g1'], shift2, scale2, w['router_w'],
                                     split_inputs=True)
    new = dict(fox_k=k.reshape(b, l, FOX_HEADS, FOX_HD), fox_v=v.reshape(b, l, FOX_HEADS, FOX_HD),
               fox_logf=small[..., :FOX_HEADS], ssd_conv=conv_new, ssd=ssd_new)
    return x, h, logits, new


def _odd_layer(x, mod_a, mod_b, past, w):
    shift, scale, gate = _split_mod(mod_a)
    rw, qkv, gz, small = _inproj_call(
        _inproj_odd_kernel, "inproj_odd", x, w['norm_g0'], shift, scale, w['w_in'],
        [w['small_bias'], w['small_alog']], [RWKV_PROJ, GDN_CONV_DIM, GDN_DIM, LANES], [F32, F32, F32, F32])
    o_rwkv, rwkv_new = _rwkv_mixer(rw, past['rwkv_shift'], past['rwkv'], w['p'])
    o_gdn, gconv_new, gdn_new = _gdn_mixer(qkv, gz, small, past['gdn_conv'], past['gdn'], w['p']['gdn_conv_w'],
                                           w['p']['gdn_norm_g'])
    shift2, scale2, _ = _split_mod(mod_b)
    x, h, logits = _outproj_residual(o_rwkv, o_gdn, w['w_out'], x, gate, w['norm_g1'], shift2, scale2, w['router_w'],
                                     split_inputs=False)
    new = dict(rwkv_shift=rw[:, -1], rwkv=rwkv_new, gdn_conv=gconv_new, gdn=gdn_new)
    return x, h, logits, new


def _run_trunk(x, mods, past_even, past_odd, layers):
    new_even, new_odd = [], []
    for i, w in enumerate(layers):
        j = i // 2
        if i % 2 == 0:
            past = {n: (None if a is None else a[j]) for n, a in past_even.items()}
            x, h, logits, st = _even_layer(x, mods[2 * i], mods[2 * i + 1], past, w)
            new_even.append(st)
        else:
            x, h, logits, st = _odd_layer(x, mods[2 * i], mods[2 * i + 1], {n: a[j] for n, a in past_odd.items()}, w)
            new_odd.append(st)
        x = _moe(h, logits, x, _split_mod(mods[2 * i + 1])[2], *w['moe'])
    stack = lambda lst: {n: jnp.stack([d[n] for d in lst]) for n in lst[0]}
    return x, stack(new_even), stack(new_odd)


def kernel(x_prompt, x_sample, cache_fox_k, cache_fox_v, cache_fox_logf, state_ssd_conv, state_ssd, state_rwkv_shift, state_rwkv, state_gdn_conv, state_gdn, c_prompt, c_sample, ada_w, ada_b, norm_g, even_w_in, even_w_out, fox_q_g, fox_k_g, fox_f_b, ssd_conv_w, ssd_conv_b, ssd_dt_bias, ssd_A_log, ssd_D, ssd_norm_g, odd_w_in, odd_w_out, rwkv_mu, rwkv_w0, rwkv_w2, rwkv_a0, rwkv_a2, rwkv_g2, rwkv_k_k, rwkv_k_a, rwkv_r_k, rwkv_ln_g, rwkv_ln_b, gdn_conv_w, gdn_A_log, gdn_dt_bias, gdn_norm_g, router_w, router_b, moe_w1, moe_w3, moe_w2, shared_w1, shared_w3, shared_w2):
    depth = ada_w.shape[0]
    ne, no = cache_fox_k.shape[0], state_rwkv.shape[0]
    bp, bs = x_prompt.shape[0], x_sample.shape[0]
    odd_params = dict(rwkv_mu=rwkv_mu, rwkv_w0=rwkv_w0, rwkv_w2=rwkv_w2, rwkv_a0=rwkv_a0, rwkv_a2=rwkv_a2, rwkv_g2=rwkv_g2,
                      rwkv_k_k=rwkv_k_k, rwkv_k_a=rwkv_k_a, rwkv_r_k=rwkv_r_k, rwkv_ln_g=rwkv_ln_g, rwkv_ln_b=rwkv_ln_b,
                      gdn_conv_w=gdn_conv_w, gdn_norm_g=gdn_norm_g)
    layers = []
    for i in range(depth):
        j = i // 2
        w = dict(norm_g0=norm_g[i, 0][None], norm_g1=norm_g[i, 1][None], router_w=router_w[i],
                 moe=(router_b[i], (i, moe_w1, moe_w3, moe_w2),
                      shared_w1[i].astype(BF16), shared_w3[i].astype(BF16), shared_w2[i].astype(BF16)))
        if i % 2 == 0:
            w.update(w_in=_pack_even_w_in(even_w_in[j]), w_out=even_w_out[j],
                     head_mean=_head_mean_matrix(FOX_DIM, FOX_HD),
                     q_gain=jnp.tile(fox_q_g[j], FOX_HEADS)[None], k_gain=jnp.tile(fox_k_g[j], FOX_HEADS)[None],
                     small_bias=_pad_lanes(fox_f_b[j], ssd_dt_bias[j]),
                     ssd_conv_w=ssd_conv_w[j], ssd_conv_b=ssd_conv_b[j], ssd_A_log=ssd_A_log[j], ssd_D=ssd_D[j],
                     ssd_norm_g=ssd_norm_g[j])
        else:
            w.update(w_in=_pack_odd_w_in(odd_w_in[j]), w_out=odd_w_out[j],
                     small_bias=_pad_lanes(jnp.zeros(GDN_HEADS), gdn_dt_bias[j]),
                     small_alog=_pad_lanes(jnp.zeros(GDN_HEADS), gdn_A_log[j]),
                     p={n: a[j] for n, a in odd_params.items()})
        layers.append(w)

    mods = _ada_modulation(jnp.concatenate([c_prompt, c_sample], axis=0), ada_w, ada_b)
    mods_p, mods_s = mods[:, :bp], mods[:, bp:]

    prompt_even = dict(fox_k=None, fox_v=None, fox_logf=None,
                       ssd_conv=jnp.zeros((ne, bp) + state_ssd_conv.shape[2:], F32),
                       ssd=jnp.zeros((ne, bp) + state_ssd.shape[2:], F32))
    prompt_odd = dict(rwkv_shift=jnp.zeros((no, bp) + state_rwkv_shift.shape[2:], F32),
                      rwkv=jnp.zeros((no, bp) + state_rwkv.shape[2:], F32),
                      gdn_conv=jnp.zeros((no, bp) + state_gdn_conv.shape[2:], F32),
                      gdn=jnp.zeros((no, bp) + state_gdn.shape[2:], F32))
    sample_even = dict(fox_k=cache_fox_k, fox_v=cache_fox_v, fox_logf=cache_fox_logf, ssd_conv=state_ssd_conv, ssd=state_ssd)
    sample_odd = dict(rwkv_shift=state_rwkv_shift, rwkv=state_rwkv, gdn_conv=state_gdn_conv, gdn=state_gdn)
    y_prompt, pe, po = _run_trunk(x_prompt, mods_p, prompt_even, prompt_odd, layers)
    y_sample, se, so = _run_trunk(x_sample, mods_s, sample_even, sample_odd, layers)
    return (y_prompt, y_sample,
            pe['fox_k'], pe['fox_v'], pe['fox_logf'], pe['ssd_conv'], pe['ssd'],
            po['rwkv_shift'], po['rwkv'], po['gdn_conv'], po['gdn'],
            se['fox_k'], se['fox_v'], se['fox_logf'], se['ssd_conv'], se['ssd'],
            so['rwkv_shift'], so['rwkv'], so['gdn_conv'], so['gdn'])
```

```python
import functools
import math

import jax
import jax.numpy as jnp
from jax import lax
from jax.experimental import pallas as pl
from jax.experimental.pallas import tpu as pltpu

F32 = jnp.float32
BF16 = jnp.bfloat16
I32 = jnp.int32

D_MODEL = 1024
CONV_W = 4
EPS = 1e-6

FOX_HEADS = 8
FOX_HD = 64
FOX_DIM = FOX_HEADS * FOX_HD
FOX_SCALE = 1.0 / math.sqrt(FOX_HD)

SSD_HEADS = 8
SSD_HD = 64
SSD_INNER = SSD_HEADS * SSD_HD
SSD_GROUPS = 2
SSD_N = 128
SSD_CONV_DIM = SSD_INNER + 2 * SSD_GROUPS * SSD_N

RWKV_HEADS = 8
RWKV_HD = 64
RWKV_DIM = RWKV_HEADS * RWKV_HD
RWKV_DECAY_LORA = 64
RWKV_A_LORA = 64
RWKV_GATE_LORA = 128
RWKV_PROJ = 3 * RWKV_DIM + RWKV_DECAY_LORA + RWKV_A_LORA + RWKV_GATE_LORA
RWKV_GN_EPS = 64e-5

GDN_HEADS = 4
GDN_HD = 128
GDN_DIM = GDN_HEADS * GDN_HD
GDN_CONV_DIM = 3 * GDN_DIM

N_EXPERTS = 64
TOP_K = 8
N_EXPERT_GROUPS = 8
TOPK_GROUPS = 4
EXPERTS_PER_GROUP = N_EXPERTS // N_EXPERT_GROUPS
D_EXPERT = 256
D_SHARED = 256
ROUTED_SCALE = 2.5

LANES = 128
VMEM_LIMIT_BYTES = 56 * 1024 * 1024


def _params(*semantics):
    return pltpu.CompilerParams(dimension_semantics=semantics, vmem_limit_bytes=VMEM_LIMIT_BYTES)


def _const_spec(a):
    return pl.BlockSpec(a.shape, lambda *_: (0,) * a.ndim, pipeline_mode=pl.Buffered(1))


def _mm(a, b):
    return jnp.dot(a.astype(BF16), b.astype(BF16), preferred_element_type=F32)


def _mm_nt(a, b):
    return lax.dot_general(a.astype(BF16), b.astype(BF16), (((1,), (1,)), ((), ())), preferred_element_type=F32)


def _mm_tn(a, b):
    return lax.dot_general(a.astype(BF16), b.astype(BF16), (((0,), (0,)), ((), ())), preferred_element_type=F32)


def _split_hi_lo(x):
    hi = x.astype(BF16)
    return hi, (x - hi.astype(F32)).astype(BF16)


def _dot3(dims, a, b, batch=((), ())):
    a_hi, a_lo = _split_hi_lo(a)
    b_hi, b_lo = _split_hi_lo(b)
    dot = lambda u, w: lax.dot_general(u, w, (dims, batch), preferred_element_type=F32)
    return dot(a_hi, b_hi) + (dot(a_hi, b_lo) + dot(a_lo, b_hi))


_mm3 = functools.partial(_dot3, ((1,), (0,)))
_mm3_nt = functools.partial(_dot3, ((1,), (1,)))
_mm3_tn = functools.partial(_dot3, ((0,), (0,)))

_HEAD_BATCH = ((0,), (0,))


def _bdot(dims, a, b):
    return lax.dot_general(a.astype(BF16), b.astype(BF16), (dims, _HEAD_BATCH), preferred_element_type=F32)


_bmm = functools.partial(_bdot, ((2,), (1,)))
_bmm_nt = functools.partial(_bdot, ((2,), (2,)))
_bmm_tn = functools.partial(_bdot, ((1,), (1,)))
_bmm3 = functools.partial(_dot3, ((2,), (1,)), batch=_HEAD_BATCH)


def _heads(x, hd):
    return jnp.stack([x[:, h * hd:(h + 1) * hd] for h in range(x.shape[1] // hd)], axis=0)


def _unheads(x):
    return jnp.concatenate([x[h] for h in range(x.shape[0])], axis=-1)


def _cat3_lhs(x):
    hi, lo = _split_hi_lo(x)
    return jnp.concatenate([hi, hi, lo], axis=-1)


def _split_hi_lo_outside(w):
    w = w.astype(F32)
    hi = lax.bitcast_convert_type(lax.bitcast_convert_type(w, jnp.uint32) & jnp.uint32(0xFFFF0000), F32)
    return hi.astype(BF16), (w - hi).astype(BF16)


def _cat3_rhs(w, axis=0):
    hi, lo = _split_hi_lo_outside(w)
    return jnp.concatenate([hi, lo, hi], axis=axis)


def _sigmoid(x):
    return 1.0 / (1.0 + jnp.exp(-x))


def _silu(x):
    return x * _sigmoid(x)


def _softplus(x):
    return jnp.maximum(x, 0.0) + jnp.log1p(jnp.exp(-jnp.abs(x)))


def _cumsum_rows(x):
    n = x.shape[0]
    row = lax.broadcasted_iota(I32, x.shape, 0)
    s = 1
    while s < n:
        x = x + jnp.where(row >= s, pltpu.roll(x, s, 0), 0.0)
        s *= 2
    return x


def _ada_norm(x, g, shift, scale):
    y = x * lax.rsqrt(jnp.mean(x * x, axis=-1, keepdims=True) + EPS)
    return (y * g) * (1.0 + scale) + shift


def _ada_kernel(c_ref, w_ref, b_ref, o_ref):
    o_ref[0] = _mm3(_silu(c_ref[...]), w_ref[0]) + b_ref[0]


def _ada_modulation(c_all, ada_w, ada_b):
    n = c_all.shape[0]
    n_mod = ada_w.shape[0] * ada_w.shape[1]
    w = ada_w.reshape(n_mod, D_MODEL, 3 * D_MODEL)
    b = ada_b.reshape(n_mod, 1, 3 * D_MODEL)
    return pl.pallas_call(
        _ada_kernel,
        out_shape=jax.ShapeDtypeStruct((n_mod, n, 3 * D_MODEL), F32),
        grid=(n_mod, 3),
        in_specs=[
            pl.BlockSpec((n, D_MODEL), lambda i, j: (0, 0)),
            pl.BlockSpec((1, D_MODEL, D_MODEL), lambda i, j: (i, 0, j)),
            pl.BlockSpec((1, 1, D_MODEL), lambda i, j: (i, 0, j)),
        ],
        out_specs=pl.BlockSpec((1, n, D_MODEL), lambda i, j: (i, 0, j)),
        compiler_params=_params("parallel", "parallel"),
        name="ada_modulation",
    )(c_all, w, b)


ROW_TILE = 256

EV_Q, EV_K, EV_V, EV_Z, EV_XBC, EV_SMALL = 0, FOX_DIM, 2 * FOX_DIM, 3 * FOX_DIM, 3 * FOX_DIM + SSD_INNER, 3 * FOX_DIM + SSD_INNER + SSD_CONV_DIM
EV_COLS = EV_SMALL + LANES
OD_RW, OD_QKV, OD_Z, OD_SMALL = 0, RWKV_PROJ, RWKV_PROJ + GDN_CONV_DIM, RWKV_PROJ + GDN_CONV_DIM + GDN_DIM
OD_COLS = OD_SMALL + LANES


def _row_blocking(b, l):
    if l >= ROW_TILE:
        assert l % ROW_TILE == 0
        return 1, ROW_TILE
    assert ROW_TILE % l == 0 and b % (ROW_TILE // l) == 0 and l % 8 == 0
    return ROW_TILE // l, l


def _normed_rows(x_ref, g_ref, sh_ref, sc_ref):
    x = x_ref[...]
    h = _ada_norm(x, g_ref[...], sh_ref[...], sc_ref[...])
    return h.reshape(x.shape[0] * x.shape[1], x.shape[2])


def _head_rms(y, head_mean, gain):
    hi, lo = _split_hi_lo(y * y)
    ms = jnp.dot(hi, head_mean, preferred_element_type=F32) + jnp.dot(lo, head_mean, preferred_element_type=F32)
    return y * lax.rsqrt(ms + EPS) * gain


FOX_HD3 = 3 * FOX_HD


def _head_cat3(x, rhs):
    hi, lo = _split_hi_lo(x)
    pieces = []
    for h in range(FOX_HEADS):
        hs = slice(h * FOX_HD, (h + 1) * FOX_HD)
        pieces += [hi[:, hs], lo[:, hs], hi[:, hs]] if rhs else [hi[:, hs], hi[:, hs], lo[:, hs]]
    return jnp.concatenate(pieces, axis=-1)


def _inproj_even_kernel(x_ref, g_ref, sh_ref, sc_ref, w_ref, hm_ref, qg_ref, kg_ref, bias_ref,
                        q3_ref, k3_ref, k_ref, v_ref, z_ref, xbc_ref, small_ref):
    h = _cat3_lhs(_normed_rows(x_ref, g_ref, sh_ref, sc_ref))
    shp3 = lambda r: (x_ref.shape[0], x_ref.shape[1], r.shape[2])
    cols = lambda lo, hi: jnp.dot(h, w_ref[:, lo:hi], preferred_element_type=F32)
    hm = hm_ref[...]
    q = _head_rms(cols(EV_Q, EV_K), hm, qg_ref[...]) * FOX_SCALE
    q3_ref[...] = _head_cat3(q, rhs=False).reshape(shp3(q3_ref))
    k = _head_rms(cols(EV_K, EV_V), hm, kg_ref[...])
    k_ref[...] = k.reshape(shp3(k_ref))
    k3_ref[...] = _head_cat3(k, rhs=True).reshape(shp3(k3_ref))
    v_ref[...] = cols(EV_V, EV_Z).reshape(shp3(v_ref))
    z_ref[...] = cols(EV_Z, EV_XBC).reshape(shp3(z_ref))
    xbc_ref[...] = cols(EV_XBC, EV_SMALL).reshape(shp3(xbc_ref))
    t = cols(EV_SMALL, EV_COLS) + bias_ref[...]
    lane = lax.broadcasted_iota(I32, t.shape, 1)
    small = jnp.where(lane < FOX_HEADS, -_softplus(-t), jnp.where(lane < FOX_HEADS + SSD_HEADS, _softplus(t), 0.0))
    small_ref[...] = small.reshape(shp3(small_ref))


def _inproj_odd_kernel(x_ref, g_ref, sh_ref, sc_ref, w_ref, bias_ref, alog_ref,
                       rw_ref, qkv_ref, z_ref, small_ref):
    h = _normed_rows(x_ref, g_ref, sh_ref, sc_ref).astype(BF16)
    shp3 = lambda r: (x_ref.shape[0], x_ref.shape[1], r.shape[2])
    rw_ref[...] = jnp.dot(h, w_ref[:, OD_RW:OD_QKV], preferred_element_type=F32).reshape(shp3(rw_ref))
    qkv_ref[...] = jnp.dot(h, w_ref[:, OD_QKV:OD_Z], preferred_element_type=F32).reshape(shp3(qkv_ref))
    z_ref[...] = jnp.dot(h, w_ref[:, OD_Z:OD_SMALL], preferred_element_type=F32).reshape(shp3(z_ref))
    t = jnp.dot(h, w_ref[:, OD_SMALL:OD_COLS], preferred_element_type=F32) + bias_ref[...]
    lane = lax.broadcasted_iota(I32, t.shape, 1)
    small = jnp.where(lane < GDN_HEADS, _sigmoid(t),
                      jnp.where(lane < 2 * GDN_HEADS, -jnp.exp(alog_ref[...]) * _softplus(t), 0.0))
    small_ref[...] = small.reshape(shp3(small_ref))


def _inproj_call(kernel_fn, name, x, g, shift, scale, w, extras, out_cols, out_dtypes):
    b, l, d = x.shape
    kb, tl = _row_blocking(b, l)
    tok = lambda c: pl.BlockSpec((kb, tl, c), lambda i, j: (i, j, 0))
    per_batch = pl.BlockSpec((kb, 1, d), lambda i, j: (i, 0, 0))
    const = _const_spec
    return pl.pallas_call(
        kernel_fn,
        out_shape=[jax.ShapeDtypeStruct((b, l, c), dt) for c, dt in zip(out_cols, out_dtypes)],
        grid=(b // kb, l // tl),
        in_specs=[tok(d), const(g), per_batch, per_batch, const(w)] + [const(e) for e in extras],
        out_specs=[tok(c) for c in out_cols],
        compiler_params=_params("parallel", "parallel"),
        name=name,
    )(x, g, shift, scale, w, *extras)


def _pad_lanes(*pieces):
    v = jnp.concatenate([p.reshape(-1).astype(F32) for p in pieces])
    return jnp.pad(v, (0, LANES - v.shape[0])).reshape(1, LANES)


def _head_mean_matrix(dim, hd):
    r = jnp.arange(dim) // hd
    return jnp.where(r[:, None] == r[None, :], 1.0 / hd, 0.0).astype(BF16)


def _pack_even_w_in(w_in):
    q, k, v, f, z, xbc, dt = jnp.split(w_in, (FOX_DIM, 2 * FOX_DIM, 3 * FOX_DIM, 3 * FOX_DIM + FOX_HEADS,
                                              3 * FOX_DIM + FOX_HEADS + SSD_INNER,
                                              3 * FOX_DIM + FOX_HEADS + SSD_INNER + SSD_CONV_DIM), axis=1)
    pad = jnp.zeros((w_in.shape[0], LANES - FOX_HEADS - SSD_HEADS), w_in.dtype)
    return _cat3_rhs(jnp.concatenate([q, k, v, z, xbc, f, dt, pad], axis=1))


def _pack_odd_w_in(w_in):
    main, small = w_in[:, :OD_SMALL], w_in[:, OD_SMALL:]
    pad = jnp.zeros((w_in.shape[0], LANES - 2 * GDN_HEADS), w_in.dtype)
    return jnp.concatenate([main, small, pad], axis=1).astype(BF16)


def _split3_bf16(x):
    c1 = x.astype(BF16)
    r1 = x - c1.astype(F32)
    c2 = r1.astype(BF16)
    c3 = (r1 - c2.astype(F32)).astype(BF16)
    return c1, c2, c3


def _rows_to_lanes(sel, x):
    nt = lambda a, b: lax.dot_general(a, b, (((1,), (1,)), ((), ())), preferred_element_type=F32)
    c1, c2, c3 = _split3_bf16(x)
    return (nt(sel, c1) + nt(sel, c2)) + nt(sel, c3)


def _fox_cum_kernel(*refs, n_past):
    if n_past:
        past_ref, small_ref, sel_ref, cum_ref, cumt_ref = refs
        seq = jnp.concatenate([past_ref[0], small_ref[0]], axis=0)
    else:
        small_ref, sel_ref, cum_ref, cumt_ref = refs
        seq = small_ref[0]
    cum = _cumsum_rows(seq)
    cum_ref[0] = cum[n_past:]
    cumt_ref[0] = _rows_to_lanes(sel_ref[...], cum)


def _fox_cumulative(small, past_logf):
    b, l, _ = small.shape
    n_past = 0 if past_logf is None else past_logf.shape[1]
    sel = jnp.eye(FOX_HEADS, LANES, dtype=BF16)
    ins, specs = [], []
    if n_past:
        ins.append(past_logf)
        specs.append(pl.BlockSpec((1, n_past, LANES), lambda i: (i, 0, 0)))
    ins += [small, sel]
    specs += [pl.BlockSpec((1, l, LANES), lambda i: (i, 0, 0)), pl.BlockSpec(sel.shape, lambda i: (0, 0))]
    return pl.pallas_call(
        functools.partial(_fox_cum_kernel, n_past=n_past),
        out_shape=[jax.ShapeDtypeStruct((b, l, LANES), F32), jax.ShapeDtypeStruct((b, FOX_HEADS, n_past + l), F32)],
        grid=(b,),
        in_specs=specs,
        out_specs=[pl.BlockSpec((1, l, LANES), lambda i: (i, 0, 0)),
                   pl.BlockSpec((1, FOX_HEADS, n_past + l), lambda i: (i, 0, 0))],
        compiler_params=_params("parallel"),
        name="fox_cumulative",
    )(*ins)


FOX_TILE = 512


def _fox_flash_kernel(q_ref, k_ref, v_ref, cq_ref, ck_ref, o_ref, m_sc, l_sc, acc_sc):
    qi, kj = pl.program_id(1), pl.program_id(2)

    @pl.when(kj == 0)
    def _():
        m_sc[...] = jnp.full(m_sc.shape, -jnp.inf, F32)
        l_sc[...] = jnp.zeros(l_sc.shape, F32)
        acc_sc[...] = jnp.zeros(acc_sc.shape, F32)

    def step(on_diagonal):
        q3, k3 = q_ref[0], k_ref[0]
        v_hi, v_lo = _split_hi_lo(v_ref[0])
        cq, ck = cq_ref[0], ck_ref[0]
        if on_diagonal:
            visible = (lax.broadcasted_iota(I32, (FOX_TILE, FOX_TILE), 1)
                       <= lax.broadcasted_iota(I32, (FOX_TILE, FOX_TILE), 0))
        for h in range(FOX_HEADS):
            hs = slice(h * FOX_HD, (h + 1) * FOX_HD)
            hs3 = slice(h * FOX_HD3, (h + 1) * FOX_HD3)
            s = _scores3(q3[:, hs3], k3[:, hs3]) + (cq[:, h:h + 1] - ck[h:h + 1, :])
            if on_diagonal:
                s = jnp.where(visible, s, -jnp.inf)
            m_prev = m_sc[h]
            m_new = jnp.maximum(m_prev, jnp.max(s, axis=-1, keepdims=True))
            p = jnp.exp(s - m_new)
            alpha = jnp.exp(m_prev - m_new)
            l_sc[h] = alpha * l_sc[h] + jnp.sum(p, axis=-1, keepdims=True)
            acc_sc[h] = alpha * acc_sc[h] + _pv3(p, v_hi[:, hs], v_lo[:, hs])
            m_sc[h] = m_new

    @pl.when(kj < qi)
    def _():
        step(on_diagonal=False)

    @pl.when(kj == qi)
    def _():
        step(on_diagonal=True)
        o_ref[0] = jnp.concatenate([acc_sc[h] / l_sc[h] for h in range(FOX_HEADS)], axis=-1)


def _scores3(q3, k3):
    return lax.dot_general(q3, k3, (((1,), (1,)), ((), ())), preferred_element_type=F32)


def _pv3(p, v_hi, v_lo):
    p_hi, p_lo = _split_hi_lo(p)
    dot = lambda a, b: jnp.dot(a, b, preferred_element_type=F32)
    return dot(p_hi, v_hi) + (dot(p_hi, v_lo) + dot(p_lo, v_hi))


def _fox_attention_prompt(q3, k3, v, cum, cumt):
    b, l, _ = v.shape
    n = l // FOX_TILE
    qspec = lambda c: pl.BlockSpec((1, FOX_TILE, c), lambda i, j, t: (i, j, 0))
    kspec = lambda c: pl.BlockSpec((1, FOX_TILE, c), lambda i, j, t: (i, jnp.minimum(t, j), 0))
    return pl.pallas_call(
        _fox_flash_kernel,
        out_shape=jax.ShapeDtypeStruct((b, l, FOX_DIM), F32),
        grid=(b, n, n),
        in_specs=[qspec(FOX_HEADS * FOX_HD3), kspec(FOX_HEADS * FOX_HD3), kspec(FOX_DIM), qspec(LANES),
                  pl.BlockSpec((1, FOX_HEADS, FOX_TILE), lambda i, j, t: (i, 0, jnp.minimum(t, j)))],
        out_specs=qspec(FOX_DIM),
        scratch_shapes=[pltpu.VMEM((FOX_HEADS, FOX_TILE, 1), F32), pltpu.VMEM((FOX_HEADS, FOX_TILE, 1), F32),
                        pltpu.VMEM((FOX_HEADS, FOX_TILE, FOX_HD), F32)],
        compiler_params=_params("parallel", "parallel", "arbitrary"),
        name="fox_flash",
    )(q3, k3, v, cum, cumt)


def _fox_decode_kernel(q_ref, kp_ref, vp_ref, kn_ref, vn_ref, cq_ref, ck_ref, o_ref):
    n_past, l = kp_ref.shape[1], q_ref.shape[1]
    q3, kn3 = q_ref[0], kn_ref[0]
    kp3 = _head_cat3(kp_ref[0], rhs=True)
    vp_hi, vp_lo = _split_hi_lo(vp_ref[0])
    vn_hi, vn_lo = _split_hi_lo(vn_ref[0])
    cq, ck = cq_ref[0], ck_ref[0]
    row = lax.broadcasted_iota(I32, (l, l), 0)
    col = lax.broadcasted_iota(I32, (l, l), 1)
    outs = []
    for h in range(FOX_HEADS):
        hs = slice(h * FOX_HD, (h + 1) * FOX_HD)
        hs3 = slice(h * FOX_HD3, (h + 1) * FOX_HD3)
        cqh = cq[:, h:h + 1]
        s_past = _scores3(q3[:, hs3], kp3[:, hs3]) + (cqh - ck[h:h + 1, :n_past])
        s_new = _scores3(q3[:, hs3], kn3[:, hs3]) + (cqh - ck[h:h + 1, n_past:])
        s_new = jnp.where(col <= row, s_new, -jnp.inf)
        m = jnp.maximum(jnp.max(s_past, axis=-1, keepdims=True), jnp.max(s_new, axis=-1, keepdims=True))
        p_past, p_new = jnp.exp(s_past - m), jnp.exp(s_new - m)
        denom = jnp.sum(p_past, axis=-1, keepdims=True) + jnp.sum(p_new, axis=-1, keepdims=True)
        outs.append((_pv3(p_past, vp_hi[:, hs], vp_lo[:, hs]) + _pv3(p_new, vn_hi[:, hs], vn_lo[:, hs])) / denom)
    o_ref[0] = jnp.concatenate(outs, axis=-1)


def _fox_attention_decode(q3, k3, v, k_past, v_past, cum, cumt):
    b, l, _ = v.shape
    n_past = k_past.shape[1]
    new = lambda c: pl.BlockSpec((1, l, c), lambda i: (i, 0, 0))
    past = pl.BlockSpec((1, n_past, FOX_DIM), lambda i: (i, 0, 0))
    return pl.pallas_call(
        _fox_decode_kernel,
        out_shape=jax.ShapeDtypeStruct((b, l, FOX_DIM), F32),
        grid=(b,),
        in_specs=[new(FOX_HEADS * FOX_HD3), past, past, new(FOX_HEADS * FOX_HD3), new(FOX_DIM), new(LANES),
                  pl.BlockSpec((1, FOX_HEADS, n_past + l), lambda i: (i, 0, 0))],
        out_specs=new(FOX_DIM),
        compiler_params=_params("parallel"),
        name="fox_decode",
    )(q3, k_past, v_past, k3, v, cum, cumt)


CONV_TAIL = 8


def _causal_conv(x, tail, w):
    q = x.shape[0]
    row = lax.broadcasted_iota(I32, (CONV_TAIL, x.shape[1]), 0)
    acc = x * w[CONV_W - 1:CONV_W]
    for s in range(1, CONV_W):
        xr = pltpu.roll(x, s, 0)
        head = jnp.where(row < s, pltpu.roll(tail, s, 0), xr[:CONV_TAIL])
        shifted = head if q == CONV_TAIL else jnp.concatenate([head, xr[CONV_TAIL:]], axis=0)
        acc = acc + shifted * w[CONV_W - 1 - s:CONV_W - s]
    return acc


def _conv_tail_from_state(state):
    return jnp.pad(state, ((0, 0), (CONV_TAIL - (CONV_W - 1), 0), (0, 0)))


SSD_CHUNK = 128
SSD_DT_LANE = FOX_HEADS


def _ssd_kernel(xbc_ref, z_ref, small_ref, tail0_ref, h0_ref, cw_ref, cb_ref, alog_ref, drep_ref, ng_ref, sel_ref,
                y_ref, tail_ref, hout_ref, tail_sc, h_sc):
    c = pl.program_id(1)

    @pl.when(c == 0)
    def _():
        tail_sc[...] = tail0_ref[0]
        h_sc[...] = h0_ref[0]

    x = xbc_ref[0]
    q = x.shape[0]
    act = _silu(_causal_conv(x, tail_sc[...], cw_ref[...]) + cb_ref[...])
    tail_sc[...] = x[q - CONV_TAIL:]
    xs = act[:, :SSD_INNER]
    bm = act[:, SSD_INNER:SSD_INNER + SSD_GROUPS * SSD_N]
    cm = act[:, SSD_INNER + SSD_GROUPS * SSD_N:]

    small = small_ref[0]
    lane = lax.broadcasted_iota(I32, small.shape, 1)
    is_dt = jnp.logical_and(lane >= SSD_DT_LANE, lane < SSD_DT_LANE + SSD_HEADS)
    da = jnp.where(is_dt, small * -jnp.exp(alog_ref[...]), 0.0)
    acum = _cumsum_rows(da)
    acum_t = _rows_to_lanes(sel_ref[...], acum)
    trow = lax.broadcasted_iota(I32, (q, q), 0)
    tcol = lax.broadcasted_iota(I32, (q, q), 1)
    heads_per_group = SSD_HEADS // SSD_GROUPS
    ys = []
    for g in range(SSD_GROUPS):
        bg = bm[:, g * SSD_N:(g + 1) * SSD_N]
        cg = cm[:, g * SSD_N:(g + 1) * SSD_N]
        cb = _mm3_nt(cg, bg)
        for h in range(g * heads_per_group, (g + 1) * heads_per_group):
            a_col = acum[:, SSD_DT_LANE + h:SSD_DT_LANE + h + 1]
            dt_col = small[:, SSD_DT_LANE + h:SSD_DT_LANE + h + 1]
            a_last = a_col[q - 1:q]
            seg = jnp.exp(jnp.where(trow >= tcol, a_col - acum_t[h:h + 1, :], -jnp.inf))
            xdt = xs[:, h * SSD_HD:(h + 1) * SSD_HD] * dt_col
            state = h_sc[h]
            ys.append(_mm3(cb * seg, xdt) + jnp.exp(a_col) * _mm3_nt(cg, state))
            h_sc[h] = state * jnp.exp(a_last) + _mm3_tn(xdt * jnp.exp(a_last - a_col), bg)
    y = jnp.concatenate(ys, axis=-1) + drep_ref[...] * xs
    y = y * _silu(z_ref[0])
    gw = SSD_INNER // SSD_GROUPS
    normed = []
    for g in range(SSD_GROUPS):
        yg = y[:, g * gw:(g + 1) * gw]
        normed.append(yg * lax.rsqrt(jnp.mean(yg * yg, axis=-1, keepdims=True) + EPS))
    y_ref[0] = (jnp.concatenate(normed, axis=-1) * ng_ref[...]).astype(y_ref.dtype)

    @pl.when(c == pl.num_programs(1) - 1)
    def _():
        tail_ref[0] = tail_sc[...]
        hout_ref[0] = h_sc[...]


def _ssd_mixer(xbc, z, small, conv_state, h0, conv_w, conv_b, a_log, d_skip, norm_g):
    b, l, _ = xbc.shape
    q = min(SSD_CHUNK, l)
    tail0 = _conv_tail_from_state(conv_state)
    alog = _pad_lanes(jnp.zeros(SSD_DT_LANE), a_log)
    drep = jnp.repeat(d_skip.astype(F32), SSD_HD)[None]
    sel = jnp.eye(SSD_HEADS, LANES, k=SSD_DT_LANE, dtype=BF16)
    tok = lambda c: pl.BlockSpec((1, q, c), lambda i, j: (i, j, 0))
    per_b = lambda a: pl.BlockSpec((1,) + a.shape[1:], lambda i, j: (i,) + (0,) * (a.ndim - 1))
    const = _const_spec
    consts = [conv_w, conv_b[None], alog, drep, norm_g[None], sel]
    y, tail, h_new = pl.pallas_call(
        _ssd_kernel,
        out_shape=[jax.ShapeDtypeStruct((b, l, SSD_INNER), F32),
                   jax.ShapeDtypeStruct((b, CONV_TAIL, SSD_CONV_DIM), F32),
                   jax.ShapeDtypeStruct(h0.shape, F32)],
        grid=(b, l // q),
        in_specs=[tok(SSD_CONV_DIM), tok(SSD_INNER), tok(LANES), per_b(tail0), per_b(h0)] + [const(a) for a in consts],
        out_specs=[tok(SSD_INNER), per_b(tail0), per_b(h0)],
        scratch_shapes=[pltpu.VMEM((CONV_TAIL, SSD_CONV_DIM), F32), pltpu.VMEM(h0.shape[1:], F32)],
        compiler_params=_params("parallel", "arbitrary"),
        name="ssd_mixer",
    )(xbc, z, small, tail0, h0, *consts)
    return y, tail[:, CONV_TAIL - (CONV_W - 1):], h_new


REC_CHUNK = 64


INV_BASE = 8


def _unit_lower_inverse(n):
    c = n.shape[-1]
    mm = _mm3 if n.ndim == 2 else _bmm3
    merge_mm = _mm if n.ndim == 2 else _bmm
    row = lax.broadcasted_iota(I32, (c, c), 0)
    col = lax.broadcasted_iota(I32, (c, c), 1)
    shift = INV_BASE.bit_length() - 1
    same = lax.shift_right_logical(row, shift) == lax.shift_right_logical(col, shift)
    diag = jnp.where(same, n, 0.0)
    inv = jnp.where(row == col, 1.0, 0.0) + diag
    power = diag
    span = 2
    while span < INV_BASE:
        power = mm(power, power)
        inv = inv + mm(inv, power)
        span *= 2
    size = INV_BASE
    while size < c:
        shift += 1
        size *= 2
        merged = lax.shift_right_logical(row, shift) == lax.shift_right_logical(col, shift)
        off = jnp.where(jnp.logical_and(merged, jnp.logical_not(same)), n, 0.0)
        inv = inv + merge_mm(merge_mm(inv, off), inv)
        same = merged
    return inv


def _shift_rows_by_one(x, tail):
    q = x.shape[0]
    row = lax.broadcasted_iota(I32, (CONV_TAIL, x.shape[1]), 0)
    xr = pltpu.roll(x, 1, 0)
    head = jnp.where(row < 1, pltpu.roll(tail, 1, 0), xr[:CONV_TAIL])
    return head if q == CONV_TAIL else jnp.concatenate([head, xr[CONV_TAIL:]], axis=0)


RW_R, RW_K, RW_V, RW_LORA, RW_GATE = 0, RWKV_DIM, 2 * RWKV_DIM, 3 * RWKV_DIM, 3 * RWKV_DIM + RWKV_DECAY_LORA + RWKV_A_LORA


def _rwkv_kernel(rw_ref, tail0_ref, s0_ref, mu_ref, w0_ref, w2_ref, a0_ref, a2_ref, g2_ref, kk_ref, ka_ref, rk_ref,
                 lng_ref, lnb_ref, hsum_ref, o_ref, sout_ref, tail_sc, s_sc):
    c = pl.program_id(1)

    @pl.when(c == 0)
    def _():
        tail_sc[...] = tail0_ref[0]
        s_sc[...] = s0_ref[0]

    x = rw_ref[0]
    q = x.shape[0]
    mixed = x + (_shift_rows_by_one(x, tail_sc[...]) - x) * mu_ref[...]
    tail_sc[...] = x[q - CONV_TAIL:]
    r = mixed[:, RW_R:RW_K]
    k = mixed[:, RW_K:RW_V]
    v = mixed[:, RW_V:RW_LORA]
    lora = mixed[:, RW_LORA:RW_GATE]
    w_log = -_softplus(-(w0_ref[...] + _mm(jnp.tanh(lora), w2_ref[...]))) - 0.5
    logw = -jnp.exp(w_log)
    icl = _sigmoid(a0_ref[...] + _mm(lora, a2_ref[...]))
    out_gate = _mm(_sigmoid(mixed[:, RW_GATE:]), g2_ref[...])
    kk = k * kk_ref[...]
    kk = kk * lax.rsqrt(_mm(kk * kk, hsum_ref[...]) + EPS)
    k2 = k * (1.0 + (icl - 1.0) * ka_ref[...])
    cum = _cumsum_rows(logw)
    w_run = jnp.exp(cum)
    w_inv = jnp.exp(-cum)
    rt = r * w_run
    at = -kk * jnp.exp(cum - logw)
    bt = kk * icl * w_inv
    kt = k2 * w_inv
    trow = lax.broadcasted_iota(I32, (q, q), 0)
    tcol = lax.broadcasted_iota(I32, (q, q), 1)
    strict, incl = trow > tcol, trow >= tcol
    hd = RWKV_HD
    at, rt, bt, kt, vh, rh, k2h = (_heads(t, hd) for t in (at, rt, bt, kt, v, r, k2))
    a_ab = jnp.where(strict, _bmm_nt(at, bt), 0.0)
    a_ak = jnp.where(strict, _bmm_nt(at, kt), 0.0)
    a_rb = jnp.where(incl, _bmm_nt(rt, bt), 0.0)
    a_rk = jnp.where(incl, _bmm_nt(rt, kt), 0.0)
    s0 = s_sc[...]
    u = _bmm(_unit_lower_inverse(a_ab), _bmm_nt(at, s0) + _bmm(a_ak, vh))
    o = _bmm_nt(rt, s0) + (_bmm(a_rb, u) + _bmm(a_rk, vh))
    s_sc[...] = (s0 + (_bmm_tn(u, bt) + _bmm_tn(vh, kt))) * _heads(w_run[q - 1:q], hd)
    mean = jnp.mean(o, axis=-1, keepdims=True)
    var = jnp.mean(jnp.square(o - mean), axis=-1, keepdims=True)
    o = (o - mean) * lax.rsqrt(var + RWKV_GN_EPS) * _heads(lng_ref[...], hd) + _heads(lnb_ref[...], hd)
    bonus = jnp.sum(rh * k2h * _heads(rk_ref[...], hd), axis=-1, keepdims=True) * vh
    o_ref[0] = (_unheads(o + bonus) * out_gate).astype(o_ref.dtype)

    @pl.when(c == pl.num_programs(1) - 1)
    def _():
        sout_ref[0] = s_sc[...]


def _rwkv_mixer(rw, shift_state, s0, p):
    b, l, _ = rw.shape
    q = min(REC_CHUNK, l)
    tail0 = jnp.pad(shift_state[:, None, :], ((0, 0), (CONV_TAIL - 1, 0), (0, 0)))
    zeros = jnp.zeros((RWKV_DECAY_LORA, RWKV_DIM), F32)
    consts = [p['rwkv_mu'][None], p['rwkv_w0'][None],
              jnp.concatenate([p['rwkv_w2'], zeros], axis=0).astype(BF16), p['rwkv_a0'][None],
              jnp.concatenate([zeros, p['rwkv_a2']], axis=0).astype(BF16), p['rwkv_g2'].astype(BF16),
              p['rwkv_k_k'][None], p['rwkv_k_a'][None], p['rwkv_r_k'].reshape(1, RWKV_DIM),
              p['rwkv_ln_g'][None], p['rwkv_ln_b'][None], _head_mean_matrix(RWKV_DIM, RWKV_HD) * RWKV_HD]
    tok = lambda c: pl.BlockSpec((1, q, c), lambda i, j: (i, j, 0))
    per_b = lambda a: pl.BlockSpec((1,) + a.shape[1:], lambda i, j: (i,) + (0,) * (a.ndim - 1))
    const = _const_spec
    return pl.pallas_call(
        _rwkv_kernel,
        out_shape=[jax.ShapeDtypeStruct((b, l, RWKV_DIM), BF16), jax.ShapeDtypeStruct(s0.shape, F32)],
        grid=(b, l // q),
        in_specs=[tok(RWKV_PROJ), per_b(tail0), per_b(s0)] + [const(a) for a in consts],
        out_specs=[tok(RWKV_DIM), per_b(s0)],
        scratch_shapes=[pltpu.VMEM((CONV_TAIL, RWKV_PROJ), F32), pltpu.VMEM(s0.shape[1:], F32)],
        compiler_params=_params("parallel", "arbitrary"),
        name="rwkv_mixer",
    )(rw, tail0, s0, *consts)


GDN_BETA_LANE, GDN_G_LANE = 0, GDN_HEADS


def _l2norm(x):
    return x * lax.rsqrt(jnp.sum(x * x, axis=-1, keepdims=True) + EPS)


def _gdn_kernel(qkv_ref, z_ref, small_ref, tail0_ref, s0_ref, cw_ref, ng_ref, sel_ref,
                o_ref, tail_ref, sout_ref, tail_sc, s_sc):
    c = pl.program_id(1)

    @pl.when(c == 0)
    def _():
        tail_sc[...] = tail0_ref[0]
        s_sc[...] = s0_ref[0]

    x = qkv_ref[0]
    q = x.shape[0]
    act = _silu(_causal_conv(x, tail_sc[...], cw_ref[...]))
    tail_sc[...] = x[q - CONV_TAIL:]
    small = small_ref[0]
    z = z_ref[0]
    lane = lax.broadcasted_iota(I32, small.shape, 1)
    is_g = jnp.logical_and(lane >= GDN_G_LANE, lane < GDN_G_LANE + GDN_HEADS)
    gam = _cumsum_rows(jnp.where(is_g, small, 0.0))
    gam_t = _rows_to_lanes(sel_ref[...], gam)
    trow = lax.broadcasted_iota(I32, (q, q), 0)
    tcol = lax.broadcasted_iota(I32, (q, q), 1)
    heads = range(GDN_HEADS)
    qh = _l2norm(_heads(act[:, :GDN_DIM], GDN_HD)) * (GDN_HD ** -0.5)
    kh = _l2norm(_heads(act[:, GDN_DIM:2 * GDN_DIM], GDN_HD))
    vh = _heads(act[:, 2 * GDN_DIM:], GDN_HD)
    beta = jnp.stack([small[:, GDN_BETA_LANE + h:GDN_BETA_LANE + h + 1] for h in heads])
    g_col = jnp.stack([gam[:, GDN_G_LANE + h:GDN_G_LANE + h + 1] for h in heads])
    g_row = jnp.stack([gam_t[h:h + 1, :] for h in heads])
    g_last = g_col[:, q - 1:q]
    diff = g_col - g_row
    a_mat = _bmm_nt(kh, kh) * jnp.exp(jnp.where(trow > tcol, diff, -jnp.inf)) * beta
    eg = jnp.exp(g_col)
    uw = _bmm(_unit_lower_inverse(-a_mat), jnp.concatenate([vh * beta, kh * (beta * eg)], axis=2))
    qk = _bmm_nt(qh, kh) * jnp.exp(jnp.where(trow >= tcol, diff, -jnp.inf))
    state = s_sc[...]
    nu = uw[:, :, :GDN_HD] - _bmm(uw[:, :, GDN_HD:], state)
    o = _bmm(qh * eg, state) + _bmm(qk, nu)
    s_sc[...] = state * jnp.exp(g_last) + _bmm_tn(kh * jnp.exp(g_last - g_col), nu)
    o = o * lax.rsqrt(jnp.mean(o * o, axis=-1, keepdims=True) + EPS) * ng_ref[...]
    o_ref[0] = (_unheads(o) * _silu(z)).astype(o_ref.dtype)

    @pl.when(c == pl.num_programs(1) - 1)
    def _():
        tail_ref[0] = tail_sc[...]
        sout_ref[0] = s_sc[...]


def _gdn_mixer(qkv, z, small, conv_state, s0, conv_w, norm_g):
    b, l, _ = qkv.shape
    q = min(REC_CHUNK, l)
    tail0 = _conv_tail_from_state(conv_state)
    sel = jnp.eye(GDN_HEADS, LANES, k=GDN_G_LANE, dtype=BF16)
    consts = [conv_w, norm_g[None], sel]
    tok = lambda c: pl.BlockSpec((1, q, c), lambda i, j: (i, j, 0))
    per_b = lambda a: pl.BlockSpec((1,) + a.shape[1:], lambda i, j: (i,) + (0,) * (a.ndim - 1))
    const = _const_spec
    o, tail, s_new = pl.pallas_call(
        _gdn_kernel,
        out_shape=[jax.ShapeDtypeStruct((b, l, GDN_DIM), BF16),
                   jax.ShapeDtypeStruct((b, CONV_TAIL, GDN_CONV_DIM), F32),
                   jax.ShapeDtypeStruct(s0.shape, F32)],
        grid=(b, l // q),
        in_specs=[tok(GDN_CONV_DIM), tok(GDN_DIM), tok(LANES), per_b(tail0), per_b(s0)] + [const(a) for a in consts],
        out_specs=[tok(GDN_DIM), per_b(tail0), per_b(s0)],
        scratch_shapes=[pltpu.VMEM((CONV_TAIL, GDN_CONV_DIM), F32), pltpu.VMEM(s0.shape[1:], F32)],
        compiler_params=_params("parallel", "arbitrary"),
        name="gdn_mixer",
    )(qkv, z, small, tail0, s0, *consts)
    return o, tail[:, CONV_TAIL - (CONV_W - 1):], s_new


def _outproj_kernel(a_ref, b_ref, w_ref, x_ref, gate_ref, g_ref, sh_ref, sc_ref, rwt_ref,
                    xo_ref, h_ref, logit_ref, *, split_inputs):
    kb, tl, d = x_ref.shape
    rows = kb * tl
    a = a_ref[...].reshape(rows, a_ref.shape[2])
    b = b_ref[...].reshape(rows, b_ref.shape[2])
    lhs = jnp.concatenate([_cat3_lhs(a), _cat3_lhs(b)] if split_inputs else [a.astype(BF16), b.astype(BF16)], axis=-1)
    mix = jnp.dot(lhs, w_ref[...], preferred_element_type=F32)
    x_new = x_ref[...] + gate_ref[...] * mix.reshape(kb, tl, d)
    xo_ref[...] = x_new
    h = _ada_norm(x_new, g_ref[...], sh_ref[...], sc_ref[...])
    h_ref[...] = h.astype(h_ref.dtype)
    logit_ref[...] = lax.dot_general(rwt_ref[...], _cat3_lhs(h.reshape(rows, d)), (((1,), (1,)), ((), ())),
                                     preferred_element_type=F32)


def _outproj_residual(a, b_half, w_out, x, gate, g2, shift2, scale2, router_w, split_inputs):
    b, l, d = x.shape
    kb, tl = _row_blocking(b, l)
    half = a.shape[2]
    pack = _cat3_rhs if split_inputs else (lambda w: w.astype(BF16))
    w = jnp.concatenate([pack(w_out[:half]), pack(w_out[half:])], axis=0)
    rwt = _cat3_rhs(router_w.T, axis=1)
    n_l = l // tl
    tok = lambda c: pl.BlockSpec((kb, tl, c), lambda i, j: (i, j, 0))
    per_batch = pl.BlockSpec((kb, 1, d), lambda i, j: (i, 0, 0))
    const = _const_spec
    return pl.pallas_call(
        functools.partial(_outproj_kernel, split_inputs=split_inputs),
        out_shape=[jax.ShapeDtypeStruct((b, l, d), F32), jax.ShapeDtypeStruct((b, l, d), BF16),
                   jax.ShapeDtypeStruct((N_EXPERTS, b * l), F32)],
        grid=(b // kb, n_l),
        in_specs=[tok(half), tok(b_half.shape[2]), const(w), tok(d), per_batch, const(g2), per_batch, per_batch, const(rwt)],
        out_specs=[tok(d), tok(d), pl.BlockSpec((N_EXPERTS, kb * tl), lambda i, j: (0, i * n_l + j))],
        compiler_params=_params("parallel", "parallel"),
        name="outproj_residual",
    )(a, b_half, w, x, gate, g2, shift2, scale2, rwt)


def _beats(a, b, a_first):
    return jnp.where(a > b, 1, jnp.where(a == b, a_first, 0))


def _router_kernel(logit_ref, bias_ref, gate_ref, pos_ref, count_ref):
    tm = logit_ref.shape[1]
    scores = _sigmoid(logit_ref[...])
    sel = scores + bias_ref[...]
    ng, eg = N_EXPERT_GROUPS, EXPERTS_PER_GROUP
    sub = lax.broadcasted_iota(I32, (eg, tm), 0)
    groups, grp = [], []
    for g in range(ng):
        xg = sel[g * eg:(g + 1) * eg]
        m1 = jnp.max(xg, axis=0, keepdims=True)
        first = jnp.min(jnp.where(xg == m1, sub, eg), axis=0, keepdims=True)
        m2 = jnp.max(jnp.where(sub == first, -jnp.inf, xg), axis=0, keepdims=True)
        groups.append(xg)
        grp.append(m1 + m2)
    masked = []
    for g in range(ng):
        rank = sum(_beats(grp[o], grp[g], 1 if o < g else 0) for o in range(ng) if o != g)
        masked.append(jnp.where(rank < TOPK_GROUPS, groups[g], -jnp.inf))
    selm = jnp.concatenate(masked, axis=0)
    erow = lax.broadcasted_iota(I32, (N_EXPERTS, tm), 0)
    rank = jnp.zeros((N_EXPERTS, tm), I32)
    for o in range(N_EXPERTS):
        rank = rank + _beats(selm[o:o + 1], selm, jnp.where(erow > o, 1, 0))
    gate = jnp.concatenate(
        [jnp.sum(jnp.where(rank == k, scores, 0.0), axis=0, keepdims=True) for k in range(TOP_K)], axis=0)
    gate_ref[...] = gate / jnp.sum(gate, axis=0, keepdims=True) * ROUTED_SCALE
    chosen = jnp.where(rank < TOP_K, 1.0, 0.0)
    earlier = jnp.where(lax.broadcasted_iota(I32, (tm, tm), 0) < lax.broadcasted_iota(I32, (tm, tm), 1), 1.0, 0.0)
    before = _mm(chosen, earlier)
    count = jnp.sum(chosen, axis=1, keepdims=True)
    padded = jnp.ceil(count * (1.0 / SEG_ALIGN)) * SEG_ALIGN
    lower = jnp.where(lax.broadcasted_iota(I32, (N_EXPERTS, N_EXPERTS), 1) < lax.broadcasted_iota(I32, (N_EXPERTS, N_EXPERTS), 0), 1.0, 0.0)
    seg_start = _mm(lower, jnp.broadcast_to(padded, (N_EXPERTS, LANES)))[:, :1]
    slot = seg_start + before
    pos_ref[...] = jnp.concatenate(
        [jnp.sum(jnp.where(rank == k, slot, 0.0), axis=0, keepdims=True) for k in range(TOP_K)], axis=0).astype(I32)
    count_ref[0] = jnp.broadcast_to(count, (N_EXPERTS, LANES)).astype(I32)


def _router(logits, router_b):
    t = logits.shape[1]
    n_tiles = t // ROW_TILE
    bias = router_b.astype(F32)[:, None]
    tile = lambda rows: pl.BlockSpec((rows, ROW_TILE), lambda i: (0, i))
    return pl.pallas_call(
        _router_kernel,
        out_shape=[jax.ShapeDtypeStruct((TOP_K, t), F32), jax.ShapeDtypeStruct((TOP_K, t), I32),
                   jax.ShapeDtypeStruct((n_tiles, N_EXPERTS, LANES), I32)],
        grid=(n_tiles,),
        in_specs=[tile(N_EXPERTS), _const_spec(bias)],
        out_specs=[tile(TOP_K), tile(TOP_K), pl.BlockSpec((1, N_EXPERTS, LANES), lambda i: (i, 0, 0))],
        compiler_params=_params("parallel"),
        name="moe_router",
    )(logits, bias)


SEG_ALIGN = 16
SEG_CHUNK = 32
EXPERT_BLOCK = 512
ONEHOT_CHUNK = 256
XS_COLS = D_MODEL + LANES
LOCAL_ROWS = -(-(TOP_K * ROW_TILE + N_EXPERTS * (SEG_ALIGN - 1)) // ONEHOT_CHUNK) * ONEHOT_CHUNK


def _segment_dmas(n_rows, src_ref, src_row, dst_ref, dst_row, sem, wait):
    def dma(rows, s, d):
        cp = pltpu.make_async_copy(src_ref.at[pl.ds(pl.multiple_of(s, SEG_ALIGN), rows)],
                                   dst_ref.at[pl.ds(pl.multiple_of(d, SEG_ALIGN), rows)], sem)
        cp.wait() if wait else cp.start()

    n_full = lax.shift_right_logical(n_rows, SEG_CHUNK.bit_length() - 1)

    def bulk(q, carry):
        dma(SEG_CHUNK, src_row + q * SEG_CHUNK, dst_row + q * SEG_CHUNK)
        return carry

    lax.fori_loop(0, n_full, bulk, 0)
    off = n_full * SEG_CHUNK
    size = SEG_CHUNK // 2
    while size >= SEG_ALIGN:
        has = (n_rows & size) != 0

        @pl.when(has)
        def _(size=size, off=off):
            dma(size, src_row + off, dst_row + off)

        off = off + jnp.where(has, size, 0)
        size //= 2


def _tile_segments(tile, cnt_ref, lstart_ref, gstart_ref, local_ref, global_ref, sem, to_global, wait):
    base = tile * N_EXPERTS

    def seg(e, carry):
        n, lo, go = cnt_ref[base + e], lstart_ref[base + e], gstart_ref[base + e]
        if to_global:
            _segment_dmas(n, local_ref, lo, global_ref, go, sem, wait)
        else:
            _segment_dmas(n, global_ref, go, local_ref, lo, sem, wait)
        return carry

    lax.fori_loop(0, N_EXPERTS, seg, 0)


def _zero_fill(tail_rows_ref, tail_start_ref, used_ref, zero_ref, xs_ref, sem, wait):
    def tail(e, carry):
        _segment_dmas(tail_rows_ref[e], zero_ref, 0, xs_ref, tail_start_ref[e], sem, wait)
        return carry

    lax.fori_loop(0, N_EXPERTS, tail, 0)

    def block(j, carry):
        cp = pltpu.make_async_copy(zero_ref, xs_ref.at[pl.ds(pl.multiple_of(j * EXPERT_BLOCK, EXPERT_BLOCK), EXPERT_BLOCK)], sem)
        cp.wait() if wait else cp.start()
        return carry

    lax.fori_loop(used_ref[0], xs_ref.shape[0] // EXPERT_BLOCK, block, 0)


def _dispatch_kernel(cnt_ref, lstart_ref, gstart_ref, tail_rows_ref, tail_start_ref, used_ref,
                     h_ref, pos_ref, gate_ref, xs_ref, loc_ref, zero_ref, sem):
    tile = pl.program_id(0) * pl.num_programs(1) + pl.program_id(1)
    last = pl.num_programs(0) * pl.num_programs(1) - 1
    slot = tile & 1
    kb, tl, d = h_ref.shape
    tm = kb * tl
    h = h_ref[...].reshape(tm, d)
    pos, gate = pos_ref[...], gate_ref[...]
    row = lax.broadcasted_iota(I32, (ONEHOT_CHUNK, tm), 0)
    lane = lax.broadcasted_iota(I32, (ONEHOT_CHUNK, LANES), 1)
    used_rows = lstart_ref[(tile + 1) * N_EXPERTS - 1] + cnt_ref[(tile + 1) * N_EXPERTS - 1]
    for c in range(LOCAL_ROWS // ONEHOT_CHUNK):
        @pl.when(c * ONEHOT_CHUNK < used_rows)
        def _(c=c):
            gsel = jnp.zeros((ONEHOT_CHUNK, tm), F32)
            for k in range(TOP_K):
                gsel = jnp.where((pos[k:k + 1, :] - c * ONEHOT_CHUNK) == row, gate[k:k + 1, :], gsel)
            onehot = jnp.where(gsel != 0.0, 1.0, 0.0)
            rows = jnp.dot(onehot.astype(BF16), h, preferred_element_type=F32)
            g1, g2, g3 = (t.astype(F32) for t in _split3_bf16(jnp.sum(gsel, axis=1, keepdims=True)))
            sl = slice(c * ONEHOT_CHUNK, (c + 1) * ONEHOT_CHUNK)
            loc_ref[slot, sl, :d] = rows.astype(BF16)
            loc_ref[slot, sl, d:] = jnp.where(lane == 0, g1, jnp.where(lane == 1, g2, jnp.where(lane == 2, g3, 0.0))).astype(BF16)

    @pl.when(tile > 0)
    def _():
        _tile_segments(tile - 1, cnt_ref, lstart_ref, gstart_ref, loc_ref.at[1 - slot], xs_ref, sem, to_global=True, wait=True)

    _tile_segments(tile, cnt_ref, lstart_ref, gstart_ref, loc_ref.at[slot], xs_ref, sem, to_global=True, wait=False)

    @pl.when(tile == last)
    def _():
        zero_ref[...] = jnp.zeros(zero_ref.shape, zero_ref.dtype)
        _zero_fill(tail_rows_ref, tail_start_ref, used_ref, zero_ref, xs_ref, sem, wait=False)
        _tile_segments(tile, cnt_ref, lstart_ref, gstart_ref, loc_ref.at[slot], xs_ref, sem, to_global=True, wait=True)
        _zero_fill(tail_rows_ref, tail_start_ref, used_ref, zero_ref, xs_ref, sem, wait=True)


def _moe_dispatch(h, pos, gate, plan):
    b, l, d = h.shape
    kb, tl = _row_blocking(b, l)
    n_l = l // tl
    tok = pl.BlockSpec((kb, tl, d), lambda i, j, *_: (i, j, 0))
    per_choice = pl.BlockSpec((TOP_K, kb * tl), lambda i, j, *_: (0, i * n_l + j))
    return pl.pallas_call(
        _dispatch_kernel,
        out_shape=jax.ShapeDtypeStruct((plan['cap'], XS_COLS), BF16),
        grid_spec=pltpu.PrefetchScalarGridSpec(
            num_scalar_prefetch=6,
            grid=(b // kb, n_l),
            in_specs=[tok, per_choice, per_choice],
            out_specs=pl.BlockSpec(memory_space=pl.ANY),
            scratch_shapes=[pltpu.VMEM((2, LOCAL_ROWS, XS_COLS), BF16), pltpu.VMEM((EXPERT_BLOCK, XS_COLS), BF16),
                            pltpu.SemaphoreType.DMA]),
        compiler_params=_params("arbitrary", "arbitrary"),
        name="moe_dispatch",
    )(plan['seg_rows'], plan['local_start'], plan['global_start'], plan['tail_rows'], plan['tail_start'],
      plan['blocks_used'], h, pos, gate)


def _swiglu(x, w1, w3, w2):
    hid = _silu(jnp.dot(x, w1, preferred_element_type=F32)) * jnp.dot(x, w3, preferred_element_type=F32)
    return jnp.dot(hid.astype(BF16), w2, preferred_element_type=F32)


def _expert_block_kernel(be_ref, nused_ref, x_ref, w1_ref, w3_ref, w2_ref, y_ref, w1_sc, w3_sc, w2_sc):
    j = pl.program_id(0)

    @pl.when(jnp.logical_or(j == 0, be_ref[j] != be_ref[jnp.maximum(j - 1, 0)]))
    def _():
        w1_sc[...] = w1_ref[0, 0].astype(BF16)
        w3_sc[...] = w3_ref[0, 0].astype(BF16)
        w2_sc[...] = w2_ref[0, 0].astype(BF16)

    @pl.when(j < nused_ref[0])
    def _():
        g = x_ref[:, D_MODEL:].astype(F32)
        gate = (g[:, 0:1] + g[:, 1:2]) + g[:, 2:3]
        out = _swiglu(x_ref[:, :D_MODEL], w1_sc[...], w3_sc[...], w2_sc[...])
        y_ref[...] = (out * gate).astype(y_ref.dtype)

    @pl.when(j >= nused_ref[0])
    def _():
        y_ref[...] = jnp.zeros(y_ref.shape, y_ref.dtype)


def _expert_blocks(xs, plan, experts):
    layer, w1, w3, w2 = experts
    cap = xs.shape[0]
    d = D_MODEL
    weight = lambda shape: pl.BlockSpec((1, 1) + shape, lambda j, be, nu: (layer, be[j], 0, 0))
    return pl.pallas_call(
        _expert_block_kernel,
        out_shape=jax.ShapeDtypeStruct((cap, d), BF16),
        grid_spec=pltpu.PrefetchScalarGridSpec(
            num_scalar_prefetch=2,
            grid=(cap // EXPERT_BLOCK,),
            in_specs=[pl.BlockSpec((EXPERT_BLOCK, XS_COLS), lambda j, *_: (j, 0)),
                      weight((d, D_EXPERT)), weight((d, D_EXPERT)), weight((D_EXPERT, d))],
            out_specs=pl.BlockSpec((EXPERT_BLOCK, d), lambda j, *_: (j, 0)),
            scratch_shapes=[pltpu.VMEM((d, D_EXPERT), BF16), pltpu.VMEM((d, D_EXPERT), BF16),
                            pltpu.VMEM((D_EXPERT, d), BF16)]),
        compiler_params=_params("arbitrary"),
        name="moe_expert_blocks",
    )(plan['block_expert'], plan['blocks_used'], xs, w1, w3, w2)


def _combine_kernel(cnt_ref, lstart_ref, gstart_ref, ys_ref, pos_ref, h_ref, x_ref, gmod_ref, w1_ref, w3_ref, w2_ref,
                    o_ref, loc_ref, onehot_ref, sem):
    tile = pl.program_id(0) * pl.num_programs(1) + pl.program_id(1)
    kb, tl, d = x_ref.shape
    tm = kb * tl

    @pl.when(tile == 0)
    def _():
        loc_ref[...] = jnp.zeros(loc_ref.shape, loc_ref.dtype)

    _tile_segments(tile, cnt_ref, lstart_ref, gstart_ref, loc_ref, ys_ref, sem, to_global=False, wait=False)
    shared = _swiglu(h_ref[...].reshape(tm, d), w1_ref[...], w3_ref[...], w2_ref[...])
    _tile_segments(tile, cnt_ref, lstart_ref, gstart_ref, loc_ref, ys_ref, sem, to_global=False, wait=True)
    pos = pos_ref[...]
    col = lax.broadcasted_iota(I32, (tm, ONEHOT_CHUNK), 1)
    used_rows = lstart_ref[(tile + 1) * N_EXPERTS - 1] + cnt_ref[(tile + 1) * N_EXPERTS - 1]
    for c in range(LOCAL_ROWS // ONEHOT_CHUNK):
        cols = slice(c * ONEHOT_CHUNK, (c + 1) * ONEHOT_CHUNK)

        @pl.when(c * ONEHOT_CHUNK < used_rows)
        def _(c=c, cols=cols):
            onehot = jnp.zeros((tm, ONEHOT_CHUNK), F32)
            for k in range(TOP_K):
                onehot = jnp.where((pos[:, k:k + 1] - c * ONEHOT_CHUNK) == col, 1.0, onehot)
            onehot_ref[:, cols] = onehot.astype(BF16)

        @pl.when(c * ONEHOT_CHUNK >= used_rows)
        def _(cols=cols):
            onehot_ref[:, cols] = jnp.zeros((tm, ONEHOT_CHUNK), BF16)

    routed = jnp.dot(onehot_ref[...], loc_ref[...], preferred_element_type=F32)
    o_ref[...] = x_ref[...] + gmod_ref[...] * (routed + shared).reshape(kb, tl, d)


def _moe_combine(ys, pos_t, h, x, gate_mod, plan, sw1, sw3, sw2):
    b, l, d = x.shape
    kb, tl = _row_blocking(b, l)
    n_l = l // tl
    tok = pl.BlockSpec((kb, tl, d), lambda i, j, *_: (i, j, 0))
    per_batch = pl.BlockSpec((kb, 1, d), lambda i, j, *_: (i, 0, 0))
    return pl.pallas_call(
        _combine_kernel,
        out_shape=jax.ShapeDtypeStruct((b, l, d), F32),
        grid_spec=pltpu.PrefetchScalarGridSpec(
            num_scalar_prefetch=3,
            grid=(b // kb, n_l),
            in_specs=[pl.BlockSpec(memory_space=pl.ANY),
                      pl.BlockSpec((kb * tl, TOP_K), lambda i, j, *_: (i * n_l + j, 0)),
                      tok, tok, per_batch, _const_spec(sw1), _const_spec(sw3), _const_spec(sw2)],
            out_specs=tok,
            scratch_shapes=[pltpu.VMEM((LOCAL_ROWS, D_MODEL), BF16), pltpu.VMEM((kb * tl, LOCAL_ROWS), BF16),
                            pltpu.SemaphoreType.DMA]),
        compiler_params=_params("arbitrary", "arbitrary"),
        name="moe_combine",
    )(plan['seg_rows'], plan['local_start'], plan['global_start'], ys, pos_t, h, x, gate_mod, sw1, sw3, sw2)


def _dispatch_plan(counts, n_tok):
    n_tiles = counts.shape[0]
    seg = (counts + SEG_ALIGN - 1) // SEG_ALIGN * SEG_ALIGN
    local_start = jnp.cumsum(seg, axis=1) - seg
    region = jnp.sum(seg, axis=0)
    region_pad = (region + EXPERT_BLOCK - 1) // EXPERT_BLOCK * EXPERT_BLOCK
    region_end = jnp.cumsum(region_pad)
    region_beg = region_end - region_pad
    global_start = region_beg[None, :] + jnp.cumsum(seg, axis=0) - seg
    worst = TOP_K * n_tok + N_EXPERTS * (SEG_ALIGN - 1) * n_tiles + N_EXPERTS * (EXPERT_BLOCK - 1)
    cap = -(-worst // EXPERT_BLOCK) * EXPERT_BLOCK
    block_row = jnp.arange(cap // EXPERT_BLOCK, dtype=I32) * EXPERT_BLOCK
    block_expert = jnp.minimum(jnp.sum((region_end[None, :] <= block_row[:, None]).astype(I32), axis=1), N_EXPERTS - 1)
    flat = lambda a: a.reshape(-1).astype(I32)
    return dict(cap=cap, seg_rows=flat(seg), local_start=flat(local_start), global_start=flat(global_start),
                tail_rows=flat(region_pad - region), tail_start=flat(region_beg + region),
                block_expert=flat(block_expert), blocks_used=flat(region_end[-1:] // EXPERT_BLOCK))


def _moe(h, logits, x, gate_mod, router_b, experts, sw1, sw3, sw2):
    b, l, d = x.shape
    gate, pos, counts = _router(logits, router_b)
    plan = _dispatch_plan(counts[:, :, 0], b * l)
    xs = _moe_dispatch(h, pos, gate, plan)
    ys = _expert_blocks(xs, plan, experts)
    return _moe_combine(ys, pos.T, h, x, gate_mod, plan, sw1, sw3, sw2)


def _split_mod(m):
    m = m[:, None, :]
    return m[..., :D_MODEL], m[..., D_MODEL:2 * D_MODEL], m[..., 2 * D_MODEL:]


def _even_layer(x, mod_a, mod_b, past, w):
    b, l, _ = x.shape
    shift, scale, gate = _split_mod(mod_a)
    q3, k3, k, v, z, xbc, small = _inproj_call(
        _inproj_even_kernel, "inproj_even", x, w['norm_g0'], shift, scale, w['w_in'],
        [w['head_mean'], w['q_gain'], w['k_gain'], w['small_bias']],
        [FOX_HEADS * FOX_HD3, FOX_HEADS * FOX_HD3, FOX_DIM, FOX_DIM, SSD_INNER, SSD_CONV_DIM, LANES],
        [BF16, BF16, F32, F32, F32, F32, F32])
    if past['fox_k'] is None:
        cum, cumt = _fox_cumulative(small, None)
        fox = _fox_attention_prompt(q3, k3, v, cum, cumt)
    else:
        n_past = past['fox_k'].shape[1]
        past_logf = jnp.pad(past['fox_logf'].astype(F32), ((0, 0), (0, 0), (0, LANES - FOX_HEADS)))
        cum, cumt = _fox_cumulative(small, past_logf)
        fox = _fox_attention_decode(q3, k3, v, past['fox_k'].reshape(b, n_past, FOX_DIM),
                                    past['fox_v'].reshape(b, n_past, FOX_DIM), cum, cumt)
    y, conv_new, ssd_new = _ssd_mixer(xbc, z, small, past['ssd_conv'], past['ssd'], w['ssd_conv_w'], w['ssd_conv_b'],
                                      w['ssd_A_log'], w['ssd_D'], w['ssd_norm_g'])
    shift2, scale2, _ = _split_mod(mod_b)
    x, h, logits = _outproj_residual(fox, y, w['w_out'], x, gate, w['norm_g1'], shift2, scale2, w['router_w'],
                                     split_inputs=True)
    new = dict(fox_k=k.reshape(b, l, FOX_HEADS, FOX_HD), fox_v=v.reshape(b, l, FOX_HEADS, FOX_HD),
               fox_logf=small[..., :FOX_HEADS], ssd_conv=conv_new, ssd=ssd_new)
    return x, h, logits, new


def _odd_layer(x, mod_a, mod_b, past, w):
    shift, scale, gate = _split_mod(mod_a)
    rw, qkv, gz, small = _inproj_call(
        _inproj_odd_kernel, "inproj_odd", x, w['norm_g0'], shift, scale, w['w_in'],
        [w['small_bias'], w['small_alog']], [RWKV_PROJ, GDN_CONV_DIM, GDN_DIM, LANES], [F32, F32, F32, F32])
    o_rwkv, rwkv_new = _rwkv_mixer(rw, past['rwkv_shift'], past['rwkv'], w['p'])
    o_gdn, gconv_new, gdn_new = _gdn_mixer(qkv, gz, small, past['gdn_conv'], past['gdn'], w['p']['gdn_conv_w'],
                                           w['p']['gdn_norm_g'])
    shift2, scale2, _ = _split_mod(mod_b)
    x, h, logits = _outproj_residual(o_rwkv, o_gdn, w['w_out'], x, gate, w['norm_g1'], shift2, scale2, w['router_w'],
                                     split_inputs=False)
    new = dict(rwkv_shift=rw[:, -1], rwkv=rwkv_new, gdn_conv=gconv_new, gdn=gdn_new)
    return x, h, logits, new


def _run_trunk(x, mods, past_even, past_odd, layers):
    new_even, new_odd = [], []
    for i, w in enumerate(layers):
        j = i // 2
        if i % 2 == 0:
            past = {n: (None if a is None else a[j]) for n, a in past_even.items()}
            x, h, logits, st = _even_layer(x, mods[2 * i], mods[2 * i + 1], past, w)
            new_even.append(st)
        else:
            x, h, logits, st = _odd_layer(x, mods[2 * i], mods[2 * i + 1], {n: a[j] for n, a in past_odd.items()}, w)
            new_odd.append(st)
        x = _moe(h, logits, x, _split_mod(mods[2 * i + 1])[2], *w['moe'])
    stack = lambda lst: {n: jnp.stack([d[n] for d in lst]) for n in lst[0]}
    return x, stack(new_even), stack(new_odd)


def kernel(x_prompt, x_sample, cache_fox_k, cache_fox_v, cache_fox_logf, state_ssd_conv, state_ssd, state_rwkv_shift, state_rwkv, state_gdn_conv, state_gdn, c_prompt, c_sample, ada_w, ada_b, norm_g, even_w_in, even_w_out, fox_q_g, fox_k_g, fox_f_b, ssd_conv_w, ssd_conv_b, ssd_dt_bias, ssd_A_log, ssd_D, ssd_norm_g, odd_w_in, odd_w_out, rwkv_mu, rwkv_w0, rwkv_w2, rwkv_a0, rwkv_a2, rwkv_g2, rwkv_k_k, rwkv_k_a, rwkv_r_k, rwkv_ln_g, rwkv_ln_b, gdn_conv_w, gdn_A_log, gdn_dt_bias, gdn_norm_g, router_w, router_b, moe_w1, moe_w3, moe_w2, shared_w1, shared_w3, shared_w2):
    depth = ada_w.shape[0]
    ne, no = cache_fox_k.shape[0], state_rwkv.shape[0]
    bp, bs = x_prompt.shape[0], x_sample.shape[0]
    odd_params = dict(rwkv_mu=rwkv_mu, rwkv_w0=rwkv_w0, rwkv_w2=rwkv_w2, rwkv_a0=rwkv_a0, rwkv_a2=rwkv_a2, rwkv_g2=rwkv_g2,
                      rwkv_k_k=rwkv_k_k, rwkv_k_a=rwkv_k_a, rwkv_r_k=rwkv_r_k, rwkv_ln_g=rwkv_ln_g, rwkv_ln_b=rwkv_ln_b,
                      gdn_conv_w=gdn_conv_w, gdn_norm_g=gdn_norm_g)
    layers = []
    for i in range(depth):
        j = i // 2
        w = dict(norm_g0=norm_g[i, 0][None], norm_g1=norm_g[i, 1][None], router_w=router_w[i],
                 moe=(router_b[i], (i, moe_w1, moe_w3, moe_w2),
                      shared_w1[i].astype(BF16), shared_w3[i].astype(BF16), shared_w2[i].astype(BF16)))
        if i % 2 == 0:
            w.update(w_in=_pack_even_w_in(even_w_in[j]), w_out=even_w_out[j],
                     head_mean=_head_mean_matrix(FOX_DIM, FOX_HD),
                     q_gain=jnp.tile(fox_q_g[j], FOX_HEADS)[None], k_gain=jnp.tile(fox_k_g[j], FOX_HEADS)[None],
                     small_bias=_pad_lanes(fox_f_b[j], ssd_dt_bias[j]),
                     ssd_conv_w=ssd_conv_w[j], ssd_conv_b=ssd_conv_b[j], ssd_A_log=ssd_A_log[j], ssd_D=ssd_D[j],
                     ssd_norm_g=ssd_norm_g[j])
        else:
            w.update(w_in=_pack_odd_w_in(odd_w_in[j]), w_out=odd_w_out[j],
                     small_bias=_pad_lanes(jnp.zeros(GDN_HEADS), gdn_dt_bias[j]),
                     small_alog=_pad_lanes(jnp.zeros(GDN_HEADS), gdn_A_log[j]),
                     p={n: a[j] for n, a in odd_params.items()})
        layers.append(w)

    mods = _ada_modulation(jnp.concatenate([c_prompt, c_sample], axis=0), ada_w, ada_b)
    mods_p, mods_s = mods[:, :bp], mods[:, bp:]

    prompt_even = dict(fox_k=None, fox_v=None, fox_logf=None,
                       ssd_conv=jnp.zeros((ne, bp) + state_ssd_conv.shape[2:], F32),
                       ssd=jnp.zeros((ne, bp) + state_ssd.shape[2:], F32))
    prompt_odd = dict(rwkv_shift=jnp.zeros((no, bp) + state_rwkv_shift.shape[2:], F32),
                      rwkv=jnp.zeros((no, bp) + state_rwkv.shape[2:], F32),
                      gdn_conv=jnp.zeros((no, bp) + state_gdn_conv.shape[2:], F32),
                      gdn=jnp.zeros((no, bp) + state_gdn.shape[2:], F32))
    sample_even = dict(fox_k=cache_fox_k, fox_v=cache_fox_v, fox_logf=cache_fox_logf, ssd_conv=state_ssd_conv, ssd=state_ssd)
    sample_odd = dict(rwkv_shift=state_rwkv_shift, rwkv=state_rwkv, gdn_conv=state_gdn_conv, gdn=state_gdn)
    y_prompt, pe, po = _run_trunk(x_prompt, mods_p, prompt_even, prompt_odd, layers)
    y_sample, se, so = _run_trunk(x_sample, mods_s, sample_even, sample_odd, layers)
    return (y_prompt, y_sample,
            pe['fox_k'], pe['fox_v'], pe['fox_logf'], pe['ssd_conv'], pe['ssd'],
            po['rwkv_shift'], po['rwkv'], po['gdn_conv'], po['gdn'],
            se['fox_k'], se['fox_v'], se['fox_logf'], se['ssd_conv'], se['ssd'],
            so['rwkv_shift'], so['rwkv'], so['gdn_conv'], so['gdn'])
```
